```python
import math
import jax, jax.numpy as jnp
from jax import lax
import numpy as np

D_MODEL = 4096
BATCH = 8
SEQ = 4096
DEPTH = 1

N_META = 16
HEAD_DIM = 64
N_Q_HEADS = 32
N_KV_HEADS = 4
Q_PER_KV = N_Q_HEADS // N_KV_HEADS
ATTN_WIDTH = N_Q_HEADS * HEAD_DIM
KV_WIDTH = N_KV_HEADS * HEAD_DIM
WINDOW = 128
BLOCK = 128
ATTN_SCALE = HEAD_DIM ** -0.5
ROPE_DIM = HEAD_DIM // 4
ROPE_THETA = 500000.0
NEG_INF = -1e30
POOL_WINDOWS = (2, 4, 8, 16)
N_POOL_GROUPS = len(POOL_WINDOWS)
POOL_WIDTH = D_MODEL // 2
POOL_GROUP_WIDTH = POOL_WIDTH // N_POOL_GROUPS
N_BRANCHES = 2
IN_WIDTH = ATTN_WIDTH + 2 * KV_WIDTH + POOL_WIDTH + N_BRANCHES * D_MODEL
SPLITS = [ATTN_WIDTH, ATTN_WIDTH + KV_WIDTH, ATTN_WIDTH + 2 * KV_WIDTH,
          ATTN_WIDTH + 2 * KV_WIDTH + POOL_WIDTH]
D_FF = -(-8 * D_MODEL // 768) * 256
DN_ALPHA = (2 * DEPTH) ** 0.25
DN_BETA = (8 * DEPTH) ** -0.25
LN_EPS = 1e-5

kernel_name = "hybrid_swa_sinks_multiscale_pool_gated_deepnorm"


def layer_norm(x, g, b):
    xf = x.astype(jnp.float32)
    mu = xf.mean(-1, keepdims=True)
    var = jnp.square(xf - mu).mean(-1, keepdims=True)
    y = (xf - mu) * lax.rsqrt(var + LN_EPS)
    return (y * g.astype(jnp.float32) + b.astype(jnp.float32)).astype(x.dtype)


def partial_rope(x, pos):
    half = ROPE_DIM // 2
    inv_freq = ROPE_THETA ** (-jnp.arange(half, dtype=jnp.float32) * 2.0 / ROPE_DIM)
    ang = pos.astype(jnp.float32)[:, None] * inv_freq[None, :]
    cos = jnp.cos(ang)[None, :, None, :]
    sin = jnp.sin(ang)[None, :, None, :]
    xr = x[..., :ROPE_DIM].astype(jnp.float32)
    x1, x2 = xr[..., :half], xr[..., half:]
    rot = jnp.concatenate([x1 * cos - x2 * sin, x2 * cos + x1 * sin], axis=-1).astype(x.dtype)
    return jnp.concatenate([rot, x[..., ROPE_DIM:]], axis=-1)


def _band_blocks(a, nb):
    B = a.shape[0]
    ab = a.reshape(B, nb, BLOCK, N_KV_HEADS, HEAD_DIM)
    prev = jnp.pad(ab[:, :-1], ((0, 0), (1, 0), (0, 0), (0, 0), (0, 0)))
    return jnp.concatenate([prev, ab], axis=2)


def sliding_window_attention(q, k, v, sinks):
    B, T = q.shape[0], q.shape[1]
    lead = (-N_META) % BLOCK
    tail = (-(lead + T)) % BLOCK
    Tp = lead + T + tail
    nb = Tp // BLOCK
    pad = ((0, 0), (lead, tail), (0, 0), (0, 0))
    qb = jnp.pad(q, pad).reshape(B, nb, BLOCK, N_KV_HEADS, Q_PER_KV, HEAD_DIM)
    kb = _band_blocks(jnp.pad(k, pad), nb)
    vb = _band_blocks(jnp.pad(v, pad), nb)
    k_meta, v_meta = k[:, :N_META], v[:, :N_META]

    s_band = jnp.einsum('bnqkgd,bnskd->bnkgqs', qb, kb).astype(jnp.float32) * ATTN_SCALE
    s_meta = jnp.einsum('bnqkgd,bmkd->bnkgqm', qb, k_meta).astype(jnp.float32) * ATTN_SCALE

    blk = jnp.arange(nb)
    q_idx = blk[:, None] * BLOCK + jnp.arange(BLOCK)[None, :]
    k_idx = (blk[:, None] - 1) * BLOCK + jnp.arange(2 * BLOCK)[None, :]
    diff = q_idx[:, :, None] - k_idx[:, None, :]
    band_ok = (diff >= 0) & (diff < WINDOW) & (k_idx[:, None, :] >= lead + N_META)
    meta_ok = q_idx[:, :, None] >= lead + jnp.arange(N_META)[None, None, :]
    s_band = jnp.where(band_ok[None, :, None, None], s_band, NEG_INF)
    s_meta = jnp.where(meta_ok[None, :, None, None], s_meta, NEG_INF)

    sink = sinks.astype(jnp.float32).reshape(N_KV_HEADS, Q_PER_KV)[None, None, :, :, None, None]
    m = jnp.maximum(jnp.maximum(s_band.max(-1, keepdims=True), s_meta.max(-1, keepdims=True)), sink)
    p_band = jnp.exp(s_band - m)
    p_meta = jnp.exp(s_meta - m)
    inv = 1.0 / (p_band.sum(-1, keepdims=True) + p_meta.sum(-1, keepdims=True) + jnp.exp(sink - m))
    o = (jnp.einsum('bnkgqs,bnskd->bnqkgd', (p_band * inv).astype(v.dtype), vb)
         + jnp.einsum('bnkgqm,bmkd->bnqkgd', (p_meta * inv).astype(v.dtype), v_meta))
    return o.reshape(B, Tp, ATTN_WIDTH)[:, lead:lead + T]


def multiscale_pool(u, w_grp, scale):
    B, T = u.shape[0], u.shape[1]
    ug = u.reshape(B, T, N_POOL_GROUPS, POOL_GROUP_WIDTH)
    cs = jnp.cumsum(ug.astype(jnp.float32), axis=1)
    cs = jnp.pad(cs, ((0, 0), (1, 0), (0, 0), (0, 0)))
    t = jnp.arange(T)
    outs = []
    for g, w in enumerate(POOL_WINDOWS):
        csg = cs[:, :, g]
        start = jnp.maximum(t + 1 - w, 0)
        win_sum = csg[:, 1:] - csg[:, start]
        count = jnp.minimum(t + 1, w).astype(jnp.float32)[None, :, None]
        outs.append(win_sum / count - ug[:, :, g].astype(jnp.float32))
    pooled = jnp.stack(outs, axis=2).astype(u.dtype)
    mixed = jnp.einsum('btgc,gcd->btgd', pooled, w_grp)
    return mixed.reshape(B, T, POOL_WIDTH) * scale


def _fwd_setup_inputs(seed: int = 0) -> dict:
    key = jax.random.key(seed)
    ks = jax.random.split(key, 20)
    f32 = jnp.float32

    def nrm(k, shape, s):
        return jax.random.normal(k, shape, f32) * s

    return {
        "x": nrm(ks[0], (BATCH, SEQ, D_MODEL), 1.0),
        "meta_tokens": nrm(ks[1], (N_META, D_MODEL), 1.0),
        "ln_in_g": 1.0 + nrm(ks[2], (D_MODEL,), 0.02),
        "ln_in_b": nrm(ks[3], (D_MODEL,), 0.02),
        "w_in": nrm(ks[4], (DEPTH, D_MODEL, IN_WIDTH), D_MODEL ** -0.5),
        "b_gate": nrm(ks[5], (DEPTH, N_BRANCHES, D_MODEL), 0.1),
        "attn_sinks": nrm(ks[6], (DEPTH, N_Q_HEADS), 0.5),
        "w_attn_up": nrm(ks[7], (DEPTH, ATTN_WIDTH, D_MODEL), ATTN_WIDTH ** -0.5),
        "w_pool_grp": nrm(ks[8], (DEPTH, N_POOL_GROUPS, POOL_GROUP_WIDTH, POOL_GROUP_WIDTH), POOL_GROUP_WIDTH ** -0.5),
        "pool_scale": 1.0 + nrm(ks[9], (DEPTH, POOL_WIDTH), 0.02),
        "w_pool_up": nrm(ks[10], (DEPTH, POOL_WIDTH, D_MODEL), POOL_WIDTH ** -0.5),
        "w_out": nrm(ks[11], (DEPTH, D_MODEL, D_MODEL), DN_BETA * D_MODEL ** -0.5),
        "ln1_g": 1.0 + nrm(ks[12], (DEPTH, D_MODEL), 0.02),
        "ln1_b": nrm(ks[13], (DEPTH, D_MODEL), 0.02),
        "w_ffn_in": nrm(ks[14], (DEPTH, D_MODEL, 2 * D_FF), D_MODEL ** -0.5),
        "w_ffn_down": nrm(ks[15], (DEPTH, D_FF, D_MODEL), DN_BETA * D_FF ** -0.5),
        "ln2_g": 1.0 + nrm(ks[16], (DEPTH, D_MODEL), 0.02),
        "ln2_b": nrm(ks[17], (DEPTH, D_MODEL), 0.02),
    }


def _fwd_reference(x, meta_tokens, ln_in_g, ln_in_b, w_in, b_gate, attn_sinks, w_attn_up,
              w_pool_grp, pool_scale, w_pool_up, w_out, ln1_g, ln1_b, w_ffn_in,
              w_ffn_down, ln2_g, ln2_b):
    B = x.shape[0]
    meta = jnp.broadcast_to(meta_tokens[None].astype(x.dtype), (B, N_META, D_MODEL))
    h = layer_norm(jnp.concatenate([meta, x], axis=1), ln_in_g, ln_in_b)
    T = h.shape[1]
    pos = jnp.arange(T)

    for l in range(DEPTH):
        proj = h @ w_in[l]
        q, k, v, u, gate_logits = jnp.split(proj, SPLITS, axis=-1)
        q = partial_rope(q.reshape(B, T, N_Q_HEADS, HEAD_DIM), pos)
        k = partial_rope(k.reshape(B, T, N_KV_HEADS, HEAD_DIM), pos)
        v = v.reshape(B, T, N_KV_HEADS, HEAD_DIM)

        a_out = sliding_window_attention(q, k, v, attn_sinks[l]) @ w_attn_up[l]
        p_out = multiscale_pool(u, w_pool_grp[l], pool_scale[l]) @ w_pool_up[l]

        gates = jax.nn.sigmoid(gate_logits.reshape(B, T, N_BRANCHES, D_MODEL) + b_gate[l])
        mixed = gates[:, :, 0] * a_out + gates[:, :, 1] * p_out
        h = layer_norm(DN_ALPHA * h + mixed @ w_out[l], ln1_g[l], ln1_b[l])

        f_gate, f_up = jnp.split(h @ w_ffn_in[l], 2, axis=-1)
        ffn = (jax.nn.silu(f_gate) * f_up) @ w_ffn_down[l]
        h = layer_norm(DN_ALPHA * h + ffn, ln2_g[l], ln2_b[l])

    return h[:, N_META:]


import jax as _jax
import jax.numpy as _jnp

TWIN_FORMAT = 'train_step'
FWD_PARAMS = ['x', 'meta_tokens', 'ln_in_g', 'ln_in_b', 'w_in', 'b_gate', 'attn_sinks', 'w_attn_up', 'w_pool_grp', 'pool_scale', 'w_pool_up', 'w_out', 'ln1_g', 'ln1_b', 'w_ffn_in', 'w_ffn_down', 'ln2_g', 'ln2_b']
TWIN_WEIGHTS = ['meta_tokens', 'ln_in_g', 'ln_in_b', 'w_in', 'b_gate', 'attn_sinks', 'w_attn_up', 'w_pool_grp', 'pool_scale', 'w_pool_up', 'w_out', 'ln1_g', 'ln1_b', 'w_ffn_in', 'w_ffn_down', 'ln2_g', 'ln2_b']
TWIN_DIFF_INPUT = 'x'
TWIN_INPUTS = ['x', 'meta_tokens', 'ln_in_g', 'ln_in_b', 'w_in', 'b_gate', 'attn_sinks', 'w_attn_up', 'w_pool_grp', 'pool_scale', 'w_pool_up', 'w_out', 'ln1_g', 'ln1_b', 'w_ffn_in', 'w_ffn_down', 'ln2_g', 'ln2_b', 'loss_target', 'm_meta_tokens', 'm_ln_in_g', 'm_ln_in_b', 'm_w_in', 'm_b_gate', 'm_attn_sinks', 'm_w_attn_up', 'm_w_pool_grp', 'm_pool_scale', 'm_w_pool_up', 'm_w_out', 'm_ln1_g', 'm_ln1_b', 'm_w_ffn_in', 'm_w_ffn_down', 'm_ln2_g', 'm_ln2_b', 'v_meta_tokens', 'v_ln_in_g', 'v_ln_in_b', 'v_w_in', 'v_b_gate', 'v_attn_sinks', 'v_w_attn_up', 'v_w_pool_grp', 'v_pool_scale', 'v_w_pool_up', 'v_w_out', 'v_ln1_g', 'v_ln1_b', 'v_w_ffn_in', 'v_w_ffn_down', 'v_ln2_g', 'v_ln2_b']
TWIN_OUTPUTS = ['loss', 'grad_x', 'grad_meta_tokens', 'grad_ln_in_g', 'grad_ln_in_b', 'grad_w_in', 'grad_b_gate', 'grad_attn_sinks', 'grad_w_attn_up', 'grad_w_pool_grp', 'grad_pool_scale', 'grad_w_pool_up', 'grad_w_out', 'grad_ln1_g', 'grad_ln1_b', 'grad_w_ffn_in', 'grad_w_ffn_down', 'grad_ln2_g', 'grad_ln2_b', 'delta_meta_tokens', 'delta_ln_in_g', 'delta_ln_in_b', 'delta_w_in', 'delta_b_gate', 'delta_attn_sinks', 'delta_w_attn_up', 'delta_w_pool_grp', 'delta_pool_scale', 'delta_w_pool_up', 'delta_w_out', 'delta_ln1_g', 'delta_ln1_b', 'delta_w_ffn_in', 'delta_w_ffn_down', 'delta_ln2_g', 'delta_ln2_b', 'new_m_meta_tokens', 'new_m_ln_in_g', 'new_m_ln_in_b', 'new_m_w_in', 'new_m_b_gate', 'new_m_attn_sinks', 'new_m_w_attn_up', 'new_m_w_pool_grp', 'new_m_pool_scale', 'new_m_w_pool_up', 'new_m_w_out', 'new_m_ln1_g', 'new_m_ln1_b', 'new_m_w_ffn_in', 'new_m_w_ffn_down', 'new_m_ln2_g', 'new_m_ln2_b', 'new_v_meta_tokens', 'new_v_ln_in_g', 'new_v_ln_in_b', 'new_v_w_in', 'new_v_b_gate', 'new_v_attn_sinks', 'new_v_w_attn_up', 'new_v_w_pool_grp', 'new_v_pool_scale', 'new_v_w_pool_up', 'new_v_w_out', 'new_v_ln1_g', 'new_v_ln1_b', 'new_v_w_ffn_in', 'new_v_w_ffn_down', 'new_v_ln2_g', 'new_v_ln2_b']
TWIN_LEAF_KINDS = {'loss': 'loss', 'grad_x': 'grad_x', 'grad_meta_tokens': 'grad_w', 'grad_ln_in_g': 'grad_w', 'grad_ln_in_b': 'grad_w', 'grad_w_in': 'grad_w', 'grad_b_gate': 'grad_w', 'grad_attn_sinks': 'grad_w', 'grad_w_attn_up': 'grad_w', 'grad_w_pool_grp': 'grad_w', 'grad_pool_scale': 'grad_w', 'grad_w_pool_up': 'grad_w', 'grad_w_out': 'grad_w', 'grad_ln1_g': 'grad_w', 'grad_ln1_b': 'grad_w', 'grad_w_ffn_in': 'grad_w', 'grad_w_ffn_down': 'grad_w', 'grad_ln2_g': 'grad_w', 'grad_ln2_b': 'grad_w', 'delta_meta_tokens': 'delta_w', 'delta_ln_in_g': 'delta_w', 'delta_ln_in_b': 'delta_w', 'delta_w_in': 'delta_w', 'delta_b_gate': 'delta_w', 'delta_attn_sinks': 'delta_w', 'delta_w_attn_up': 'delta_w', 'delta_w_pool_grp': 'delta_w', 'delta_pool_scale': 'delta_w', 'delta_w_pool_up': 'delta_w', 'delta_w_out': 'delta_w', 'delta_ln1_g': 'delta_w', 'delta_ln1_b': 'delta_w', 'delta_w_ffn_in': 'delta_w', 'delta_w_ffn_down': 'delta_w', 'delta_ln2_g': 'delta_w', 'delta_ln2_b': 'delta_w', 'new_m_meta_tokens': 'new_m', 'new_m_ln_in_g': 'new_m', 'new_m_ln_in_b': 'new_m', 'new_m_w_in': 'new_m', 'new_m_b_gate': 'new_m', 'new_m_attn_sinks': 'new_m', 'new_m_w_attn_up': 'new_m', 'new_m_w_pool_grp': 'new_m', 'new_m_pool_scale': 'new_m', 'new_m_w_pool_up': 'new_m', 'new_m_w_out': 'new_m', 'new_m_ln1_g': 'new_m', 'new_m_ln1_b': 'new_m', 'new_m_w_ffn_in': 'new_m', 'new_m_w_ffn_down': 'new_m', 'new_m_ln2_g': 'new_m', 'new_m_ln2_b': 'new_m', 'new_v_meta_tokens': 'new_v', 'new_v_ln_in_g': 'new_v', 'new_v_ln_in_b': 'new_v', 'new_v_w_in': 'new_v', 'new_v_b_gate': 'new_v', 'new_v_attn_sinks': 'new_v', 'new_v_w_attn_up': 'new_v', 'new_v_w_pool_grp': 'new_v', 'new_v_pool_scale': 'new_v', 'new_v_w_pool_up': 'new_v', 'new_v_w_out': 'new_v', 'new_v_ln1_g': 'new_v', 'new_v_ln1_b': 'new_v', 'new_v_w_ffn_in': 'new_v', 'new_v_w_ffn_down': 'new_v', 'new_v_ln2_g': 'new_v', 'new_v_ln2_b': 'new_v'}


def _forward(args):
    return _fwd_reference(*[args[k] for k in FWD_PARAMS])


def _output_shape():
    out = _jax.eval_shape(lambda: _forward(_fwd_setup_inputs(0)))
    return out.shape, out.dtype

N_MICROBATCH = 1
ADAM_LR = 0.001
ADAM_B1 = 0.9
ADAM_B2 = 0.999
ADAM_EPS = 1e-08
ADAM_WD = 0.01
ADAM_STEP = 10
PER_EXAMPLE_BATCH_AXIS = {'x': 0, 'loss_target': 0}
SHARED_INPUTS = []
_WEIGHT_DTYPES = {'meta_tokens': _jnp.float32, 'ln_in_g': _jnp.float32, 'ln_in_b': _jnp.float32, 'w_in': _jnp.float32, 'b_gate': _jnp.float32, 'attn_sinks': _jnp.float32, 'w_attn_up': _jnp.float32, 'w_pool_grp': _jnp.float32, 'pool_scale': _jnp.float32, 'w_pool_up': _jnp.float32, 'w_out': _jnp.float32, 'ln1_g': _jnp.float32, 'ln1_b': _jnp.float32, 'w_ffn_in': _jnp.float32, 'w_ffn_down': _jnp.float32, 'ln2_g': _jnp.float32, 'ln2_b': _jnp.float32}
MOMENT_SCALE = {'meta_tokens': 3.970395e-04, 'ln_in_g': 2.150501e-01, 'ln_in_b': 1.088467e-01, 'w_in': 6.754144e-03, 'b_gate': 3.080787e-03, 'attn_sinks': 4.009434e-04, 'w_attn_up': 2.119024e-03, 'w_pool_grp': 1.505464e-02, 'pool_scale': 1.537476e-02, 'w_pool_up': 1.065972e-02, 'w_out': 1.820674e-02, 'ln1_g': 2.295673e-01, 'ln1_b': 1.078747e-01, 'w_ffn_in': 7.842273e-03, 'w_ffn_down': 2.130493e-02, 'ln2_g': 7.994269e+00, 'ln2_b': 2.008709e-01}


def _to_microbatches(a, axis):
    t = _jnp.moveaxis(a, axis, 0)
    t = t.reshape((N_MICROBATCH, t.shape[0] // N_MICROBATCH) + t.shape[1:])
    return _jnp.moveaxis(t, 1, axis + 1)


def setup_inputs(seed: int = 0) -> dict:
    inp = _fwd_setup_inputs(seed)
    key = _jax.random.fold_in(_jax.random.key(seed), 7919)
    shape, _ = _output_shape()
    out = dict(inp)
    out["loss_target"] = _jax.random.normal(_jax.random.fold_in(key, 0), shape, _jnp.float32)
    for i, name in enumerate(TWIN_WEIGHTS):
        w = inp[name].astype(_jnp.float32)
        if MOMENT_SCALE is None:
            s = _jnp.sqrt(_jnp.mean(_jnp.square(w)) + 1e-30)
        else:
            s = MOMENT_SCALE[name]
        km, kv = _jax.random.split(_jax.random.fold_in(key, i + 1))
        out[name] = w
        out["m_" + name] = s * _jax.random.normal(km, w.shape, _jnp.float32)
        out["v_" + name] = (s * s) * _jax.random.uniform(kv, w.shape, _jnp.float32, 0.5, 1.5)
    if N_MICROBATCH > 1:
        for name, axis in PER_EXAMPLE_BATCH_AXIS.items():
            out[name] = _to_microbatches(out[name], axis)
    return {'x': out['x'], 'meta_tokens': out['meta_tokens'], 'ln_in_g': out['ln_in_g'], 'ln_in_b': out['ln_in_b'], 'w_in': out['w_in'], 'b_gate': out['b_gate'], 'attn_sinks': out['attn_sinks'], 'w_attn_up': out['w_attn_up'], 'w_pool_grp': out['w_pool_grp'], 'pool_scale': out['pool_scale'], 'w_pool_up': out['w_pool_up'], 'w_out': out['w_out'], 'ln1_g': out['ln1_g'], 'ln1_b': out['ln1_b'], 'w_ffn_in': out['w_ffn_in'], 'w_ffn_down': out['w_ffn_down'], 'ln2_g': out['ln2_g'], 'ln2_b': out['ln2_b'], 'loss_target': out['loss_target'], 'm_meta_tokens': out['m_meta_tokens'], 'm_ln_in_g': out['m_ln_in_g'], 'm_ln_in_b': out['m_ln_in_b'], 'm_w_in': out['m_w_in'], 'm_b_gate': out['m_b_gate'], 'm_attn_sinks': out['m_attn_sinks'], 'm_w_attn_up': out['m_w_attn_up'], 'm_w_pool_grp': out['m_w_pool_grp'], 'm_pool_scale': out['m_pool_scale'], 'm_w_pool_up': out['m_w_pool_up'], 'm_w_out': out['m_w_out'], 'm_ln1_g': out['m_ln1_g'], 'm_ln1_b': out['m_ln1_b'], 'm_w_ffn_in': out['m_w_ffn_in'], 'm_w_ffn_down': out['m_w_ffn_down'], 'm_ln2_g': out['m_ln2_g'], 'm_ln2_b': out['m_ln2_b'], 'v_meta_tokens': out['v_meta_tokens'], 'v_ln_in_g': out['v_ln_in_g'], 'v_ln_in_b': out['v_ln_in_b'], 'v_w_in': out['v_w_in'], 'v_b_gate': out['v_b_gate'], 'v_attn_sinks': out['v_attn_sinks'], 'v_w_attn_up': out['v_w_attn_up'], 'v_w_pool_grp': out['v_w_pool_grp'], 'v_pool_scale': out['v_pool_scale'], 'v_w_pool_up': out['v_w_pool_up'], 'v_w_out': out['v_w_out'], 'v_ln1_g': out['v_ln1_g'], 'v_ln1_b': out['v_ln1_b'], 'v_w_ffn_in': out['v_w_ffn_in'], 'v_w_ffn_down': out['v_w_ffn_down'], 'v_ln2_g': out['v_ln2_g'], 'v_ln2_b': out['v_ln2_b']}


def _loss(weights, diff, rest, loss_target):
    with _jax.named_scope("forward"):
        args = {**rest, TWIN_DIFF_INPUT: diff, **{k: w.astype(_WEIGHT_DTYPES[k]) for k, w in weights.items()}}
        y = _forward(args)
    with _jax.named_scope("loss_head"):
        err = _jnp.square(y.astype(_jnp.float32) - loss_target)
        return 0.5 * _jnp.sum(_jnp.mean(err, axis=-1)) if err.ndim else 0.5 * err


def _adamw(w, g, m, v):
    m = ADAM_B1 * m + (1.0 - ADAM_B1) * g
    v = ADAM_B2 * v + (1.0 - ADAM_B2) * _jnp.square(g)
    m_hat = m / (1.0 - ADAM_B1 ** ADAM_STEP)
    v_hat = v / (1.0 - ADAM_B2 ** ADAM_STEP)
    delta = -ADAM_LR * (m_hat / (_jnp.sqrt(v_hat) + ADAM_EPS) + ADAM_WD * w)
    return delta, m, v


def reference(x, meta_tokens, ln_in_g, ln_in_b, w_in, b_gate, attn_sinks, w_attn_up, w_pool_grp, pool_scale, w_pool_up, w_out, ln1_g, ln1_b, w_ffn_in, w_ffn_down, ln2_g, ln2_b, loss_target, m_meta_tokens, m_ln_in_g, m_ln_in_b, m_w_in, m_b_gate, m_attn_sinks, m_w_attn_up, m_w_pool_grp, m_pool_scale, m_w_pool_up, m_w_out, m_ln1_g, m_ln1_b, m_w_ffn_in, m_w_ffn_down, m_ln2_g, m_ln2_b, v_meta_tokens, v_ln_in_g, v_ln_in_b, v_w_in, v_b_gate, v_attn_sinks, v_w_attn_up, v_w_pool_grp, v_pool_scale, v_w_pool_up, v_w_out, v_ln1_g, v_ln1_b, v_w_ffn_in, v_w_ffn_down, v_ln2_g, v_ln2_b):
    given = dict(x=x, meta_tokens=meta_tokens, ln_in_g=ln_in_g, ln_in_b=ln_in_b, w_in=w_in, b_gate=b_gate, attn_sinks=attn_sinks, w_attn_up=w_attn_up, w_pool_grp=w_pool_grp, pool_scale=pool_scale, w_pool_up=w_pool_up, w_out=w_out, ln1_g=ln1_g, ln1_b=ln1_b, w_ffn_in=w_ffn_in, w_ffn_down=w_ffn_down, ln2_g=ln2_g, ln2_b=ln2_b, loss_target=loss_target, m_meta_tokens=m_meta_tokens, m_ln_in_g=m_ln_in_g, m_ln_in_b=m_ln_in_b, m_w_in=m_w_in, m_b_gate=m_b_gate, m_attn_sinks=m_attn_sinks, m_w_attn_up=m_w_attn_up, m_w_pool_grp=m_w_pool_grp, m_pool_scale=m_pool_scale, m_w_pool_up=m_w_pool_up, m_w_out=m_w_out, m_ln1_g=m_ln1_g, m_ln1_b=m_ln1_b, m_w_ffn_in=m_w_ffn_in, m_w_ffn_down=m_w_ffn_down, m_ln2_g=m_ln2_g, m_ln2_b=m_ln2_b, v_meta_tokens=v_meta_tokens, v_ln_in_g=v_ln_in_g, v_ln_in_b=v_ln_in_b, v_w_in=v_w_in, v_b_gate=v_b_gate, v_attn_sinks=v_attn_sinks, v_w_attn_up=v_w_attn_up, v_w_pool_grp=v_w_pool_grp, v_pool_scale=v_pool_scale, v_w_pool_up=v_w_pool_up, v_w_out=v_w_out, v_ln1_g=v_ln1_g, v_ln1_b=v_ln1_b, v_w_ffn_in=v_w_ffn_in, v_w_ffn_down=v_w_ffn_down, v_ln2_g=v_ln2_g, v_ln2_b=v_ln2_b)
    weights = {n: given[n] for n in TWIN_WEIGHTS}
    shared = {n: given[n] for n in SHARED_INPUTS}
    per_example = {n: given[n] for n in ['x']}
    grad_fn = _jax.value_and_grad(_loss, argnums=(0, 1))

    def one_microbatch(ex, loss_target):
        ex = dict(ex)
        diff = ex.pop(TWIN_DIFF_INPUT)
        return grad_fn(weights, diff, {**shared, **ex}, loss_target)

    if N_MICROBATCH == 1:
        loss, (grad_w, grad_x) = one_microbatch(per_example, given["loss_target"])
    else:
        def body(carry, xs):
            loss_sum, grad_sum = carry
            l_k, (gw_k, gx_k) = one_microbatch(xs[0], xs[1])
            with _jax.named_scope("update"):
                return (loss_sum + l_k, _jax.tree.map(_jnp.add, grad_sum, gw_k)), gx_k

        init = (_jnp.zeros((), _jnp.float32), _jax.tree.map(_jnp.zeros_like, weights))
        (loss, grad_w), grad_x = _jax.lax.scan(body, init, (per_example, given["loss_target"]))
    with _jax.named_scope("update"):
        delta_w, new_m, new_v = {}, {}, {}
        for n in TWIN_WEIGHTS:
            delta_w[n], new_m[n], new_v[n] = _adamw(weights[n], grad_w[n], given["m_" + n], given["v_" + n])
    return (loss, grad_x, *[grad_w[n] for n in TWIN_WEIGHTS], *[delta_w[n] for n in TWIN_WEIGHTS],
            *[new_m[n] for n in TWIN_WEIGHTS], *[new_v[n] for n in TWIN_WEIGHTS])
```

```python
import functools
import math

import jax
import jax.numpy as jnp
from jax import lax
from jax.experimental import pallas as pl
from jax.experimental.pallas import tpu as pltpu

F32 = jnp.float32
BF16 = jnp.bfloat16
MESH = pl.DeviceIdType.MESH
ANY = pl.BlockSpec(memory_space=pl.ANY)
VMEM_FULL = pl.BlockSpec(memory_space=pltpu.VMEM)

N_META = 16
HEAD_DIM = 64
N_KV = 4
BLK = 128
LEAD = (-N_META) % BLK
ROPE_DIM = HEAD_DIM // 4
ROPE_THETA = 500000.0
NEG_INF = -1e30
POOL_WINDOWS = (2, 4, 8, 16)
N_GRP = len(POOL_WINDOWS)
LN_EPS = 1e-5
DN_ALPHA = 2.0 ** 0.25
ADAM_LR = 0.001
ADAM_B1 = 0.9
ADAM_B2 = 0.999
ADAM_EPS = 1e-08
ADAM_WD = 0.01
ADAM_STEP = 10
N_CHIPS = 4
LANES = 128
VMEM_LIMIT_MB = 56
HI = lax.Precision.HIGHEST


def _cparams(sem=None, vmem_mb=VMEM_LIMIT_MB):
    kw = dict(vmem_limit_bytes=vmem_mb << 20)
    if sem is not None:
        kw["dimension_semantics"] = sem
    return pltpu.CompilerParams(**kw)


def _pick(dim, *cands):
    for c in cands:
        if c <= dim and dim % c == 0:
            return c
    return dim


def _sds(shape, dtype):
    return jax.ShapeDtypeStruct(shape, dtype)


_DOT_DIMS = {
    "nn": (((1,), (0,)), ((), ())),
    "nt": (((1,), (1,)), ((), ())),
    "tn": (((0,), (0,)), ((), ())),
}


def _mm(a, b, mode, out_dtype, tm, tn, tk, name, j_outer=False):
    if mode == "nn":
        (m, k), n = a.shape, b.shape[1]
    elif mode == "nt":
        (m, k), n = a.shape, b.shape[0]
    else:
        (k, m), n = a.shape, b.shape[1]
    tm, tn, tk = _pick(m, tm), _pick(n, tn), _pick(k, tk)
    gi, gj, gk = m // tm, n // tn, k // tk
    dims = _DOT_DIMS[mode]

    def ij(g0, g1):
        return (g1, g0) if j_outer else (g0, g1)

    if mode == "tn":
        a_spec = pl.BlockSpec((tk, tm), lambda g0, g1, kk: (kk, ij(g0, g1)[0]))
    else:
        a_spec = pl.BlockSpec((tm, tk), lambda g0, g1, kk: (ij(g0, g1)[0], kk))
    if mode == "nt":
        b_spec = pl.BlockSpec((tn, tk), lambda g0, g1, kk: (ij(g0, g1)[1], kk))
    else:
        b_spec = pl.BlockSpec((tk, tn), lambda g0, g1, kk: (kk, ij(g0, g1)[1]))
    o_spec = pl.BlockSpec((tm, tn), lambda g0, g1, kk: ij(g0, g1))

    def body(a_ref, b_ref, o_ref, *scr):
        p = lax.dot_general(a_ref[...], b_ref[...], dims, preferred_element_type=F32)
        if gk == 1:
            o_ref[...] = p.astype(out_dtype)
        else:
            acc = scr[0]
            kk = pl.program_id(2)

            @pl.when(kk == 0)
            def _():
                acc[...] = p

            @pl.when(kk > 0)
            def _():
                acc[...] += p

            @pl.when(kk == gk - 1)
            def _():
                o_ref[...] = acc[...].astype(out_dtype)

    return pl.pallas_call(
        body,
        name=name,
        grid=(gj, gi, gk) if j_outer else (gi, gj, gk),
        in_specs=[a_spec, b_spec],
        out_specs=o_spec,
        out_shape=_sds((m, n), out_dtype),
        scratch_shapes=[pltpu.VMEM((tm, tn), F32)] if gk > 1 else [],
        compiler_params=_cparams(("parallel", "parallel", "arbitrary")),
    )(a, b)


def _stream_block(i, x_ref, meta_ref):
    d = x_ref.shape[-1]
    first = jnp.concatenate([jnp.zeros((LEAD, d), F32), meta_ref[...]], axis=0)
    return jnp.where(i == 0, first, x_ref[...])


def _norm(xb):
    mu = jnp.mean(xb, axis=-1, keepdims=True)
    xc = xb - mu
    var = jnp.mean(xc * xc, axis=-1, keepdims=True)
    rstd = lax.rsqrt(var + LN_EPS)
    return xc * rstd, rstd


def _ln_bwd_rows(dy, xhat, rstd, g):
    dyg = dy * g
    m1 = jnp.mean(dyg, axis=-1, keepdims=True)
    m2 = jnp.mean(dyg * xhat, axis=-1, keepdims=True)
    return rstd * (dyg - m1 - xhat * m2)


def _ln_in_fwd(x2d, meta, g, b, nb):
    seq, d = x2d.shape

    def body(x_ref, meta_ref, g_ref, b_ref, h_ref, hb_ref):
        xb = _stream_block(pl.program_id(0), x_ref, meta_ref)
        xhat, _ = _norm(xb)
        y = xhat * g_ref[...] + b_ref[...]
        h_ref[...] = y
        hb_ref[...] = y.astype(BF16)

    row = pl.BlockSpec((BLK, d), lambda i: (i, 0))
    vec = pl.BlockSpec((1, d), lambda i: (0, 0))
    return pl.pallas_call(
        body,
        name="ln_in_fwd",
        grid=(nb,),
        in_specs=[pl.BlockSpec((BLK, d), lambda i: (jnp.maximum(i - 1, 0), 0)), pl.BlockSpec((N_META, d), lambda i: (0, 0)), vec, vec],
        out_specs=[row, row],
        out_shape=[_sds((nb * BLK, d), F32), _sds((nb * BLK, d), BF16)],
        compiler_params=_cparams(("parallel",)),
    )(x2d, meta, g, b)


def _res_ln_fwd(h, z, g, b):
    tp, d = h.shape

    def body(h_ref, z_ref, g_ref, b_ref, r_ref, y_ref, yb_ref):
        r = DN_ALPHA * h_ref[...] + z_ref[...]
        xhat, _ = _norm(r)
        y = xhat * g_ref[...] + b_ref[...]
        r_ref[...] = r
        y_ref[...] = y
        yb_ref[...] = y.astype(BF16)

    row = pl.BlockSpec((BLK, d), lambda i: (i, 0))
    vec = pl.BlockSpec((1, d), lambda i: (0, 0))
    return pl.pallas_call(
        body,
        name="res_ln1_fwd",
        grid=(tp // BLK,),
        in_specs=[row, row, vec, vec],
        out_specs=[row, row, row],
        out_shape=[_sds((tp, d), F32), _sds((tp, d), F32), _sds((tp, d), BF16)],
        compiler_params=_cparams(("parallel",)),
    )(h, z, g, b)


def _final_ln_loss(h1, z2, g, b, tgt):
    tp, d = h1.shape

    def body(h_ref, z_ref, g_ref, b_ref, t_ref, dr_ref, drb_ref, loss_ref, dg_ref, db_ref):
        i = pl.program_id(0)
        r = DN_ALPHA * h_ref[...] + z_ref[...]
        xhat, rstd = _norm(r)
        y = xhat * g_ref[...] + b_ref[...]
        err = jnp.where(i >= 1, y - t_ref[...], 0.0)
        dy = err * (1.0 / d)
        dr = _ln_bwd_rows(dy, xhat, rstd, g_ref[...])
        dr_ref[...] = dr
        drb_ref[...] = dr.astype(BF16)

        @pl.when(i == 0)
        def _():
            loss_ref[...] = jnp.zeros_like(loss_ref)
            dg_ref[...] = jnp.zeros_like(dg_ref)
            db_ref[...] = jnp.zeros_like(db_ref)

        loss_ref[...] += 0.5 * jnp.sum(jnp.sum(err * err, axis=-1, keepdims=True) * (1.0 / d), axis=0, keepdims=True)
        dg_ref[...] += jnp.sum(dy * xhat, axis=0, keepdims=True)
        db_ref[...] += jnp.sum(dy, axis=0, keepdims=True)

    row = pl.BlockSpec((BLK, d), lambda i: (i, 0))
    vec = pl.BlockSpec((1, d), lambda i: (0, 0))
    return pl.pallas_call(
        body,
        name="final_ln_loss",
        grid=(tp // BLK,),
        in_specs=[row, row, vec, vec, pl.BlockSpec((BLK, d), lambda i: (jnp.maximum(i - 1, 0), 0))],
        out_specs=[row, row, pl.BlockSpec((8, LANES), lambda i: (0, 0)), vec, vec],
        out_shape=[_sds((tp, d), F32), _sds((tp, d), BF16), _sds((8, LANES), F32), _sds((1, d), F32), _sds((1, d), F32)],
        compiler_params=_cparams(("arbitrary",)),
    )(h1, z2, g, b, tgt)


def _ln1_bwd(d_res, d_mm, r, g):
    tp, d = r.shape

    def body(a_ref, m_ref, r_ref, g_ref, dr_ref, drb_ref, dg_ref, db_ref):
        dy = DN_ALPHA * a_ref[...] + m_ref[...]
        xhat, rstd = _norm(r_ref[...])
        dr = _ln_bwd_rows(dy, xhat, rstd, g_ref[...])
        dr_ref[...] = dr
        drb_ref[...] = dr.astype(BF16)

        @pl.when(pl.program_id(0) == 0)
        def _():
            dg_ref[...] = jnp.zeros_like(dg_ref)
            db_ref[...] = jnp.zeros_like(db_ref)

        dg_ref[...] += jnp.sum(dy * xhat, axis=0, keepdims=True)
        db_ref[...] += jnp.sum(dy, axis=0, keepdims=True)

    row = pl.BlockSpec((BLK, d), lambda i: (i, 0))
    vec = pl.BlockSpec((1, d), lambda i: (0, 0))
    return pl.pallas_call(
        body,
        name="ln1_bwd",
        grid=(tp // BLK,),
        in_specs=[row, row, row, vec],
        out_specs=[row, row, vec, vec],
        out_shape=[_sds((tp, d), F32), _sds((tp, d), BF16), _sds((1, d), F32), _sds((1, d), F32)],
        compiler_params=_cparams(("arbitrary",)),
    )(d_res, d_mm, r, g)


def _ln_in_bwd(d_res, d_mm, x2d, meta, g):
    seq, d = x2d.shape
    nb = d_res.shape[0] // BLK

    def body(a_ref, m_ref, x_ref, meta_ref, g_ref, gx_ref, gm_ref, dg_ref, db_ref):
        i = pl.program_id(0)
        dy = DN_ALPHA * a_ref[...] + m_ref[...]
        xhat, rstd = _norm(_stream_block(i, x_ref, meta_ref))
        dx = _ln_bwd_rows(dy, xhat, rstd, g_ref[...])
        gx_ref[...] = dx

        @pl.when(i == 0)
        def _():
            gm_ref[...] = dx[LEAD:, :]
            dg_ref[...] = jnp.zeros_like(dg_ref)
            db_ref[...] = jnp.zeros_like(db_ref)

        dg_ref[...] += jnp.sum(dy * xhat, axis=0, keepdims=True)
        db_ref[...] += jnp.sum(dy, axis=0, keepdims=True)

    row = pl.BlockSpec((BLK, d), lambda i: (i, 0))
    xrow = pl.BlockSpec((BLK, d), lambda i: (jnp.maximum(i - 1, 0), 0))
    vec = pl.BlockSpec((1, d), lambda i: (0, 0))
    met = pl.BlockSpec((N_META, d), lambda i: (0, 0))
    return pl.pallas_call(
        body,
        name="ln_in_bwd",
        grid=(nb,),
        in_specs=[row, row, xrow, met, vec],
        out_specs=[xrow, met, vec, vec],
        out_shape=[_sds((seq, d), F32), _sds((N_META, d), F32), _sds((1, d), F32), _sds((1, d), F32)],
        compiler_params=_cparams(("arbitrary",)),
    )(d_res, d_mm, x2d, meta, g)


def _rope_tables(tp):
    half = ROPE_DIM // 2
    inv_freq = ROPE_THETA ** (-jnp.arange(half, dtype=F32) * 2.0 / ROPE_DIM)
    pos = (jnp.arange(tp) - LEAD).astype(F32)
    ang = pos[:, None] * inv_freq[None, :]
    cos, sin = jnp.cos(ang), jnp.sin(ang)
    ones = jnp.ones((tp, HEAD_DIM - ROPE_DIM), F32)
    cos_h = jnp.concatenate([cos, cos, ones], axis=1)
    sin_h = jnp.concatenate([-sin, sin, 0.0 * ones], axis=1)
    reps = LANES // HEAD_DIM
    return jnp.tile(cos_h, (1, reps)), jnp.tile(sin_h, (1, reps))


def _rope_partner(x):
    half = ROPE_DIM // 2
    lane = lax.broadcasted_iota(jnp.int32, x.shape, 1) % HEAD_DIM
    upper = jnp.where(lane < ROPE_DIM, pltpu.roll(x, half, 1), 0.0)
    return jnp.where(lane < half, pltpu.roll(x, LANES - half, 1), upper)


def _rope_fwd(proj, cos, sin, n_rot, width):
    tp = proj.shape[0]

    def body(p_ref, c_ref, s_ref, o_ref):
        c, s = c_ref[...], s_ref[...]
        for j in range(width // LANES):
            sl = slice(j * LANES, (j + 1) * LANES)
            xj = p_ref[:, sl]
            if j < n_rot:
                xj = xj * c + _rope_partner(xj) * s
            o_ref[:, sl] = xj.astype(BF16)

    tab = pl.BlockSpec((BLK, LANES), lambda i: (i, 0))
    return pl.pallas_call(
        body,
        name="rope_fwd",
        grid=(tp // BLK,),
        in_specs=[pl.BlockSpec((BLK, width), lambda i: (i, 0)), tab, tab],
        out_specs=pl.BlockSpec((BLK, width), lambda i: (i, 0)),
        out_shape=_sds((tp, width), BF16),
        compiler_params=_cparams(("parallel",)),
    )(proj, cos, sin)


def _rope_bwd(dq, dk_cur, dk_prev, dk_meta, dv_cur, dv_prev, dv_meta, cos, sin):
    tp, aw = dq.shape
    kw = dk_cur.shape[1]
    nb = tp // BLK

    def body(dq_ref, kc_ref, kp_ref, km_ref, vc_ref, vp_ref, vm_ref, c_ref, s_ref, o_ref):
        i = pl.program_id(0)
        c, s = c_ref[...], s_ref[...]
        has_next = i + 1 < nb

        def unrot(g):
            return g * c + _rope_partner(g * s)

        def kv_sum(cur, prv, met):
            return cur[...] + jnp.where(has_next, prv[...], 0.0) + jnp.where(i == 0, met[...], 0.0)

        for j in range(aw // LANES):
            sl = slice(j * LANES, (j + 1) * LANES)
            o_ref[:, sl] = unrot(dq_ref[:, sl]).astype(BF16)
        dk = kv_sum(kc_ref, kp_ref, km_ref)
        dv = kv_sum(vc_ref, vp_ref, vm_ref)
        for j in range(kw // LANES):
            sl = slice(j * LANES, (j + 1) * LANES)
            o_ref[:, aw + j * LANES:aw + (j + 1) * LANES] = unrot(dk[:, sl]).astype(BF16)
            o_ref[:, aw + kw + j * LANES:aw + kw + (j + 1) * LANES] = dv[:, sl].astype(BF16)

    cur = pl.BlockSpec((BLK, kw), lambda i: (i, 0))
    nxt = pl.BlockSpec((BLK, kw), lambda i: (jnp.minimum(i + 1, nb - 1), 0))
    met = pl.BlockSpec((BLK, kw), lambda i: (0, 0))
    tab = pl.BlockSpec((BLK, LANES), lambda i: (i, 0))
    return pl.pallas_call(
        body,
        name="rope_bwd",
        grid=(nb,),
        in_specs=[pl.BlockSpec((BLK, aw), lambda i: (i, 0)), cur, nxt, met, cur, nxt, met, tab, tab],
        out_specs=pl.BlockSpec((BLK, aw + 2 * kw), lambda i: (i, 0)),
        out_shape=_sds((tp, aw + 2 * kw), BF16),
        compiler_params=_cparams(("parallel",)),
    )(dq, dk_cur, dk_prev, dk_meta, dv_cur, dv_prev, dv_meta, cos, sin)


def _attn_probs(n, q_ref, km_ref, kp_ref, kc_ref, sink_ref, grp):
    scale = HEAD_DIM ** -0.5
    qs = q_ref[...].reshape(grp * BLK, HEAD_DIM)
    kcat = jnp.concatenate([km_ref[...], kp_ref[...], kc_ref[...]], axis=0)
    s = lax.dot_general(qs, kcat, _DOT_DIMS["nt"], preferred_element_type=F32) * scale
    s = s.reshape(grp, BLK, 3 * BLK)
    r = lax.broadcasted_iota(jnp.int32, (1, BLK, 3 * BLK), 1)
    j = lax.broadcasted_iota(jnp.int32, (1, BLK, 3 * BLK), 2)
    q_idx = n * BLK + r
    meta_ok = (j >= LEAD) & (j < BLK) & (q_idx >= j)
    k_idx = (n - 1) * BLK + (j - BLK)
    diff = q_idx - k_idx
    band_ok = (j >= BLK) & (diff >= 0) & (diff < BLK) & (k_idx >= LEAD + N_META)
    s = jnp.where(meta_ok | band_ok, s, NEG_INF)
    sink = sink_ref[...]
    m = jnp.maximum(jnp.max(s, axis=-1, keepdims=True), sink)
    p = jnp.exp(s - m)
    e_sink = jnp.exp(sink - m)
    inv = 1.0 / (jnp.sum(p, axis=-1, keepdims=True) + e_sink)
    return qs, kcat, p * inv, e_sink * inv


def _attn_specs(grp, nkv):
    qspec = pl.BlockSpec((grp, BLK, HEAD_DIM), lambda kk, n: (kk, n, 0))
    kmeta = pl.BlockSpec((None, BLK, HEAD_DIM), lambda kk, n: (kk, 0, 0))
    kprev = pl.BlockSpec((None, BLK, HEAD_DIM), lambda kk, n: (kk, jnp.maximum(n - 1, 0), 0))
    kcur = pl.BlockSpec((None, BLK, HEAD_DIM), lambda kk, n: (kk, n, 0))
    sink = pl.BlockSpec((None, grp, BLK, 1), lambda kk, n: (kk, 0, 0, 0))
    return qspec, kmeta, kprev, kcur, sink


def _attn_fwd(q_hm, k_hm, v_hm, sink4):
    nq, tp, _ = q_hm.shape
    nkv = k_hm.shape[0]
    grp = nq // nkv

    def body(q_ref, km_ref, kp_ref, kc_ref, vm_ref, vp_ref, vc_ref, sink_ref, o_ref):
        n = pl.program_id(1)
        _, _, pn, _ = _attn_probs(n, q_ref, km_ref, kp_ref, kc_ref, sink_ref, grp)
        vcat = jnp.concatenate([vm_ref[...], vp_ref[...], vc_ref[...]], axis=0)
        o = jnp.dot(pn.reshape(grp * BLK, 3 * BLK).astype(BF16), vcat, preferred_element_type=F32)
        o_ref[...] = o.reshape(grp, BLK, HEAD_DIM).astype(BF16)

    qspec, kmeta, kprev, kcur, sink = _attn_specs(grp, nkv)
    return pl.pallas_call(
        body,
        name="attn_fwd",
        grid=(nkv, tp // BLK),
        in_specs=[qspec, kmeta, kprev, kcur, kmeta, kprev, kcur, sink],
        out_specs=qspec,
        out_shape=_sds((nq, tp, HEAD_DIM), BF16),
        compiler_params=_cparams(("parallel", "parallel")),
    )(q_hm, k_hm, k_hm, k_hm, v_hm, v_hm, v_hm, sink4)


def _attn_bwd(q_hm, k_hm, v_hm, sink4, do_hm):
    nq, tp, _ = q_hm.shape
    nkv = k_hm.shape[0]
    grp = nq // nkv
    scale = HEAD_DIM ** -0.5

    def body(q_ref, km_ref, kp_ref, kc_ref, vm_ref, vp_ref, vc_ref, sink_ref, do_ref,
             dq_ref, dkc_ref, dkp_ref, dkm_ref, dvc_ref, dvp_ref, dvm_ref, dsk_ref):
        n = pl.program_id(1)
        qs, kcat, pn, p_sink = _attn_probs(n, q_ref, km_ref, kp_ref, kc_ref, sink_ref, grp)
        vcat = jnp.concatenate([vm_ref[...], vp_ref[...], vc_ref[...]], axis=0)
        pn2 = pn.reshape(grp * BLK, 3 * BLK)
        pnb = pn2.astype(BF16)
        dob = do_ref[...].reshape(grp * BLK, HEAD_DIM).astype(BF16)
        dp = lax.dot_general(dob, vcat, _DOT_DIMS["nt"], preferred_element_type=F32)
        delta = jnp.sum(pn2 * dp, axis=-1, keepdims=True)
        ds = (pn2 * (dp - delta) * scale).astype(BF16)
        dq_ref[...] = jnp.dot(ds, kcat, preferred_element_type=F32).reshape(grp, BLK, HEAD_DIM)
        dk = lax.dot_general(ds, qs, _DOT_DIMS["tn"], preferred_element_type=F32)
        dv = lax.dot_general(pnb, dob, _DOT_DIMS["tn"], preferred_element_type=F32)
        dkp_ref[...] = dk[BLK:2 * BLK]
        dkc_ref[...] = dk[2 * BLK:]
        dvp_ref[...] = dv[BLK:2 * BLK]
        dvc_ref[...] = dv[2 * BLK:]
        dsk = -jnp.sum(p_sink * delta.reshape(grp, BLK, 1), axis=1, keepdims=True)

        @pl.when(n == 0)
        def _():
            dkm_ref[...] = jnp.zeros_like(dkm_ref)
            dvm_ref[...] = jnp.zeros_like(dvm_ref)
            dsk_ref[...] = jnp.zeros_like(dsk_ref)

        dkm_ref[...] += dk[:BLK]
        dvm_ref[...] += dv[:BLK]
        dsk_ref[...] += jnp.broadcast_to(dsk, (grp, BLK, 1))

    qspec, kmeta, kprev, kcur, sink = _attn_specs(grp, nkv)
    kv_shape = _sds((nkv, tp, HEAD_DIM), F32)
    meta_shape = _sds((nkv, BLK, HEAD_DIM), F32)
    return pl.pallas_call(
        body,
        name="attn_bwd",
        grid=(nkv, tp // BLK),
        in_specs=[qspec, kmeta, kprev, kcur, kmeta, kprev, kcur, sink, qspec],
        out_specs=[qspec, kcur, kcur, kmeta, kcur, kcur, kmeta, sink],
        out_shape=[_sds((nq, tp, HEAD_DIM), F32), kv_shape, kv_shape, meta_shape, kv_shape, kv_shape, meta_shape,
                   _sds((nkv, grp, BLK, 1), F32)],
        compiler_params=_cparams(("parallel", "arbitrary")),
    )(q_hm, k_hm, k_hm, k_hm, v_hm, v_hm, v_hm, sink4, do_hm)


def _pool_coef(row_blk, col_blk, w):
    r = lax.broadcasted_iota(jnp.int32, (BLK, BLK), 0)
    j = lax.broadcasted_iota(jnp.int32, (BLK, BLK), 1)
    t = row_blk * BLK + r - LEAD
    tj = col_blk * BLK + j - LEAD
    dist = t - tj
    inwin = (dist >= 0) & (dist < w) & (tj >= 0)
    count = jnp.maximum(jnp.minimum(t + 1, w), 1).astype(F32)
    return jnp.where(inwin, 1.0 / count, 0.0) - jnp.where((dist == 0) & (tj >= 0), 1.0, 0.0)


def _pool_fwd(proj, wg, scale, u_off, pool_w):
    tp = proj.shape[0]
    gw = pool_w // N_GRP
    nb = tp // BLK
    cb = u_off // gw

    def body(up_ref, uc_ref, wg_ref, sc_ref, pooled_ref, mx_ref, pm_ref):
        n, g = pl.program_id(0), pl.program_id(1)
        w = jnp.left_shift(2, g)
        pooled = (jnp.dot(_pool_coef(n, n - 1, w), up_ref[...], precision=HI, preferred_element_type=F32)
                  + jnp.dot(_pool_coef(n, n, w), uc_ref[...], precision=HI, preferred_element_type=F32))
        pb = pooled.astype(BF16)
        mx = jnp.dot(pb, wg_ref[...], preferred_element_type=F32)
        pooled_ref[...] = pb
        mx_ref[...] = mx
        pm_ref[...] = (mx * sc_ref[...]).astype(BF16)

    blk = pl.BlockSpec((BLK, gw), lambda n, g: (n, g))
    return pl.pallas_call(
        body,
        name="pool_fwd",
        grid=(nb, N_GRP),
        in_specs=[pl.BlockSpec((BLK, gw), lambda n, g: (jnp.maximum(n - 1, 0), cb + g)),
                  pl.BlockSpec((BLK, gw), lambda n, g: (n, cb + g)),
                  pl.BlockSpec((None, gw, gw), lambda n, g: (g, 0, 0)),
                  pl.BlockSpec((1, gw), lambda n, g: (0, g))],
        out_specs=[blk, blk, blk],
        out_shape=[_sds((tp, pool_w), BF16), _sds((tp, pool_w), F32), _sds((tp, pool_w), BF16)],
        compiler_params=_cparams(("parallel", "parallel")),
    )(proj, proj, wg, scale)


def _pool_bwd_mix(d_pm, mx, pooled, wg, scale):
    tp, pool_w = d_pm.shape
    gw = pool_w // N_GRP

    def body(d_ref, mx_ref, pl_ref, wg_ref, sc_ref, dp_ref, dwg_ref, dsc_ref):
        n = pl.program_id(1)
        d = d_ref[...]
        dmx = (d * sc_ref[...]).astype(BF16)
        dp_ref[...] = lax.dot_general(dmx, wg_ref[...], _DOT_DIMS["nt"], preferred_element_type=F32)

        @pl.when(n == 0)
        def _():
            dwg_ref[...] = jnp.zeros_like(dwg_ref)
            dsc_ref[...] = jnp.zeros_like(dsc_ref)

        dwg_ref[...] += lax.dot_general(pl_ref[...], dmx, _DOT_DIMS["tn"], preferred_element_type=F32)
        dsc_ref[...] += jnp.sum(d * mx_ref[...], axis=0, keepdims=True)

    blk = pl.BlockSpec((BLK, gw), lambda g, n: (n, g))
    wspec = pl.BlockSpec((None, gw, gw), lambda g, n: (g, 0, 0))
    sspec = pl.BlockSpec((1, gw), lambda g, n: (0, g))
    return pl.pallas_call(
        body,
        name="pool_bwd_mix",
        grid=(N_GRP, tp // BLK),
        in_specs=[blk, blk, blk, wspec, sspec],
        out_specs=[blk, wspec, sspec],
        out_shape=[_sds((tp, pool_w), F32), _sds((N_GRP, gw, gw), F32), _sds((1, pool_w), F32)],
        compiler_params=_cparams(("parallel", "arbitrary")),
    )(d_pm, mx, pooled, wg, scale)


def _pool_bwd_band(dp):
    tp, pool_w = dp.shape
    gw = pool_w // N_GRP
    nb = tp // BLK

    def body(dc_ref, dn_ref, du_ref):
        n, g = pl.program_id(0), pl.program_id(1)
        w = jnp.left_shift(2, g)
        dnext = jnp.where(n + 1 < nb, dn_ref[...], 0.0)
        du = (lax.dot_general(_pool_coef(n, n, w), dc_ref[...], _DOT_DIMS["tn"], precision=HI, preferred_element_type=F32)
              + lax.dot_general(_pool_coef(n + 1, n, w), dnext, _DOT_DIMS["tn"], precision=HI, preferred_element_type=F32))
        du_ref[...] = du.astype(BF16)

    blk = pl.BlockSpec((BLK, gw), lambda n, g: (n, g))
    return pl.pallas_call(
        body,
        name="pool_bwd_band",
        grid=(nb, N_GRP),
        in_specs=[blk, pl.BlockSpec((BLK, gw), lambda n, g: (jnp.minimum(n + 1, nb - 1), g))],
        out_specs=blk,
        out_shape=_sds((tp, pool_w), BF16),
        compiler_params=_cparams(("parallel", "parallel")),
    )(dp, dp)


def _gate_tiles(tp, d, g_off):
    tc = _pick(math.gcd(g_off, d), 512, 256, 128)
    tr = _pick(tp, 384, 128)
    return tr, tc


def _mix_fwd(proj, b_gate, a_out, p_out, g_off):
    tp, d = a_out.shape
    tr, tc = _gate_tiles(tp, d, g_off)
    c0, c1 = g_off // tc, (g_off + d) // tc

    def body(g0_ref, g1_ref, b_ref, a_ref, p_ref, o_ref):
        g0 = jax.nn.sigmoid(g0_ref[...] + b_ref[0:1, :])
        g1 = jax.nn.sigmoid(g1_ref[...] + b_ref[1:2, :])
        o_ref[...] = (g0 * a_ref[...] + g1 * p_ref[...]).astype(BF16)

    blk = pl.BlockSpec((tr, tc), lambda i, j: (i, j))
    return pl.pallas_call(
        body,
        name="mix_fwd",
        grid=(tp // tr, d // tc),
        in_specs=[pl.BlockSpec((tr, tc), lambda i, j: (i, c0 + j)), pl.BlockSpec((tr, tc), lambda i, j: (i, c1 + j)),
                  pl.BlockSpec((2, tc), lambda i, j: (0, j)), blk, blk],
        out_specs=blk,
        out_shape=_sds((tp, d), BF16),
        compiler_params=_cparams(("parallel", "parallel")),
    )(proj, proj, b_gate, a_out, p_out)


def _mix_bwd(proj, b_gate, a_out, p_out, d_mixed, g_off):
    tp, d = a_out.shape
    tr, tc = _gate_tiles(tp, d, g_off)
    c0, c1 = g_off // tc, (g_off + d) // tc

    def body(g0_ref, g1_ref, b_ref, a_ref, p_ref, d_ref, da_ref, dp_ref, dl0_ref, dl1_ref, db_ref):
        g0 = jax.nn.sigmoid(g0_ref[...] + b_ref[0:1, :])
        g1 = jax.nn.sigmoid(g1_ref[...] + b_ref[1:2, :])
        dm = d_ref[...]
        da_ref[...] = (dm * g0).astype(BF16)
        dp_ref[...] = (dm * g1).astype(BF16)
        dl0 = dm * a_ref[...] * g0 * (1.0 - g0)
        dl1 = dm * p_ref[...] * g1 * (1.0 - g1)
        dl0_ref[...] = dl0.astype(BF16)
        dl1_ref[...] = dl1.astype(BF16)

        @pl.when(pl.program_id(1) == 0)
        def _():
            db_ref[...] = jnp.zeros_like(db_ref)

        db_ref[...] += jnp.concatenate([jnp.sum(dl0, axis=0, keepdims=True), jnp.sum(dl1, axis=0, keepdims=True)], axis=0)

    blk = pl.BlockSpec((tr, tc), lambda j, i: (i, j))
    big = _sds((tp, d), BF16)
    return pl.pallas_call(
        body,
        name="mix_bwd",
        grid=(d // tc, tp // tr),
        in_specs=[pl.BlockSpec((tr, tc), lambda j, i: (i, c0 + j)), pl.BlockSpec((tr, tc), lambda j, i: (i, c1 + j)),
                  pl.BlockSpec((2, tc), lambda j, i: (0, j)), blk, blk, blk],
        out_specs=[blk, blk, blk, blk, pl.BlockSpec((2, tc), lambda j, i: (0, j))],
        out_shape=[big, big, big, big, _sds((2, d), F32)],
        compiler_params=_cparams(("parallel", "arbitrary")),
    )(proj, proj, b_gate, a_out, p_out, d_mixed)


SWIGLU_ROWS = 64


def _swiglu_fwd(ff):
    tp, f2 = ff.shape
    f = f2 // 2
    tr = _pick(tp, SWIGLU_ROWS)

    def body(x_ref, o_ref):
        gate, up = x_ref[:, :f], x_ref[:, f:]
        o_ref[...] = (gate * jax.nn.sigmoid(gate) * up).astype(BF16)

    return pl.pallas_call(
        body,
        name="swiglu_fwd",
        grid=(tp // tr,),
        in_specs=[pl.BlockSpec((tr, f2), lambda i: (i, 0))],
        out_specs=pl.BlockSpec((tr, f), lambda i: (i, 0)),
        out_shape=_sds((tp, f), BF16),
        compiler_params=_cparams(("parallel",)),
    )(ff)


def _swiglu_bwd(ff, d_act):
    tp, f2 = ff.shape
    f = f2 // 2
    tr = _pick(tp, SWIGLU_ROWS)

    def body(x_ref, d_ref, o_ref):
        gate, up = x_ref[:, :f], x_ref[:, f:]
        d = d_ref[...]
        sg = jax.nn.sigmoid(gate)
        silu = gate * sg
        o_ref[:, :f] = (d * up * (sg + silu * (1.0 - sg))).astype(BF16)
        o_ref[:, f:] = (d * silu).astype(BF16)

    return pl.pallas_call(
        body,
        name="swiglu_bwd",
        grid=(tp // tr,),
        in_specs=[pl.BlockSpec((tr, f2), lambda i: (i, 0)), pl.BlockSpec((tr, f), lambda i: (i, 0))],
        out_specs=pl.BlockSpec((tr, f2), lambda i: (i, 0)),
        out_shape=_sds((tp, f2), BF16),
        compiler_params=_cparams(("parallel",)),
    )(ff, d_act)


def _tile2(rows, cols, max_bytes=3 << 20):
    tc = _pick(cols, 1024, 640, 512)
    for tr in (512, 344, 256, 128, 64, 32, 16, 8):
        if rows % tr == 0 and tr * tc * 4 <= max_bytes:
            return tr, tc
    return rows, tc


def _cast_bf16(w, name):
    r, c = w.shape
    tr, tc = _tile2(r, c)

    def body(x_ref, o_ref):
        o_ref[...] = x_ref[...].astype(BF16)

    blk = pl.BlockSpec((tr, tc), lambda i, j: (i, j))
    return pl.pallas_call(
        body, name=name, grid=(r // tr, c // tc), in_specs=[blk], out_specs=blk, out_shape=_sds((r, c), BF16),
        compiler_params=_cparams(("parallel", "parallel")),
    )(w)


def _adamw(w, g, m, v, name):
    r, c = w.shape
    tr, tc = _tile2(r, c, 1 << 20)

    def body(w_ref, g_ref, m_ref, v_ref, d_ref, nm_ref, nv_ref):
        gg = g_ref[...]
        nm = ADAM_B1 * m_ref[...] + (1.0 - ADAM_B1) * gg
        nv = ADAM_B2 * v_ref[...] + (1.0 - ADAM_B2) * jnp.square(gg)
        m_hat = nm / (1.0 - ADAM_B1 ** ADAM_STEP)
        v_hat = nv / (1.0 - ADAM_B2 ** ADAM_STEP)
        d_ref[...] = -ADAM_LR * (m_hat / (jnp.sqrt(v_hat) + ADAM_EPS) + ADAM_WD * w_ref[...])
        nm_ref[...] = nm
        nv_ref[...] = nv

    blk = pl.BlockSpec((tr, tc), lambda i, j: (i, j))
    shp = _sds((r, c), F32)
    return pl.pallas_call(
        body, name=name, grid=(r // tr, c // tc), in_specs=[blk] * 4, out_specs=[blk] * 3, out_shape=[shp] * 3,
        compiler_params=_cparams(("parallel", "parallel")),
    )(w, g, m, v)


def _coords():
    return lax.axis_index("x"), lax.axis_index("y"), lax.axis_index("c")


def _other_chips(x, y):
    return [(1 - x, y), (x, 1 - y), (1 - x, 1 - y)]


def _full_shape(kind, shard_shape):
    r, c = shard_shape
    return (r, N_CHIPS * c) if kind == "col" else (N_CHIPS * r, c)


def _piece_shape(shard_shape):
    return (shard_shape[0] // 2, shard_shape[1])


def _piece_of_full(kind, ref, shard_shape, s, h):
    r, c = shard_shape
    if kind == "col":
        return ref.at[pl.ds(h * (r // 2), r // 2), pl.ds(s * c, c)]
    return ref.at[pl.ds(s * r + h * (r // 2), r // 2), :]


def _piece_of_shard(ref, shard_shape, h):
    r = shard_shape[0]
    return ref.at[pl.ds(h * (r // 2), r // 2), :]


def _all_gather_weights(shards, kinds):
    nt = len(shards)
    shapes = [s.shape for s in shards]

    def body(*refs):
        ins, outs = refs[:nt], refs[nt:2 * nt]
        send_sems, recv_sems, local_sems = refs[2 * nt:]
        x, y, c = _coords()
        s_me = 2 * x + y
        chips = _other_chips(x, y)

        def dst(t, s, h):
            return _piece_of_full(kinds[t], outs[t], shapes[t], s, h)

        def rcopy(t, k, src, dst_ref, to):
            return pltpu.make_async_remote_copy(src_ref=src, dst_ref=dst_ref, send_sem=send_sems.at[t * 6 + k],
                                                recv_sem=recv_sems.at[t * 6 + k], device_id=to, device_id_type=MESH)

        local = []
        for t in range(nt):
            for h in range(2):
                cp = pltpu.make_async_copy(_piece_of_shard(ins[t], shapes[t], h), dst(t, s_me, h), local_sems.at[2 * t + h])
                cp.start()
                local.append(cp)
        sent = []
        for t in range(nt):
            for j, (ox, oy) in enumerate(chips):
                cp = rcopy(t, j, _piece_of_shard(ins[t], shapes[t], c), dst(t, s_me, c), (ox, oy, c))
                cp.start()
                sent.append(cp)
        for j, (ox, oy) in enumerate(chips):
            s_j = 2 * ox + oy
            for t in range(nt):
                rcopy(t, j, dst(t, s_j, c), dst(t, s_j, c), (x, y, c)).wait_recv()
                cp = rcopy(t, 3 + j, dst(t, s_j, c), dst(t, s_j, c), (x, y, 1 - c))
                cp.start()
                sent.append(cp)
        for j, (ox, oy) in enumerate(chips):
            s_j = 2 * ox + oy
            for t in range(nt):
                rcopy(t, 3 + j, dst(t, s_j, 1 - c), dst(t, s_j, 1 - c), (x, y, c)).wait_recv()
        for cp in sent:
            cp.wait_send()
        for cp in local:
            cp.wait()

    return pl.pallas_call(
        body,
        name="all_gather_weights",
        in_specs=[ANY] * nt,
        out_specs=[ANY] * nt,
        out_shape=[_sds(_full_shape(kinds[t], shapes[t]), BF16) for t in range(nt)],
        scratch_shapes=[pltpu.SemaphoreType.DMA((6 * nt,)), pltpu.SemaphoreType.DMA((6 * nt,)), pltpu.SemaphoreType.DMA((2 * nt,))],
    )(*shards)


def _sibling_exchange_grads(grads, kinds, shard_shapes):
    nt = len(grads)

    def body(*refs):
        ins, outs = refs[:nt], refs[nt:2 * nt]
        send_sems, recv_sems = refs[2 * nt:]
        x, y, c = _coords()
        copies = []
        for t in range(nt):
            for s in range(N_CHIPS):
                cp = pltpu.make_async_remote_copy(
                    src_ref=_piece_of_full(kinds[t], ins[t], shard_shapes[t], s, 1 - c), dst_ref=outs[t].at[s],
                    send_sem=send_sems.at[t * N_CHIPS + s], recv_sem=recv_sems.at[t * N_CHIPS + s],
                    device_id=(x, y, 1 - c), device_id_type=MESH)
                cp.start()
                copies.append(cp)
        for cp in copies:
            cp.wait()

    return pl.pallas_call(
        body,
        name="rs_sibling_exchange",
        in_specs=[ANY] * nt,
        out_specs=[ANY] * nt,
        out_shape=[_sds((N_CHIPS,) + _piece_shape(shard_shapes[t]), F32) for t in range(nt)],
        scratch_shapes=[pltpu.SemaphoreType.DMA((N_CHIPS * nt,)), pltpu.SemaphoreType.DMA((N_CHIPS * nt,))],
    )(*grads)


def _chip_exchange_grads(chip_sums):
    nt = len(chip_sums)

    def body(*refs):
        ins, outs = refs[:nt], refs[nt:2 * nt]
        send_sems, recv_sems = refs[2 * nt:]
        x, y, c = _coords()
        s_me = 2 * x + y
        copies = []
        for t in range(nt):
            for j, (ox, oy) in enumerate(_other_chips(x, y)):
                cp = pltpu.make_async_remote_copy(
                    src_ref=ins[t].at[2 * ox + oy], dst_ref=outs[t].at[s_me],
                    send_sem=send_sems.at[t * 3 + j], recv_sem=recv_sems.at[t * 3 + j],
                    device_id=(ox, oy, c), device_id_type=MESH)
                cp.start()
                copies.append(cp)
        for cp in copies:
            cp.wait()

    return pl.pallas_call(
        body,
        name="rs_chip_exchange",
        in_specs=[ANY] * nt,
        out_specs=[ANY] * nt,
        out_shape=[_sds(a.shape, a.dtype) for a in chip_sums],
        scratch_shapes=[pltpu.SemaphoreType.DMA((3 * nt,)), pltpu.SemaphoreType.DMA((3 * nt,))],
    )(*chip_sums)


def _sibling_share_reduced(reduced):
    nt = len(reduced)
    shapes = [r.shape for r in reduced]

    def body(*refs):
        ins = refs[:nt]
        send_sems, recv_sems = refs[2 * nt:]
        x, y, c = _coords()
        copies = []
        for t in range(nt):
            cp = pltpu.make_async_remote_copy(
                src_ref=_piece_of_shard(ins[t], shapes[t], c), dst_ref=_piece_of_shard(ins[t], shapes[t], c),
                send_sem=send_sems.at[t], recv_sem=recv_sems.at[t], device_id=(x, y, 1 - c), device_id_type=MESH)
            cp.start()
            copies.append(cp)
        for t, cp in enumerate(copies):
            cp.wait_send()
            pltpu.make_async_remote_copy(
                src_ref=_piece_of_shard(ins[t], shapes[t], 1 - c), dst_ref=_piece_of_shard(ins[t], shapes[t], 1 - c),
                send_sem=send_sems.at[t], recv_sem=recv_sems.at[t], device_id=(x, y, c), device_id_type=MESH).wait_recv()

    return pl.pallas_call(
        body,
        name="rs_sibling_share",
        in_specs=[ANY] * nt,
        out_specs=[ANY] * nt,
        out_shape=[_sds(s, F32) for s in shapes],
        input_output_aliases={t: t for t in range(nt)},
        scratch_shapes=[pltpu.SemaphoreType.DMA((nt,)), pltpu.SemaphoreType.DMA((nt,))],
    )(*reduced)


def _piece_block_index(kind, shard_shape, tr, tc):
    r, c = shard_shape
    if kind == "col":
        return lambda s, h, i, j: (h * (r // 2 // tr) + i, s * (c // tc) + j)
    return lambda s, h, i, j: ((s * r + h * (r // 2)) // tr + i, j)


def _chip_sum(grad, other, kind, shard_shape, who, name):
    pr, pc = _piece_shape(shard_shape)
    tr, tc = _tile2(pr, pc)
    full_idx = _piece_block_index(kind, shard_shape, tr, tc)

    def body(who_ref, g_ref, o_ref, out_ref):
        out_ref[...] = (g_ref[...] + o_ref[...]).astype(BF16)

    slot = pl.BlockSpec((None, tr, tc), lambda s, i, j, who_ref: (s, i, j))
    return pl.pallas_call(
        body,
        name=name,
        grid_spec=pltpu.PrefetchScalarGridSpec(
            num_scalar_prefetch=1,
            grid=(N_CHIPS, pr // tr, pc // tc),
            in_specs=[pl.BlockSpec((tr, tc), lambda s, i, j, who_ref: full_idx(s, who_ref[0], i, j)), slot],
            out_specs=slot,
        ),
        out_shape=_sds((N_CHIPS, pr, pc), BF16),
        compiler_params=_cparams(("parallel", "parallel", "parallel")),
    )(who, grad, other)


def _final_sum(grad, other, landed, kind, shard_shape, who, name):
    pr, pc = _piece_shape(shard_shape)
    tr, tc = _tile2(pr, pc)
    full_idx = _piece_block_index(kind, shard_shape, tr, tc)

    def body(who_ref, g_ref, o_ref, l1_ref, l2_ref, l3_ref, out_ref):
        acc = g_ref[...] + o_ref[...]
        for l_ref in (l1_ref, l2_ref, l3_ref):
            acc = acc + l_ref[...].astype(F32)
        out_ref[...] = acc

    def landed_spec(k):
        return pl.BlockSpec((None, tr, tc), lambda i, j, who_ref: (who_ref[1 + k], i, j))

    return pl.pallas_call(
        body,
        name=name,
        grid_spec=pltpu.PrefetchScalarGridSpec(
            num_scalar_prefetch=1,
            grid=(pr // tr, pc // tc),
            in_specs=[pl.BlockSpec((tr, tc), lambda i, j, who_ref: full_idx(who_ref[1], who_ref[0], i, j)),
                      pl.BlockSpec((None, tr, tc), lambda i, j, who_ref: (who_ref[1], i, j)),
                      landed_spec(1), landed_spec(2), landed_spec(3)],
            out_specs=pl.BlockSpec((tr, tc), lambda i, j, who_ref: (who_ref[0] * (pr // tr) + i, j)),
        ),
        out_shape=_sds(shard_shape, F32),
        compiler_params=_cparams(("parallel", "parallel")),
    )(who, grad, other, landed, landed, landed)


def _reduce_scatter(grads, kinds, shard_shapes, names):
    x, y, c = _coords()
    s_me = 2 * x + y
    who = jnp.stack([c, s_me, (s_me + 1) % N_CHIPS, (s_me + 2) % N_CHIPS, (s_me + 3) % N_CHIPS]).astype(jnp.int32)
    others = _sibling_exchange_grads(grads, kinds, shard_shapes)
    chip_sums = [_chip_sum(grads[t], others[t], kinds[t], shard_shapes[t], who, "chip_sum_" + names[t]) for t in range(len(grads))]
    landed = _chip_exchange_grads(chip_sums)
    halves = [_final_sum(grads[t], others[t], landed[t], kinds[t], shard_shapes[t], who, "final_sum_" + names[t])
              for t in range(len(grads))]
    return _sibling_share_reduced(halves)


def _gather_small(packed):
    r, c = packed.shape

    def body(in_ref, out_ref, send_sems, recv_sems):
        x, y, c_ = _coords()
        s_me = 2 * x + y
        out_ref[s_me] = in_ref[...]
        copies = []
        for j, (ox, oy) in enumerate(_other_chips(x, y)):
            cp = pltpu.make_async_remote_copy(src_ref=in_ref, dst_ref=out_ref.at[s_me], send_sem=send_sems.at[j],
                                              recv_sem=recv_sems.at[j], device_id=(ox, oy, c_), device_id_type=MESH)
            cp.start()
            copies.append(cp)
        for j, (ox, oy) in enumerate(_other_chips(x, y)):
            copies[j].wait_send()
            pltpu.make_async_remote_copy(src_ref=in_ref, dst_ref=out_ref.at[2 * ox + oy], send_sem=send_sems.at[j],
                                         recv_sem=recv_sems.at[j], device_id=(x, y, c_), device_id_type=MESH).wait_recv()

    return pl.pallas_call(
        body,
        name="gather_small",
        in_specs=[VMEM_FULL],
        out_specs=VMEM_FULL,
        out_shape=_sds((N_CHIPS, r, c), F32),
        scratch_shapes=[pltpu.SemaphoreType.DMA((3,)), pltpu.SemaphoreType.DMA((3,))],
    )(packed)


N_DEV = 8


def _all_reduce_small(packed):
    r, c = packed.shape

    def body(in_ref, out_ref, slots, send_sems, recv_sems):
        x, y, c_ = _coords()
        me = 4 * x + 2 * y + c_
        slots[me] = in_ref[...]
        copies = []
        for k in range(1, N_DEV):
            peer = me ^ k
            cp = pltpu.make_async_remote_copy(src_ref=in_ref, dst_ref=slots.at[me], send_sem=send_sems.at[k - 1],
                                              recv_sem=recv_sems.at[k - 1],
                                              device_id=(peer // 4, (peer // 2) % 2, peer % 2), device_id_type=MESH)
            cp.start()
            copies.append(cp)
        for k in range(1, N_DEV):
            copies[k - 1].wait_send()
            pltpu.make_async_remote_copy(src_ref=in_ref, dst_ref=slots.at[me ^ k], send_sem=send_sems.at[k - 1],
                                         recv_sem=recv_sems.at[k - 1], device_id=(x, y, c_), device_id_type=MESH).wait_recv()
        acc = slots[0]
        for d in range(1, N_DEV):
            acc = acc + slots[d]
        out_ref[...] = acc

    return pl.pallas_call(
        body,
        name="all_reduce_small",
        in_specs=[VMEM_FULL],
        out_specs=VMEM_FULL,
        out_shape=_sds((r, c), F32),
        scratch_shapes=[pltpu.VMEM((N_DEV, r, c), F32), pltpu.SemaphoreType.DMA((N_DEV - 1,)), pltpu.SemaphoreType.DMA((N_DEV - 1,))],
    )(packed)


def _rows_of(a, width):
    flat = a.reshape(-1)
    n = -(-flat.shape[0] // width) * width
    return jnp.pad(flat, (0, n - flat.shape[0])).reshape(-1, width)


def _pad_rows(a, mult=8):
    n = -(-a.shape[0] // mult) * mult
    return jnp.pad(a, ((0, n - a.shape[0]), (0, 0)))


def _heads_major(a, nh):
    tp = a.shape[0]
    return a.reshape(tp, nh, HEAD_DIM).transpose(1, 0, 2)


def _heads_minor(a):
    nh, tp, hd = a.shape
    return a.transpose(1, 0, 2).reshape(tp, nh * hd)


def kernel(x, meta_tokens, ln_in_g, ln_in_b, w_in, b_gate, attn_sinks, w_attn_up, w_pool_grp, pool_scale, w_pool_up, w_out, ln1_g, ln1_b, w_ffn_in, w_ffn_down, ln2_g, ln2_b, loss_target, m_meta_tokens, m_ln_in_g, m_ln_in_b, m_w_in, m_b_gate, m_attn_sinks, m_w_attn_up, m_w_pool_grp, m_pool_scale, m_w_pool_up, m_w_out, m_ln1_g, m_ln1_b, m_w_ffn_in, m_w_ffn_down, m_ln2_g, m_ln2_b, v_meta_tokens, v_ln_in_g, v_ln_in_b, v_w_in, v_b_gate, v_attn_sinks, v_w_attn_up, v_w_pool_grp, v_pool_scale, v_w_pool_up, v_w_out, v_ln1_g, v_ln1_b, v_w_ffn_in, v_w_ffn_down, v_ln2_g, v_ln2_b):
    seq, d = x.shape[1], x.shape[2]
    tp = LEAD + N_META + seq
    nb = tp // BLK
    nq = attn_sinks.shape[1]
    grp = nq // N_KV
    attn_w = nq * HEAD_DIM
    kv_w = N_KV * HEAD_DIM
    qkv_w = attn_w + 2 * kv_w
    pool_w = pool_scale.shape[1]
    gw = pool_w // N_GRP
    g_off = qkv_w + pool_w
    dc = d // N_CHIPS
    cx, cy, cc = _coords()
    s_me = 2 * cx + cy

    names = ["w_in", "w_attn_up", "w_pool_grp", "w_pool_up", "w_out", "w_ffn_in", "w_ffn_down"]
    kinds = ["col", "col", "row", "col", "row", "col", "row"]
    big_w = [w_in[0], w_attn_up[0], w_pool_grp[0].reshape(N_GRP * (gw // N_CHIPS), gw), w_pool_up[0], w_out[0], w_ffn_in[0], w_ffn_down[0]]
    big_m = [m_w_in[0], m_w_attn_up[0], m_w_pool_grp[0].reshape(big_w[2].shape), m_w_pool_up[0], m_w_out[0], m_w_ffn_in[0], m_w_ffn_down[0]]
    big_v = [v_w_in[0], v_w_attn_up[0], v_w_pool_grp[0].reshape(big_w[2].shape), v_w_pool_up[0], v_w_out[0], v_w_ffn_in[0], v_w_ffn_down[0]]
    shard_shapes = [w.shape for w in big_w]
    full = _all_gather_weights([_cast_bf16(w, "cast_" + n) for w, n in zip(big_w, names)], kinds)
    wf_in, wf_attn_up, wf_grp_sm, wf_pool_up, wf_out, wf_ffn_in, wf_ffn_down = full
    wf_grp = wf_grp_sm.reshape(N_CHIPS, N_GRP, gw // N_CHIPS, gw).transpose(1, 0, 2, 3).reshape(N_GRP, gw, gw)

    small_rows = _pad_rows(jnp.concatenate([meta_tokens, b_gate[0]], axis=0))
    gathered = _gather_small(small_rows)
    gathered = gathered.transpose(1, 0, 2).reshape(small_rows.shape[0], d)
    meta_full, b_gate_full = gathered[:N_META], gathered[N_META:N_META + 2]

    x2d, tgt2d = x[0], loss_target[0]
    g_in, b_in = ln_in_g.reshape(1, d), ln_in_b.reshape(1, d)
    h0, h0b = _ln_in_fwd(x2d, meta_full, g_in, b_in, nb)
    proj = _mm(h0b, wf_in, "nn", F32, 1408, 512, 4096, "mm_proj")
    cos, sin = _rope_tables(tp)
    n_rot = (attn_w + kv_w) // LANES
    qkv = _rope_fwd(proj, cos, sin, n_rot, qkv_w)
    q_hm = _heads_major(qkv[:, :attn_w], nq)
    k_hm = _heads_major(qkv[:, attn_w:attn_w + kv_w], N_KV)
    v_hm = _heads_major(qkv[:, attn_w + kv_w:], N_KV)
    sink4 = jnp.broadcast_to(attn_sinks.reshape(N_KV, grp, 1, 1), (N_KV, grp, BLK, 1))
    o = _heads_minor(_attn_fwd(q_hm, k_hm, v_hm, sink4))
    pooled, mx, pm = _pool_fwd(proj, wf_grp, pool_scale, qkv_w, pool_w)
    a_out = _mm(o, wf_attn_up, "nn", F32, 1408, 1024, 2048, "mm_attn_up")
    p_out = _mm(pm, wf_pool_up, "nn", F32, 1408, 1024, 2048, "mm_pool_up")
    mixed = _mix_fwd(proj, b_gate_full, a_out, p_out, g_off)
    z1 = _mm(mixed, wf_out, "nn", F32, 1408, 512, 4096, "mm_out")
    r1, h1, h1b = _res_ln_fwd(h0, z1, ln1_g, ln1_b)
    ff = _mm(h1b, wf_ffn_in, "nn", F32, 1408, 512, 4096, "mm_ffn_in")
    act = _swiglu_fwd(ff)
    z2 = _mm(act, wf_ffn_down, "nn", F32, 704, 512, 5504, "mm_ffn_down")
    d_r2, d_r2b, loss_tile, dg2, db2 = _final_ln_loss(h1, z2, ln2_g, ln2_b, tgt2d)

    d_act = _mm(d_r2b, wf_ffn_down, "nt", F32, 1408, 256, 4096, "mm_d_act")
    gw_ffn_down = _mm(act, d_r2b, "tn", F32, 256, 1024, tp, "mm_gw_ffn_down", j_outer=True)
    d_ff = _swiglu_bwd(ff, d_act)
    d_h1_mm = _mm(d_ff, wf_ffn_in, "nt", F32, 1408, 2048, 512, "mm_d_h1")
    gw_ffn_in = _mm(h1b, d_ff, "tn", F32, 1024, 512, tp, "mm_gw_ffn_in")
    d_r1, d_r1b, dg1, db1 = _ln1_bwd(d_r2, d_h1_mm, r1, ln1_g)
    d_mixed = _mm(d_r1b, wf_out, "nt", F32, 1408, 512, 4096, "mm_d_mixed")
    gw_out = _mm(mixed, d_r1b, "tn", F32, 1024, 512, tp, "mm_gw_out")
    d_a, d_p, d_gl0, d_gl1, d_bgate = _mix_bwd(proj, b_gate_full, a_out, p_out, d_mixed, g_off)
    d_o = _mm(d_a, wf_attn_up, "nt", F32, 1408, 512, 4096, "mm_d_o")
    gw_attn_up = _mm(o, d_a, "tn", F32, 1024, 512, tp, "mm_gw_attn_up")
    d_pm = _mm(d_p, wf_pool_up, "nt", F32, 1408, 512, 4096, "mm_d_pm")
    gw_pool_up = _mm(pm, d_p, "tn", F32, 1024, 512, tp, "mm_gw_pool_up")
    d_pooled, gw_grp, d_scale = _pool_bwd_mix(d_pm, mx, pooled, wf_grp, pool_scale)
    d_u = _pool_bwd_band(d_pooled)
    dq_hm, dk_cur, dk_prev, dk_meta, dv_cur, dv_prev, dv_meta, d_sink = _attn_bwd(q_hm, k_hm, v_hm, sink4, _heads_major(d_o, nq))
    d_qkv = _rope_bwd(_heads_minor(dq_hm), _heads_minor(dk_cur), _heads_minor(dk_prev), _heads_minor(dk_meta),
                      _heads_minor(dv_cur), _heads_minor(dv_prev), _heads_minor(dv_meta), cos, sin)
    d_proj = jnp.concatenate([d_qkv, d_u, d_gl0, d_gl1], axis=1)
    d_h0_mm = _mm(d_proj, wf_in, "nt", F32, 1408, 1024, 2560, "mm_d_h0")
    gw_in = _mm(h0b, d_proj, "tn", F32, 1024, 512, tp, "mm_gw_in")
    grad_x2d, d_meta, dg_in, db_in = _ln_in_bwd(d_r1, d_h0_mm, x2d, meta_full, g_in)

    gw_grp_sm = gw_grp.reshape(N_GRP, N_CHIPS, gw // N_CHIPS, gw).transpose(1, 0, 2, 3).reshape(N_CHIPS * N_GRP * (gw // N_CHIPS), gw)
    big_g = _reduce_scatter([gw_in, gw_attn_up, gw_grp_sm, gw_pool_up, gw_out, gw_ffn_in, gw_ffn_down], kinds, shard_shapes, names)

    small_parts = [d_meta, d_bgate, dg_in, db_in, dg1, db1, dg2, db2, _rows_of(d_scale, d), _rows_of(d_sink[:, :, 0, 0], d)]
    offs = [0]
    for p in small_parts:
        offs.append(offs[-1] + p.shape[0])
    red = _all_reduce_small(_pad_rows(jnp.concatenate(small_parts, axis=0)))
    r_meta, r_bgate, r_g_in, r_b_in, r_g1, r_b1, r_g2, r_b2, r_scale, r_sink = [red[offs[k]:offs[k + 1]] for k in range(len(small_parts))]
    col0 = s_me * dc
    g_meta = lax.dynamic_slice(r_meta, (0, col0), (N_META, dc))
    g_bgate = lax.dynamic_slice(r_bgate, (0, col0), (2, dc))
    g_scale = r_scale.reshape(-1)[:pool_w]
    g_sink = r_sink.reshape(-1)[:nq]

    big_upd = [_adamw(big_w[t], big_g[t], big_m[t], big_v[t], "adamw_" + names[t]) for t in range(len(names))]

    small_w = [meta_tokens, b_gate[0], ln_in_g, ln_in_b, attn_sinks, pool_scale, ln1_g, ln1_b, ln2_g, ln2_b]
    small_m = [m_meta_tokens, m_b_gate[0], m_ln_in_g, m_ln_in_b, m_attn_sinks, m_pool_scale, m_ln1_g, m_ln1_b, m_ln2_g, m_ln2_b]
    small_v = [v_meta_tokens, v_b_gate[0], v_ln_in_g, v_ln_in_b, v_attn_sinks, v_pool_scale, v_ln1_g, v_ln1_b, v_ln2_g, v_ln2_b]
    small_g = [g_meta, g_bgate, r_g_in, r_b_in, g_sink, g_scale, r_g1, r_b1, r_g2, r_b2]
    small_g = [g.reshape(w.shape) for g, w in zip(small_g, small_w)]

    def pack(parts):
        return _pad_rows(jnp.concatenate([_rows_of(p, dc) for p in parts], axis=0))

    s_delta, s_m, s_v = _adamw(pack(small_w), pack(small_g), pack(small_m), pack(small_v), "adamw_small")

    def unpack(packed):
        out, row = [], 0
        for w in small_w:
            nrow = -(-w.size // dc)
            out.append(packed[row:row + nrow].reshape(-1)[:w.size].reshape(w.shape))
            row += nrow
        return out

    s_delta, s_m, s_v = unpack(s_delta), unpack(s_m), unpack(s_v)

    order = ["meta_tokens", "ln_in_g", "ln_in_b", "w_in", "b_gate", "attn_sinks", "w_attn_up", "w_pool_grp", "pool_scale",
             "w_pool_up", "w_out", "ln1_g", "ln1_b", "w_ffn_in", "w_ffn_down", "ln2_g", "ln2_b"]
    small_names = ["meta_tokens", "b_gate", "ln_in_g", "ln_in_b", "attn_sinks", "pool_scale", "ln1_g", "ln1_b", "ln2_g", "ln2_b"]
    out_shapes = dict(meta_tokens=meta_tokens.shape, ln_in_g=ln_in_g.shape, ln_in_b=ln_in_b.shape, w_in=w_in.shape, b_gate=b_gate.shape,
                      attn_sinks=attn_sinks.shape, w_attn_up=w_attn_up.shape, w_pool_grp=w_pool_grp.shape, pool_scale=pool_scale.shape,
                      w_pool_up=w_pool_up.shape, w_out=w_out.shape, ln1_g=ln1_g.shape, ln1_b=ln1_b.shape, w_ffn_in=w_ffn_in.shape,
                      w_ffn_down=w_ffn_down.shape, ln2_g=ln2_g.shape, ln2_b=ln2_b.shape)
    grads, deltas, new_m, new_v = {}, {}, {}, {}
    for t, n in enumerate(names):
        grads[n], (deltas[n], new_m[n], new_v[n]) = big_g[t], big_upd[t]
    for k, n in enumerate(small_names):
        grads[n], deltas[n], new_m[n], new_v[n] = small_g[k], s_delta[k], s_m[k], s_v[k]

    loss = lax.psum(loss_tile[0, 0], ("x", "y", "c"))
    outs = [loss, grad_x2d.reshape(x.shape)]
    for group in (grads, deltas, new_m, new_v):
        outs += [group[n].reshape(out_shapes[n]) for n in order]
    return tuple(outs)
```

```python
import functools
import math

import jax
import jax.numpy as jnp
from jax import lax
from jax.experimental import pallas as pl
from jax.experimental.pallas import tpu as pltpu

F32 = jnp.float32
BF16 = jnp.bfloat16
MESH = pl.DeviceIdType.MESH
ANY = pl.BlockSpec(memory_space=pl.ANY)
VMEM_FULL = pl.BlockSpec(memory_space=pltpu.VMEM)

N_META = 16
HEAD_DIM = 64
N_KV = 4
BLK = 128
LEAD = (-N_META) % BLK
ROPE_DIM = HEAD_DIM // 4
ROPE_THETA = 500000.0
NEG_INF = -1e30
POOL_WINDOWS = (2, 4, 8, 16)
N_GRP = len(POOL_WINDOWS)
LN_EPS = 1e-5
DN_ALPHA = 2.0 ** 0.25
ADAM_LR = 0.001
ADAM_B1 = 0.9
ADAM_B2 = 0.999
ADAM_EPS = 1e-08
ADAM_WD = 0.01
ADAM_STEP = 10
N_CHIPS = 4
N_DEV = 8
LANES = 128
VMEM_LIMIT_MB = 56
HI = lax.Precision.HIGHEST


def _cparams(sem=None, vmem_mb=VMEM_LIMIT_MB):
    kw = dict(vmem_limit_bytes=vmem_mb << 20)
    if sem is not None:
        kw["dimension_semantics"] = sem
    return pltpu.CompilerParams(**kw)


def _pick(dim, *cands):
    for c in cands:
        if c <= dim and dim % c == 0:
            return c
    return dim


def _sds(shape, dtype):
    return jax.ShapeDtypeStruct(tuple(shape), dtype)


def _coords():
    return lax.axis_index("x"), lax.axis_index("y"), lax.axis_index("c")


def _other_chips(x, y):
    return [(1 - x, y), (x, 1 - y), (1 - x, 1 - y)]


class _Win:
    def __init__(self, kind, shard_shape, row0=0, nrows=None):
        self.kind, self.shard_shape, self.row0 = kind, tuple(shard_shape), row0
        self.nrows = shard_shape[0] if nrows is None else nrows
        self.half = self.nrows // 2

    @property
    def piece_shape(self):
        return (self.half, self.shard_shape[1])

    @property
    def full_shape(self):
        r, c = self.shard_shape
        return (r, N_CHIPS * c) if self.kind == "col" else (N_CHIPS * r, c)

    def in_full(self, ref, s, h):
        r, c = self.shard_shape
        if self.kind == "col":
            return ref.at[pl.ds(self.row0 + h * self.half, self.half), pl.ds(s * c, c)]
        return ref.at[pl.ds(s * r + self.row0 + h * self.half, self.half), :]

    def in_shard(self, ref, h):
        return ref.at[pl.ds(self.row0 + h * self.half, self.half), :]

    def split(self, n):
        return [_Win(self.kind, self.shard_shape, self.row0 + q * (self.nrows // n), self.nrows // n) for q in range(n)]


class _Weight:
    def __init__(self, name, kind, shard):
        self.name, self.kind, self.shard, self.full = name, kind, shard, None
        self.win = _Win(kind, shard.shape)


class _Grad:
    def __init__(self, name, win, grad):
        self.name, self.win, self.grad = name, win, grad
        self.key = "%s@%d" % (name, win.row0)
        self.other = self.chip = self.landed = None


class _Shard:
    def __init__(self, name):
        self.name, self.arr = name, None


class _Ctx:
    def __init__(self, side, in_refs, out_refs, send_sems, recv_sems, local_sems, base, lbase):
        self.side, self.in_refs, self.out_refs = side, in_refs, out_refs
        self.send_sems, self.recv_sems, self.local_sems, self.base, self.lbase = send_sems, recv_sems, local_sems, base, lbase

    def ref(self, key):
        info = self.side.info[key]
        return self.in_refs[info["in"]] if info["in"] is not None else self.out_refs[info["out"]]

    def remote(self, k, src, dst, to):
        return pltpu.make_async_remote_copy(src_ref=src, dst_ref=dst, send_sem=self.send_sems.at[self.base + k],
                                            recv_sem=self.recv_sems.at[self.base + k], device_id=to, device_id_type=MESH)

    def local(self, k, src, dst):
        return pltpu.make_async_copy(src, dst, self.local_sems.at[self.lbase + k])


class _Side:
    def __init__(self, ops):
        self.ops, self.info, self.keys = ops, {}, []
        self.nsem = self.nlocal = 0
        self.bases = []
        for op in ops:
            op.register(self)
            self.bases.append((self.nsem, self.nlocal))
            self.nsem += op.nsem
            self.nlocal += op.nlocal
        self.inputs, self.out_shape, self.aliases = [], [], {}
        for key in self.keys:
            info = self.info[key]
            info["in"] = info["out"] = None
            if info["arr"] is not None:
                info["in"] = len(self.inputs)
                self.inputs.append(info["arr"])
            if info["write"]:
                info["out"] = len(self.out_shape)
                self.out_shape.append(info["sds"])
                if info["in"] is not None:
                    self.aliases[info["in"]] = info["out"]

    def need(self, key, arr=None, sds=None, write=False):
        if key not in self.info:
            self.keys.append(key)
            self.info[key] = dict(arr=arr, sds=sds if arr is None else _sds(arr.shape, arr.dtype), write=write)
        else:
            self.info[key]["write"] = self.info[key]["write"] or write

    def _ctx(self, k, in_refs, out_refs, sems):
        return _Ctx(self, in_refs, out_refs, sems[0], sems[1], sems[2], *self.bases[k])

    def start(self, in_refs, out_refs, sems):
        for k, op in enumerate(self.ops):
            op.start(self._ctx(k, in_refs, out_refs, sems))

    def finish(self, in_refs, out_refs, sems):
        for k, op in enumerate(self.ops):
            op.finish(self._ctx(k, in_refs, out_refs, sems))

    def scratch(self):
        return [pltpu.SemaphoreType.DMA((max(self.nsem, 1),)), pltpu.SemaphoreType.DMA((max(self.nsem, 1),)),
                pltpu.SemaphoreType.DMA((max(self.nlocal, 1),))]

    def commit(self, outs):
        res = {key: outs[self.info[key]["out"]] for key in self.keys if self.info[key]["write"]}
        for op in self.ops:
            op.commit(res)


class _GatherIci:
    def __init__(self, pairs):
        self.pairs = pairs
        self.nsem, self.nlocal = 3 * len(pairs), 2 * len(pairs)

    def register(self, side):
        for w, win in self.pairs:
            side.need(("shard", w.name), arr=w.shard)
            side.need(("full", w.name), arr=w.full, sds=_sds(win.full_shape, BF16), write=True)

    def _copies(self, ctx):
        x, y, c = _coords()
        s_me = 2 * x + y
        local, sends, recvs = [], [], []
        for t, (w, win) in enumerate(self.pairs):
            shard, full = ctx.ref(("shard", w.name)), ctx.ref(("full", w.name))
            for h in range(2):
                local.append(ctx.local(2 * t + h, win.in_shard(shard, h), win.in_full(full, s_me, h)))
            for j, (ox, oy) in enumerate(_other_chips(x, y)):
                sends.append(ctx.remote(3 * t + j, win.in_shard(shard, c), win.in_full(full, s_me, c), (ox, oy, c)))
                landing = win.in_full(full, 2 * ox + oy, c)
                recvs.append(ctx.remote(3 * t + j, landing, landing, (x, y, c)))
        return local, sends, recvs

    def start(self, ctx):
        local, sends, _ = self._copies(ctx)
        for cp in local + sends:
            cp.start()

    def finish(self, ctx):
        local, sends, recvs = self._copies(ctx)
        for cp in recvs:
            cp.wait_recv()
        for cp in sends:
            cp.wait_send()
        for cp in local:
            cp.wait()

    def commit(self, res):
        for w, _ in self.pairs:
            w.full = res[("full", w.name)]


class _GatherD2d:
    def __init__(self, pairs):
        self.pairs = pairs
        self.nsem, self.nlocal = 3 * len(pairs), 0

    def register(self, side):
        for w, win in self.pairs:
            side.need(("full", w.name), arr=w.full, write=True)

    def _copies(self, ctx):
        x, y, c = _coords()
        sends, recvs = [], []
        for t, (w, win) in enumerate(self.pairs):
            full = ctx.ref(("full", w.name))
            for j, (ox, oy) in enumerate(_other_chips(x, y)):
                mine, theirs = win.in_full(full, 2 * ox + oy, c), win.in_full(full, 2 * ox + oy, 1 - c)
                sends.append(ctx.remote(3 * t + j, mine, mine, (x, y, 1 - c)))
                recvs.append(ctx.remote(3 * t + j, theirs, theirs, (x, y, c)))
        return sends, recvs

    def start(self, ctx):
        for cp in self._copies(ctx)[0]:
            cp.start()

    def finish(self, ctx):
        sends, recvs = self._copies(ctx)
        for cp in recvs:
            cp.wait_recv()
        for cp in sends:
            cp.wait_send()

    def commit(self, res):
        for w, _ in self.pairs:
            w.full = res[("full", w.name)]


class _ReduceSibling:
    def __init__(self, grads):
        self.grads = grads
        self.nsem, self.nlocal = N_CHIPS * len(grads), 0

    def register(self, side):
        for g in self.grads:
            side.need(("grad", g.name), arr=g.grad)
            side.need(("other", g.key), sds=_sds((N_CHIPS,) + g.win.piece_shape, F32), write=True)

    def _copies(self, ctx):
        x, y, c = _coords()
        out = []
        for t, g in enumerate(self.grads):
            grad, other = ctx.ref(("grad", g.name)), ctx.ref(("other", g.key))
            for s in range(N_CHIPS):
                out.append(ctx.remote(N_CHIPS * t + s, g.win.in_full(grad, s, 1 - c), other.at[s], (x, y, 1 - c)))
        return out

    def start(self, ctx):
        for cp in self._copies(ctx):
            cp.start()

    def finish(self, ctx):
        for cp in self._copies(ctx):
            cp.wait()

    def commit(self, res):
        for g in self.grads:
            g.other = res[("other", g.key)]


class _ReduceChips:
    def __init__(self, grads):
        self.grads = grads
        self.nsem, self.nlocal = 3 * len(grads), 0

    def register(self, side):
        for g in self.grads:
            side.need(("chip", g.key), arr=g.chip)
            side.need(("landed", g.key), sds=_sds(g.chip.shape, g.chip.dtype), write=True)

    def _copies(self, ctx):
        x, y, c = _coords()
        s_me = 2 * x + y
        out = []
        for t, g in enumerate(self.grads):
            chip, landed = ctx.ref(("chip", g.key)), ctx.ref(("landed", g.key))
            for j, (ox, oy) in enumerate(_other_chips(x, y)):
                out.append(ctx.remote(3 * t + j, chip.at[2 * ox + oy], landed.at[s_me], (ox, oy, c)))
        return out

    def start(self, ctx):
        for cp in self._copies(ctx):
            cp.start()

    def finish(self, ctx):
        for cp in self._copies(ctx):
            cp.wait()

    def commit(self, res):
        for g in self.grads:
            g.landed = res[("landed", g.key)]


class _ShareReduced:
    def __init__(self, items):
        self.items = items
        self.nsem, self.nlocal = len(items), 0

    def register(self, side):
        for sh, _ in self.items:
            side.need(("reduced", sh.name), arr=sh.arr, write=True)

    def _copies(self, ctx):
        x, y, c = _coords()
        sends, recvs = [], []
        for t, (sh, win) in enumerate(self.items):
            ref = ctx.ref(("reduced", sh.name))
            sends.append(ctx.remote(t, win.in_shard(ref, c), win.in_shard(ref, c), (x, y, 1 - c)))
            recvs.append(ctx.remote(t, win.in_shard(ref, 1 - c), win.in_shard(ref, 1 - c), (x, y, c)))
        return sends, recvs

    def start(self, ctx):
        for cp in self._copies(ctx)[0]:
            cp.start()

    def finish(self, ctx):
        sends, recvs = self._copies(ctx)
        for cp in recvs:
            cp.wait_recv()
        for cp in sends:
            cp.wait_send()

    def commit(self, res):
        for sh, _ in self.items:
            sh.arr = res[("reduced", sh.name)]


def _gcall(body, name, grid, in_specs, out_specs, out_shape, args, scratch=(), sem=None, side=None):
    single = not isinstance(out_shape, (list, tuple))
    out_shapes = [out_shape] if single else list(out_shape)
    out_specs = [out_specs] if single else list(out_specs)
    if side is None or not side.ops:
        res = pl.pallas_call(body, name=name, grid=grid, in_specs=list(in_specs), out_specs=out_specs, out_shape=out_shapes,
                             scratch_shapes=list(scratch), compiler_params=_cparams(sem))(*args)
        return res[0] if single else res
    n_in, n_out, n_scr = len(args), len(out_shapes), len(scratch)
    ns_in, ns_out = len(side.inputs), len(side.out_shape)

    def wrapped(*refs):
        a, si = refs[:n_in], refs[n_in:n_in + ns_in]
        o = refs[n_in + ns_in:n_in + ns_in + n_out]
        so = refs[n_in + ns_in + n_out:n_in + ns_in + n_out + ns_out]
        rest = refs[n_in + ns_in + n_out + ns_out:]
        scr, sems = rest[:n_scr], rest[n_scr:]
        ids = [pl.program_id(k) for k in range(len(grid))]
        first = functools.reduce(jnp.logical_and, [i == 0 for i in ids])
        last = functools.reduce(jnp.logical_and, [i == g - 1 for i, g in zip(ids, grid)])

        @pl.when(first)
        def _():
            side.start(si, so, sems)

        body(*a, *o, *scr)

        @pl.when(last)
        def _():
            side.finish(si, so, sems)

    res = pl.pallas_call(
        wrapped, name=name, grid=grid,
        in_specs=list(in_specs) + [ANY] * ns_in,
        out_specs=out_specs + [ANY] * ns_out,
        out_shape=out_shapes + side.out_shape,
        scratch_shapes=list(scratch) + side.scratch(),
        input_output_aliases={n_in + i: n_out + j for i, j in side.aliases.items()},
        compiler_params=_cparams(("arbitrary",) * len(grid)),
    )(*args, *side.inputs)
    side.commit(res[n_out:])
    return res[0] if single else res[:n_out]


def _standalone(side, name):
    ns_in, ns_out = len(side.inputs), len(side.out_shape)

    def body(*refs):
        si, so, sems = refs[:ns_in], refs[ns_in:ns_in + ns_out], refs[ns_in + ns_out:]
        side.start(si, so, sems)
        side.finish(si, so, sems)

    res = pl.pallas_call(body, name=name, in_specs=[ANY] * ns_in, out_specs=[ANY] * ns_out, out_shape=side.out_shape,
                         scratch_shapes=side.scratch(), input_output_aliases=dict(side.aliases))(*side.inputs)
    side.commit(res)


def _gather_first_weight(w):
    win = w.win

    def body(shard, full, send_sems, recv_sems, local_sems):
        x, y, c = _coords()
        s_me = 2 * x + y
        chips = _other_chips(x, y)

        def rcopy(k, src, dst, to):
            return pltpu.make_async_remote_copy(src_ref=src, dst_ref=dst, send_sem=send_sems.at[k], recv_sem=recv_sems.at[k],
                                                device_id=to, device_id_type=MESH)

        local = [pltpu.make_async_copy(win.in_shard(shard, h), win.in_full(full, s_me, h), local_sems.at[h]) for h in range(2)]
        for cp in local:
            cp.start()
        sent = []
        for j, (ox, oy) in enumerate(chips):
            cp = rcopy(j, win.in_shard(shard, c), win.in_full(full, s_me, c), (ox, oy, c))
            cp.start()
            sent.append(cp)
        for j, (ox, oy) in enumerate(chips):
            landing = win.in_full(full, 2 * ox + oy, c)
            rcopy(j, landing, landing, (x, y, c)).wait_recv()
            cp = rcopy(3 + j, landing, landing, (x, y, 1 - c))
            cp.start()
            sent.append(cp)
        for j, (ox, oy) in enumerate(chips):
            theirs = win.in_full(full, 2 * ox + oy, 1 - c)
            rcopy(3 + j, theirs, theirs, (x, y, c)).wait_recv()
        for cp in sent:
            cp.wait_send()
        for cp in local:
            cp.wait()

    w.full = pl.pallas_call(
        body, name="gather_" + w.name, in_specs=[ANY], out_specs=ANY, out_shape=_sds(win.full_shape, BF16),
        scratch_shapes=[pltpu.SemaphoreType.DMA((6,)), pltpu.SemaphoreType.DMA((6,)), pltpu.SemaphoreType.DMA((2,))],
    )(w.shard)


_DOT_DIMS = {
    "nn": (((1,), (0,)), ((), ())),
    "nt": (((1,), (1,)), ((), ())),
    "tn": (((0,), (0,)), ((), ())),
}


def _mm(a, b, mode, out_dtype, tm, tn, tk, name, j_outer=False, side=None):
    if mode == "nn":
        (m, k), n = a.shape, b.shape[1]
    elif mode == "nt":
        (m, k), n = a.shape, b.shape[0]
    else:
        (k, m), n = a.shape, b.shape[1]
    tm, tn, tk = _pick(m, tm), _pick(n, tn), _pick(k, tk)
    gi, gj, gk = m // tm, n // tn, k // tk
    dims = _DOT_DIMS[mode]

    def ij(g0, g1):
        return (g1, g0) if j_outer else (g0, g1)

    if mode == "tn":
        a_spec = pl.BlockSpec((tk, tm), lambda g0, g1, kk: (kk, ij(g0, g1)[0]))
    else:
        a_spec = pl.BlockSpec((tm, tk), lambda g0, g1, kk: (ij(g0, g1)[0], kk))
    if mode == "nt":
        b_spec = pl.BlockSpec((tn, tk), lambda g0, g1, kk: (ij(g0, g1)[1], kk))
    else:
        b_spec = pl.BlockSpec((tk, tn), lambda g0, g1, kk: (kk, ij(g0, g1)[1]))
    o_spec = pl.BlockSpec((tm, tn), lambda g0, g1, kk: ij(g0, g1))

    def body(a_ref, b_ref, o_ref, *scr):
        p = lax.dot_general(a_ref[...], b_ref[...], dims, preferred_element_type=F32)
        if gk == 1:
            o_ref[...] = p.astype(out_dtype)
        else:
            acc = scr[0]
            kk = pl.program_id(2)

            @pl.when(kk == 0)
            def _():
                acc[...] = p

            @pl.when(kk > 0)
            def _():
                acc[...] += p

            @pl.when(kk == gk - 1)
            def _():
                o_ref[...] = acc[...].astype(out_dtype)

    return _gcall(body, name, (gj, gi, gk) if j_outer else (gi, gj, gk), [a_spec, b_spec], o_spec, _sds((m, n), out_dtype), (a, b),
                  scratch=[pltpu.VMEM((tm, tn), F32)] if gk > 1 else [], sem=("parallel", "parallel", "arbitrary"), side=side)


def _stream_block(i, x_ref, meta_ref):
    d = x_ref.shape[-1]
    first = jnp.concatenate([jnp.zeros((LEAD, d), F32), meta_ref[...]], axis=0)
    return jnp.where(i == 0, first, x_ref[...])


def _norm(xb):
    mu = jnp.mean(xb, axis=-1, keepdims=True)
    xc = xb - mu
    var = jnp.mean(xc * xc, axis=-1, keepdims=True)
    rstd = lax.rsqrt(var + LN_EPS)
    return xc * rstd, rstd


def _ln_bwd_rows(dy, xhat, rstd, g):
    dyg = dy * g
    m1 = jnp.mean(dyg, axis=-1, keepdims=True)
    m2 = jnp.mean(dyg * xhat, axis=-1, keepdims=True)
    return rstd * (dyg - m1 - xhat * m2)


def _ln_in_fwd(x2d, meta, g, b, nb):
    seq, d = x2d.shape

    def body(x_ref, meta_ref, g_ref, b_ref, h_ref, hb_ref):
        xb = _stream_block(pl.program_id(0), x_ref, meta_ref)
        xhat, _ = _norm(xb)
        y = xhat * g_ref[...] + b_ref[...]
        h_ref[...] = y
        hb_ref[...] = y.astype(BF16)

    row = pl.BlockSpec((BLK, d), lambda i: (i, 0))
    vec = pl.BlockSpec((1, d), lambda i: (0, 0))
    return _gcall(body, "ln_in_fwd", (nb,),
                  [pl.BlockSpec((BLK, d), lambda i: (jnp.maximum(i - 1, 0), 0)), pl.BlockSpec((N_META, d), lambda i: (0, 0)), vec, vec],
                  [row, row], [_sds((nb * BLK, d), F32), _sds((nb * BLK, d), BF16)], (x2d, meta, g, b), sem=("parallel",))


def _res_ln_fwd(h, z, g, b, side=None):
    tp, d = h.shape

    def body(h_ref, z_ref, g_ref, b_ref, r_ref, y_ref, yb_ref):
        r = DN_ALPHA * h_ref[...] + z_ref[...]
        xhat, _ = _norm(r)
        y = xhat * g_ref[...] + b_ref[...]
        r_ref[...] = r
        y_ref[...] = y
        yb_ref[...] = y.astype(BF16)

    row = pl.BlockSpec((BLK, d), lambda i: (i, 0))
    vec = pl.BlockSpec((1, d), lambda i: (0, 0))
    return _gcall(body, "res_ln1_fwd", (tp // BLK,), [row, row, vec, vec], [row, row, row],
                  [_sds((tp, d), F32), _sds((tp, d), F32), _sds((tp, d), BF16)], (h, z, g, b), sem=("parallel",), side=side)


def _final_ln_loss(h1, z2, g, b, tgt):
    tp, d = h1.shape

    def body(h_ref, z_ref, g_ref, b_ref, t_ref, dr_ref, drb_ref, loss_ref, dg_ref, db_ref):
        i = pl.program_id(0)
        r = DN_ALPHA * h_ref[...] + z_ref[...]
        xhat, rstd = _norm(r)
        y = xhat * g_ref[...] + b_ref[...]
        err = jnp.where(i >= 1, y - t_ref[...], 0.0)
        dy = err * (1.0 / d)
        dr = _ln_bwd_rows(dy, xhat, rstd, g_ref[...])
        dr_ref[...] = dr
        drb_ref[...] = dr.astype(BF16)

        @pl.when(i == 0)
        def _():
            loss_ref[...] = jnp.zeros_like(loss_ref)
            dg_ref[...] = jnp.zeros_like(dg_ref)
            db_ref[...] = jnp.zeros_like(db_ref)

        loss_ref[...] += 0.5 * jnp.sum(jnp.sum(err * err, axis=-1, keepdims=True) * (1.0 / d), axis=0, keepdims=True)
        dg_ref[...] += jnp.sum(dy * xhat, axis=0, keepdims=True)
        db_ref[...] += jnp.sum(dy, axis=0, keepdims=True)

    row = pl.BlockSpec((BLK, d), lambda i: (i, 0))
    vec = pl.BlockSpec((1, d), lambda i: (0, 0))
    return _gcall(body, "final_ln_loss", (tp // BLK,),
                  [row, row, vec, vec, pl.BlockSpec((BLK, d), lambda i: (jnp.maximum(i - 1, 0), 0))],
                  [row, row, pl.BlockSpec((8, LANES), lambda i: (0, 0)), vec, vec],
                  [_sds((tp, d), F32), _sds((tp, d), BF16), _sds((8, LANES), F32), _sds((1, d), F32), _sds((1, d), F32)],
                  (h1, z2, g, b, tgt), sem=("arbitrary",))


def _ln1_bwd(d_res, d_mm, r, g, side=None):
    tp, d = r.shape

    def body(a_ref, m_ref, r_ref, g_ref, dr_ref, drb_ref, dg_ref, db_ref):
        dy = DN_ALPHA * a_ref[...] + m_ref[...]
        xhat, rstd = _norm(r_ref[...])
        dr = _ln_bwd_rows(dy, xhat, rstd, g_ref[...])
        dr_ref[...] = dr
        drb_ref[...] = dr.astype(BF16)

        @pl.when(pl.program_id(0) == 0)
        def _():
            dg_ref[...] = jnp.zeros_like(dg_ref)
            db_ref[...] = jnp.zeros_like(db_ref)

        dg_ref[...] += jnp.sum(dy * xhat, axis=0, keepdims=True)
        db_ref[...] += jnp.sum(dy, axis=0, keepdims=True)

    row = pl.BlockSpec((BLK, d), lambda i: (i, 0))
    vec = pl.BlockSpec((1, d), lambda i: (0, 0))
    return _gcall(body, "ln1_bwd", (tp // BLK,), [row, row, row, vec], [row, row, vec, vec],
                  [_sds((tp, d), F32), _sds((tp, d), BF16), _sds((1, d), F32), _sds((1, d), F32)], (d_res, d_mm, r, g),
                  sem=("arbitrary",), side=side)


def _ln_in_bwd(d_res, d_mm, x2d, meta, g, side=None):
    seq, d = x2d.shape
    nb = d_res.shape[0] // BLK

    def body(a_ref, m_ref, x_ref, meta_ref, g_ref, gx_ref, gm_ref, dg_ref, db_ref):
        i = pl.program_id(0)
        dy = DN_ALPHA * a_ref[...] + m_ref[...]
        xhat, rstd = _norm(_stream_block(i, x_ref, meta_ref))
        dx = _ln_bwd_rows(dy, xhat, rstd, g_ref[...])
        gx_ref[...] = dx

        @pl.when(i == 0)
        def _():
            gm_ref[...] = dx[LEAD:, :]
            dg_ref[...] = jnp.zeros_like(dg_ref)
            db_ref[...] = jnp.zeros_like(db_ref)

        dg_ref[...] += jnp.sum(dy * xhat, axis=0, keepdims=True)
        db_ref[...] += jnp.sum(dy, axis=0, keepdims=True)

    row = pl.BlockSpec((BLK, d), lambda i: (i, 0))
    xrow = pl.BlockSpec((BLK, d), lambda i: (jnp.maximum(i - 1, 0), 0))
    vec = pl.BlockSpec((1, d), lambda i: (0, 0))
    met = pl.BlockSpec((N_META, d), lambda i: (0, 0))
    return _gcall(body, "ln_in_bwd", (nb,), [row, row, xrow, met, vec], [xrow, met, vec, vec],
                  [_sds((seq, d), F32), _sds((N_META, d), F32), _sds((1, d), F32), _sds((1, d), F32)], (d_res, d_mm, x2d, meta, g),
                  sem=("arbitrary",), side=side)


def _rope_tables(tp):
    half = ROPE_DIM // 2
    inv_freq = ROPE_THETA ** (-jnp.arange(half, dtype=F32) * 2.0 / ROPE_DIM)
    pos = (jnp.arange(tp) - LEAD).astype(F32)
    ang = pos[:, None] * inv_freq[None, :]
    cos, sin = jnp.cos(ang), jnp.sin(ang)
    ones = jnp.ones((tp, HEAD_DIM - ROPE_DIM), F32)
    cos_h = jnp.concatenate([cos, cos, ones], axis=1)
    sin_h = jnp.concatenate([-sin, sin, 0.0 * ones], axis=1)
    reps = LANES // HEAD_DIM
    return jnp.tile(cos_h, (1, reps)), jnp.tile(sin_h, (1, reps))


def _rope_partner(x):
    half = ROPE_DIM // 2
    lane = lax.broadcasted_iota(jnp.int32, x.shape, 1) % HEAD_DIM
    upper = jnp.where(lane < ROPE_DIM, pltpu.roll(x, half, 1), 0.0)
    return jnp.where(lane < half, pltpu.roll(x, LANES - half, 1), upper)


def _rope_fwd(proj, cos, sin, n_rot, width):
    tp = proj.shape[0]

    def body(p_ref, c_ref, s_ref, o_ref):
        c, s = c_ref[...], s_ref[...]
        for j in range(width // LANES):
            sl = slice(j * LANES, (j + 1) * LANES)
            xj = p_ref[:, sl]
            if j < n_rot:
                xj = xj * c + _rope_partner(xj) * s
            o_ref[:, sl] = xj.astype(BF16)

    tab = pl.BlockSpec((BLK, LANES), lambda i: (i, 0))
    blk = pl.BlockSpec((BLK, width), lambda i: (i, 0))
    return _gcall(body, "rope_fwd", (tp // BLK,), [blk, tab, tab], blk, _sds((tp, width), BF16), (proj, cos, sin), sem=("parallel",))


def _rope_bwd(dq, dk_cur, dk_prev, dk_meta, dv_cur, dv_prev, dv_meta, cos, sin, side=None):
    tp, aw = dq.shape
    kw = dk_cur.shape[1]
    nb = tp // BLK

    def body(dq_ref, kc_ref, kp_ref, km_ref, vc_ref, vp_ref, vm_ref, c_ref, s_ref, o_ref):
        i = pl.program_id(0)
        c, s = c_ref[...], s_ref[...]
        has_next = i + 1 < nb

        def unrot(g):
            return g * c + _rope_partner(g * s)

        def kv_sum(cur, prv, met):
            return cur[...] + jnp.where(has_next, prv[...], 0.0) + jnp.where(i == 0, met[...], 0.0)

        for j in range(aw // LANES):
            sl = slice(j * LANES, (j + 1) * LANES)
            o_ref[:, sl] = unrot(dq_ref[:, sl]).astype(BF16)
        dk = kv_sum(kc_ref, kp_ref, km_ref)
        dv = kv_sum(vc_ref, vp_ref, vm_ref)
        for j in range(kw // LANES):
            sl = slice(j * LANES, (j + 1) * LANES)
            o_ref[:, aw + j * LANES:aw + (j + 1) * LANES] = unrot(dk[:, sl]).astype(BF16)
            o_ref[:, aw + kw + j * LANES:aw + kw + (j + 1) * LANES] = dv[:, sl].astype(BF16)

    cur = pl.BlockSpec((BLK, kw), lambda i: (i, 0))
    nxt = pl.BlockSpec((BLK, kw), lambda i: (jnp.minimum(i + 1, nb - 1), 0))
    met = pl.BlockSpec((BLK, kw), lambda i: (0, 0))
    tab = pl.BlockSpec((BLK, LANES), lambda i: (i, 0))
    return _gcall(body, "rope_bwd", (nb,), [pl.BlockSpec((BLK, aw), lambda i: (i, 0)), cur, nxt, met, cur, nxt, met, tab, tab],
                  pl.BlockSpec((BLK, aw + 2 * kw), lambda i: (i, 0)), _sds((tp, aw + 2 * kw), BF16),
                  (dq, dk_cur, dk_prev, dk_meta, dv_cur, dv_prev, dv_meta, cos, sin), sem=("parallel",), side=side)


def _attn_probs(n, q_ref, km_ref, kp_ref, kc_ref, sink_ref, grp):
    scale = HEAD_DIM ** -0.5
    qs = q_ref[...].reshape(grp * BLK, HEAD_DIM)
    kcat = jnp.concatenate([km_ref[...], kp_ref[...], kc_ref[...]], axis=0)
    s = lax.dot_general(qs, kcat, _DOT_DIMS["nt"], preferred_element_type=F32) * scale
    s = s.reshape(grp, BLK, 3 * BLK)
    r = lax.broadcasted_iota(jnp.int32, (1, BLK, 3 * BLK), 1)
    j = lax.broadcasted_iota(jnp.int32, (1, BLK, 3 * BLK), 2)
    q_idx = n * BLK + r
    meta_ok = (j >= LEAD) & (j < BLK) & (q_idx >= j)
    k_idx = (n - 1) * BLK + (j - BLK)
    diff = q_idx - k_idx
    band_ok = (j >= BLK) & (diff >= 0) & (diff < BLK) & (k_idx >= LEAD + N_META)
    s = jnp.where(meta_ok | band_ok, s, NEG_INF)
    sink = sink_ref[...]
    m = jnp.maximum(jnp.max(s, axis=-1, keepdims=True), sink)
    p = jnp.exp(s - m)
    e_sink = jnp.exp(sink - m)
    inv = 1.0 / (jnp.sum(p, axis=-1, keepdims=True) + e_sink)
    return qs, kcat, p * inv, e_sink * inv


def _attn_specs(grp):
    qspec = pl.BlockSpec((grp, BLK, HEAD_DIM), lambda kk, n: (kk, n, 0))
    kmeta = pl.BlockSpec((None, BLK, HEAD_DIM), lambda kk, n: (kk, 0, 0))
    kprev = pl.BlockSpec((None, BLK, HEAD_DIM), lambda kk, n: (kk, jnp.maximum(n - 1, 0), 0))
    kcur = pl.BlockSpec((None, BLK, HEAD_DIM), lambda kk, n: (kk, n, 0))
    sink = pl.BlockSpec((None, grp, BLK, 1), lambda kk, n: (kk, 0, 0, 0))
    return qspec, kmeta, kprev, kcur, sink


def _attn_fwd(q_hm, k_hm, v_hm, sink4, side=None):
    nq, tp, _ = q_hm.shape
    nkv = k_hm.shape[0]
    grp = nq // nkv

    def body(q_ref, km_ref, kp_ref, kc_ref, vm_ref, vp_ref, vc_ref, sink_ref, o_ref):
        n = pl.program_id(1)
        _, _, pn, _ = _attn_probs(n, q_ref, km_ref, kp_ref, kc_ref, sink_ref, grp)
        vcat = jnp.concatenate([vm_ref[...], vp_ref[...], vc_ref[...]], axis=0)
        o = jnp.dot(pn.reshape(grp * BLK, 3 * BLK).astype(BF16), vcat, preferred_element_type=F32)
        o_ref[...] = o.reshape(grp, BLK, HEAD_DIM).astype(BF16)

    qspec, kmeta, kprev, kcur, sink = _attn_specs(grp)
    return _gcall(body, "attn_fwd", (nkv, tp // BLK), [qspec, kmeta, kprev, kcur, kmeta, kprev, kcur, sink], qspec,
                  _sds((nq, tp, HEAD_DIM), BF16), (q_hm, k_hm, k_hm, k_hm, v_hm, v_hm, v_hm, sink4), sem=("parallel", "parallel"), side=side)


def _attn_bwd(q_hm, k_hm, v_hm, sink4, do_hm, side=None):
    nq, tp, _ = q_hm.shape
    nkv = k_hm.shape[0]
    grp = nq // nkv
    scale = HEAD_DIM ** -0.5

    def body(q_ref, km_ref, kp_ref, kc_ref, vm_ref, vp_ref, vc_ref, sink_ref, do_ref,
             dq_ref, dkc_ref, dkp_ref, dkm_ref, dvc_ref, dvp_ref, dvm_ref, dsk_ref):
        n = pl.program_id(1)
        qs, kcat, pn, p_sink = _attn_probs(n, q_ref, km_ref, kp_ref, kc_ref, sink_ref, grp)
        vcat = jnp.concatenate([vm_ref[...], vp_ref[...], vc_ref[...]], axis=0)
        pn2 = pn.reshape(grp * BLK, 3 * BLK)
        pnb = pn2.astype(BF16)
        dob = do_ref[...].reshape(grp * BLK, HEAD_DIM).astype(BF16)
        dp = lax.dot_general(dob, vcat, _DOT_DIMS["nt"], preferred_element_type=F32)
        delta = jnp.sum(pn2 * dp, axis=-1, keepdims=True)
        ds = (pn2 * (dp - delta) * scale).astype(BF16)
        dq_ref[...] = jnp.dot(ds, kcat, preferred_element_type=F32).reshape(grp, BLK, HEAD_DIM)
        dk = lax.dot_general(ds, qs, _DOT_DIMS["tn"], preferred_element_type=F32)
        dv = lax.dot_general(pnb, dob, _DOT_DIMS["tn"], preferred_element_type=F32)
        dkp_ref[...] = dk[BLK:2 * BLK]
        dkc_ref[...] = dk[2 * BLK:]
        dvp_ref[...] = dv[BLK:2 * BLK]
        dvc_ref[...] = dv[2 * BLK:]
        dsk = -jnp.sum(p_sink * delta.reshape(grp, BLK, 1), axis=1, keepdims=True)

        @pl.when(n == 0)
        def _():
            dkm_ref[...] = jnp.zeros_like(dkm_ref)
            dvm_ref[...] = jnp.zeros_like(dvm_ref)
            dsk_ref[...] = jnp.zeros_like(dsk_ref)

        dkm_ref[...] += dk[:BLK]
        dvm_ref[...] += dv[:BLK]
        dsk_ref[...] += jnp.broadcast_to(dsk, (grp, BLK, 1))

    qspec, kmeta, kprev, kcur, sink = _attn_specs(grp)
    kv_shape = _sds((nkv, tp, HEAD_DIM), F32)
    meta_shape = _sds((nkv, BLK, HEAD_DIM), F32)
    return _gcall(body, "attn_bwd", (nkv, tp // BLK), [qspec, kmeta, kprev, kcur, kmeta, kprev, kcur, sink, qspec],
                  [qspec, kcur, kcur, kmeta, kcur, kcur, kmeta, sink],
                  [_sds((nq, tp, HEAD_DIM), F32), kv_shape, kv_shape, meta_shape, kv_shape, kv_shape, meta_shape,
                   _sds((nkv, grp, BLK, 1), F32)],
                  (q_hm, k_hm, k_hm, k_hm, v_hm, v_hm, v_hm, sink4, do_hm), sem=("parallel", "arbitrary"), side=side)


def _pool_coef(row_blk, col_blk, w):
    r = lax.broadcasted_iota(jnp.int32, (BLK, BLK), 0)
    j = lax.broadcasted_iota(jnp.int32, (BLK, BLK), 1)
    t = row_blk * BLK + r - LEAD
    tj = col_blk * BLK + j - LEAD
    dist = t - tj
    inwin = (dist >= 0) & (dist < w) & (tj >= 0)
    count = jnp.maximum(jnp.minimum(t + 1, w), 1).astype(F32)
    return jnp.where(inwin, 1.0 / count, 0.0) - jnp.where((dist == 0) & (tj >= 0), 1.0, 0.0)


def _pool_fwd(proj, wg, scale, u_off, pool_w, side=None):
    tp = proj.shape[0]
    gw = pool_w // N_GRP
    nb = tp // BLK
    cb = u_off // gw

    def body(up_ref, uc_ref, wg_ref, sc_ref, pooled_ref, mx_ref, pm_ref):
        n, g = pl.program_id(0), pl.program_id(1)
        w = jnp.left_shift(2, g)
        pooled = (jnp.dot(_pool_coef(n, n - 1, w), up_ref[...], precision=HI, preferred_element_type=F32)
                  + jnp.dot(_pool_coef(n, n, w), uc_ref[...], precision=HI, preferred_element_type=F32))
        pb = pooled.astype(BF16)
        mx = jnp.dot(pb, wg_ref[...], preferred_element_type=F32)
        pooled_ref[...] = pb
        mx_ref[...] = mx
        pm_ref[...] = (mx * sc_ref[...]).astype(BF16)

    blk = pl.BlockSpec((BLK, gw), lambda n, g: (n, g))
    return _gcall(body, "pool_fwd", (nb, N_GRP),
                  [pl.BlockSpec((BLK, gw), lambda n, g: (jnp.maximum(n - 1, 0), cb + g)),
                   pl.BlockSpec((BLK, gw), lambda n, g: (n, cb + g)),
                   pl.BlockSpec((None, gw, gw), lambda n, g: (g, 0, 0)),
                   pl.BlockSpec((1, gw), lambda n, g: (0, g))],
                  [blk, blk, blk], [_sds((tp, pool_w), BF16), _sds((tp, pool_w), F32), _sds((tp, pool_w), BF16)],
                  (proj, proj, wg, scale), sem=("parallel", "parallel"), side=side)


def _pool_bwd_mix(d_pm, mx, pooled, wg, scale, side=None):
    tp, pool_w = d_pm.shape
    gw = pool_w // N_GRP

    def body(d_ref, mx_ref, pl_ref, wg_ref, sc_ref, dp_ref, dwg_ref, dsc_ref):
        n = pl.program_id(1)
        d = d_ref[...]
        dmx = (d * sc_ref[...]).astype(BF16)
        dp_ref[...] = lax.dot_general(dmx, wg_ref[...], _DOT_DIMS["nt"], preferred_element_type=F32)

        @pl.when(n == 0)
        def _():
            dwg_ref[...] = jnp.zeros_like(dwg_ref)
            dsc_ref[...] = jnp.zeros_like(dsc_ref)

        dwg_ref[...] += lax.dot_general(pl_ref[...], dmx, _DOT_DIMS["tn"], preferred_element_type=F32)
        dsc_ref[...] += jnp.sum(d * mx_ref[...], axis=0, keepdims=True)

    blk = pl.BlockSpec((BLK, gw), lambda g, n: (n, g))
    wspec = pl.BlockSpec((None, gw, gw), lambda g, n: (g, 0, 0))
    sspec = pl.BlockSpec((1, gw), lambda g, n: (0, g))
    return _gcall(body, "pool_bwd_mix", (N_GRP, tp // BLK), [blk, blk, blk, wspec, sspec], [blk, wspec, sspec],
                  [_sds((tp, pool_w), F32), _sds((N_GRP, gw, gw), F32), _sds((1, pool_w), F32)], (d_pm, mx, pooled, wg, scale),
                  sem=("parallel", "arbitrary"), side=side)


def _pool_bwd_band(dp, side=None):
    tp, pool_w = dp.shape
    gw = pool_w // N_GRP
    nb = tp // BLK

    def body(dc_ref, dn_ref, du_ref):
        n, g = pl.program_id(0), pl.program_id(1)
        w = jnp.left_shift(2, g)
        dnext = jnp.where(n + 1 < nb, dn_ref[...], 0.0)
        du = (lax.dot_general(_pool_coef(n, n, w), dc_ref[...], _DOT_DIMS["tn"], precision=HI, preferred_element_type=F32)
              + lax.dot_general(_pool_coef(n + 1, n, w), dnext, _DOT_DIMS["tn"], precision=HI, preferred_element_type=F32))
        du_ref[...] = du.astype(BF16)

    blk = pl.BlockSpec((BLK, gw), lambda n, g: (n, g))
    return _gcall(body, "pool_bwd_band", (nb, N_GRP), [blk, pl.BlockSpec((BLK, gw), lambda n, g: (jnp.minimum(n + 1, nb - 1), g))],
                  blk, _sds((tp, pool_w), BF16), (dp, dp), sem=("parallel", "parallel"), side=side)


def _gate_tiles(tp, d, g_off):
    tc = _pick(math.gcd(g_off, d), 512, 256, 128)
    tr = _pick(tp, 384, 128)
    return tr, tc


def _mix_fwd(proj, b_gate, a_out, p_out, g_off, side=None):
    tp, d = a_out.shape
    tr, tc = _gate_tiles(tp, d, g_off)
    c0, c1 = g_off // tc, (g_off + d) // tc

    def body(g0_ref, g1_ref, b_ref, a_ref, p_ref, o_ref):
        g0 = jax.nn.sigmoid(g0_ref[...] + b_ref[0:1, :])
        g1 = jax.nn.sigmoid(g1_ref[...] + b_ref[1:2, :])
        o_ref[...] = (g0 * a_ref[...] + g1 * p_ref[...]).astype(BF16)

    blk = pl.BlockSpec((tr, tc), lambda i, j: (i, j))
    return _gcall(body, "mix_fwd", (tp // tr, d // tc),
                  [pl.BlockSpec((tr, tc), lambda i, j: (i, c0 + j)), pl.BlockSpec((tr, tc), lambda i, j: (i, c1 + j)),
                   pl.BlockSpec((2, tc), lambda i, j: (0, j)), blk, blk],
                  blk, _sds((tp, d), BF16), (proj, proj, b_gate, a_out, p_out), sem=("parallel", "parallel"), side=side)


def _mix_bwd(proj, b_gate, a_out, p_out, d_mixed, g_off, side=None):
    tp, d = a_out.shape
    tr, tc = _gate_tiles(tp, d, g_off)
    c0, c1 = g_off // tc, (g_off + d) // tc

    def body(g0_ref, g1_ref, b_ref, a_ref, p_ref, d_ref, da_ref, dp_ref, dl0_ref, dl1_ref, db_ref):
        g0 = jax.nn.sigmoid(g0_ref[...] + b_ref[0:1, :])
        g1 = jax.nn.sigmoid(g1_ref[...] + b_ref[1:2, :])
        dm = d_ref[...]
        da_ref[...] = (dm * g0).astype(BF16)
        dp_ref[...] = (dm * g1).astype(BF16)
        dl0 = dm * a_ref[...] * g0 * (1.0 - g0)
        dl1 = dm * p_ref[...] * g1 * (1.0 - g1)
        dl0_ref[...] = dl0.astype(BF16)
        dl1_ref[...] = dl1.astype(BF16)

        @pl.when(pl.program_id(1) == 0)
        def _():
            db_ref[...] = jnp.zeros_like(db_ref)

        db_ref[...] += jnp.concatenate([jnp.sum(dl0, axis=0, keepdims=True), jnp.sum(dl1, axis=0, keepdims=True)], axis=0)

    blk = pl.BlockSpec((tr, tc), lambda j, i: (i, j))
    big = _sds((tp, d), BF16)
    return _gcall(body, "mix_bwd", (d // tc, tp // tr),
                  [pl.BlockSpec((tr, tc), lambda j, i: (i, c0 + j)), pl.BlockSpec((tr, tc), lambda j, i: (i, c1 + j)),
                   pl.BlockSpec((2, tc), lambda j, i: (0, j)), blk, blk, blk],
                  [blk, blk, blk, blk, pl.BlockSpec((2, tc), lambda j, i: (0, j))], [big, big, big, big, _sds((2, d), F32)],
                  (proj, proj, b_gate, a_out, p_out, d_mixed), sem=("parallel", "arbitrary"), side=side)


SWIGLU_ROWS = 64


def _swiglu_fwd(ff, side=None):
    tp, f2 = ff.shape
    f = f2 // 2
    tr = _pick(tp, SWIGLU_ROWS)

    def body(x_ref, o_ref):
        gate, up = x_ref[:, :f], x_ref[:, f:]
        o_ref[...] = (gate * jax.nn.sigmoid(gate) * up).astype(BF16)

    return _gcall(body, "swiglu_fwd", (tp // tr,), [pl.BlockSpec((tr, f2), lambda i: (i, 0))], pl.BlockSpec((tr, f), lambda i: (i, 0)),
                  _sds((tp, f), BF16), (ff,), sem=("parallel",), side=side)


def _swiglu_bwd(ff, d_act, side=None):
    tp, f2 = ff.shape
    f = f2 // 2
    tr = _pick(tp, SWIGLU_ROWS)

    def body(x_ref, d_ref, o_ref):
        gate, up = x_ref[:, :f], x_ref[:, f:]
        d = d_ref[...]
        sg = jax.nn.sigmoid(gate)
        silu = gate * sg
        o_ref[:, :f] = (d * up * (sg + silu * (1.0 - sg))).astype(BF16)
        o_ref[:, f:] = (d * silu).astype(BF16)

    return _gcall(body, "swiglu_bwd", (tp // tr,), [pl.BlockSpec((tr, f2), lambda i: (i, 0)), pl.BlockSpec((tr, f), lambda i: (i, 0))],
                  pl.BlockSpec((tr, f2), lambda i: (i, 0)), _sds((tp, f2), BF16), (ff, d_act), sem=("parallel",), side=side)


def _tile2(rows, cols, max_bytes=3 << 20):
    tc = _pick(cols, 1024, 640, 512)
    for tr in (512, 344, 256, 128, 64, 32, 16, 8):
        if rows % tr == 0 and tr * tc * 4 <= max_bytes:
            return tr, tc
    return rows, tc


def _cast_bf16(w, name):
    r, c = w.shape
    tr, tc = _tile2(r, c)

    def body(x_ref, o_ref):
        o_ref[...] = x_ref[...].astype(BF16)

    blk = pl.BlockSpec((tr, tc), lambda i, j: (i, j))
    return _gcall(body, name, (r // tr, c // tc), [blk], blk, _sds((r, c), BF16), (w,), sem=("parallel", "parallel"))


def _adamw(w, g, m, v, name, side=None):
    r, c = w.shape
    tr, tc = _tile2(r, c, 1 << 20)

    def body(w_ref, g_ref, m_ref, v_ref, d_ref, nm_ref, nv_ref):
        gg = g_ref[...]
        nm = ADAM_B1 * m_ref[...] + (1.0 - ADAM_B1) * gg
        nv = ADAM_B2 * v_ref[...] + (1.0 - ADAM_B2) * jnp.square(gg)
        m_hat = nm / (1.0 - ADAM_B1 ** ADAM_STEP)
        v_hat = nv / (1.0 - ADAM_B2 ** ADAM_STEP)
        d_ref[...] = -ADAM_LR * (m_hat / (jnp.sqrt(v_hat) + ADAM_EPS) + ADAM_WD * w_ref[...])
        nm_ref[...] = nm
        nv_ref[...] = nv

    blk = pl.BlockSpec((tr, tc), lambda i, j: (i, j))
    shp = _sds((r, c), F32)
    return _gcall(body, name, (r // tr, c // tc), [blk] * 4, [blk] * 3, [shp] * 3, (w, g, m, v), sem=("parallel", "parallel"), side=side)


def _piece_block_index(win, tr, tc):
    r, c = win.shard_shape
    if win.kind == "col":
        return lambda s, h, i, j: (win.row0 // tr + h * (win.half // tr) + i, s * (c // tc) + j)
    return lambda s, h, i, j: ((s * r + win.row0) // tr + h * (win.half // tr) + i, j)


def _chip_sum(g, who):
    pr, pc = g.win.piece_shape
    tr, tc = _tile2(pr, pc)
    full_idx = _piece_block_index(g.win, tr, tc)

    def body(who_ref, g_ref, o_ref, out_ref):
        out_ref[...] = (g_ref[...] + o_ref[...]).astype(BF16)

    slot = pl.BlockSpec((None, tr, tc), lambda s, i, j, who_ref: (s, i, j))
    g.chip = pl.pallas_call(
        body,
        name="chip_sum_" + g.key.replace("@", "_"),
        grid_spec=pltpu.PrefetchScalarGridSpec(
            num_scalar_prefetch=1,
            grid=(N_CHIPS, pr // tr, pc // tc),
            in_specs=[pl.BlockSpec((tr, tc), lambda s, i, j, who_ref: full_idx(s, who_ref[0], i, j)), slot],
            out_specs=slot,
        ),
        out_shape=_sds((N_CHIPS, pr, pc), BF16),
        compiler_params=_cparams(("parallel", "parallel", "parallel")),
    )(who, g.grad, g.other)


def _final_sum(g, shard, who):
    win = g.win
    pr, pc = win.piece_shape
    tr, tc = _tile2(pr, pc)
    full_idx = _piece_block_index(win, tr, tc)
    has_prev = shard.arr is not None

    def body(who_ref, g_ref, o_ref, l1_ref, l2_ref, l3_ref, *rest):
        out_ref = rest[-1]
        acc = g_ref[...] + o_ref[...]
        for l_ref in (l1_ref, l2_ref, l3_ref):
            acc = acc + l_ref[...].astype(F32)
        out_ref[...] = acc

    def landed_spec(k):
        return pl.BlockSpec((None, tr, tc), lambda i, j, who_ref: (who_ref[1 + k], i, j))

    in_specs = [pl.BlockSpec((tr, tc), lambda i, j, who_ref: full_idx(who_ref[1], who_ref[0], i, j)),
                pl.BlockSpec((None, tr, tc), lambda i, j, who_ref: (who_ref[1], i, j)),
                landed_spec(1), landed_spec(2), landed_spec(3)]
    args = [who, g.grad, g.other, g.landed, g.landed, g.landed]
    if has_prev:
        in_specs.append(ANY)
        args.append(shard.arr)
    shard.arr = pl.pallas_call(
        body,
        name="final_sum_" + g.key.replace("@", "_"),
        grid_spec=pltpu.PrefetchScalarGridSpec(
            num_scalar_prefetch=1,
            grid=(pr // tr, pc // tc),
            in_specs=in_specs,
            out_specs=pl.BlockSpec((tr, tc), lambda i, j, who_ref: (win.row0 // tr + who_ref[0] * (pr // tr) + i, j)),
        ),
        out_shape=_sds(win.shard_shape, F32),
        input_output_aliases={6: 0} if has_prev else {},
        compiler_params=_cparams(("parallel", "parallel")),
    )(*args)


def _gather_small(packed):
    r, c = packed.shape

    def body(in_ref, out_ref, send_sems, recv_sems):
        x, y, c_ = _coords()
        s_me = 2 * x + y
        out_ref[s_me] = in_ref[...]
        copies = []
        for j, (ox, oy) in enumerate(_other_chips(x, y)):
            cp = pltpu.make_async_remote_copy(src_ref=in_ref, dst_ref=out_ref.at[s_me], send_sem=send_sems.at[j],
                                              recv_sem=recv_sems.at[j], device_id=(ox, oy, c_), device_id_type=MESH)
            cp.start()
            copies.append(cp)
        for j, (ox, oy) in enumerate(_other_chips(x, y)):
            copies[j].wait_send()
            pltpu.make_async_remote_copy(src_ref=in_ref, dst_ref=out_ref.at[2 * ox + oy], send_sem=send_sems.at[j],
                                         recv_sem=recv_sems.at[j], device_id=(x, y, c_), device_id_type=MESH).wait_recv()

    return pl.pallas_call(
        body,
        name="gather_small",
        in_specs=[VMEM_FULL],
        out_specs=VMEM_FULL,
        out_shape=_sds((N_CHIPS, r, c), F32),
        scratch_shapes=[pltpu.SemaphoreType.DMA((3,)), pltpu.SemaphoreType.DMA((3,))],
    )(packed)


def _all_reduce_small(packed):
    r, c = packed.shape

    def body(in_ref, out_ref, slots, send_sems, recv_sems):
        x, y, c_ = _coords()
        me = 4 * x + 2 * y + c_
        slots[me] = in_ref[...]
        copies = []
        for k in range(1, N_DEV):
            peer = me ^ k
            cp = pltpu.make_async_remote_copy(src_ref=in_ref, dst_ref=slots.at[me], send_sem=send_sems.at[k - 1],
                                              recv_sem=recv_sems.at[k - 1],
                                              device_id=(peer // 4, (peer // 2) % 2, peer % 2), device_id_type=MESH)
            cp.start()
            copies.append(cp)
        for k in range(1, N_DEV):
            copies[k - 1].wait_send()
            pltpu.make_async_remote_copy(src_ref=in_ref, dst_ref=slots.at[me ^ k], send_sem=send_sems.at[k - 1],
                                         recv_sem=recv_sems.at[k - 1], device_id=(x, y, c_), device_id_type=MESH).wait_recv()
        acc = slots[0]
        for d in range(1, N_DEV):
            acc = acc + slots[d]
        out_ref[...] = acc

    return pl.pallas_call(
        body,
        name="all_reduce_small",
        in_specs=[VMEM_FULL],
        out_specs=VMEM_FULL,
        out_shape=_sds((r, c), F32),
        scratch_shapes=[pltpu.VMEM((N_DEV, r, c), F32), pltpu.SemaphoreType.DMA((N_DEV - 1,)), pltpu.SemaphoreType.DMA((N_DEV - 1,))],
    )(packed)


def _rows_of(a, width):
    flat = a.reshape(-1)
    n = -(-flat.shape[0] // width) * width
    return jnp.pad(flat, (0, n - flat.shape[0])).reshape(-1, width)


def _pad_rows(a, mult=8):
    n = -(-a.shape[0] // mult) * mult
    return jnp.pad(a, ((0, n - a.shape[0]), (0, 0)))


def _heads_major(a, nh):
    tp = a.shape[0]
    return a.reshape(tp, nh, HEAD_DIM).transpose(1, 0, 2)


def _heads_minor(a):
    nh, tp, hd = a.shape
    return a.transpose(1, 0, 2).reshape(tp, nh * hd)


def kernel(x, meta_tokens, ln_in_g, ln_in_b, w_in, b_gate, attn_sinks, w_attn_up, w_pool_grp, pool_scale, w_pool_up, w_out, ln1_g, ln1_b, w_ffn_in, w_ffn_down, ln2_g, ln2_b, loss_target, m_meta_tokens, m_ln_in_g, m_ln_in_b, m_w_in, m_b_gate, m_attn_sinks, m_w_attn_up, m_w_pool_grp, m_pool_scale, m_w_pool_up, m_w_out, m_ln1_g, m_ln1_b, m_w_ffn_in, m_w_ffn_down, m_ln2_g, m_ln2_b, v_meta_tokens, v_ln_in_g, v_ln_in_b, v_w_in, v_b_gate, v_attn_sinks, v_w_attn_up, v_w_pool_grp, v_pool_scale, v_w_pool_up, v_w_out, v_ln1_g, v_ln1_b, v_w_ffn_in, v_w_ffn_down, v_ln2_g, v_ln2_b):
    seq, d = x.shape[1], x.shape[2]
    tp = LEAD + N_META + seq
    nb = tp // BLK
    nq = attn_sinks.shape[1]
    grp = nq // N_KV
    attn_w = nq * HEAD_DIM
    kv_w = N_KV * HEAD_DIM
    qkv_w = attn_w + 2 * kv_w
    pool_w = pool_scale.shape[1]
    gw = pool_w // N_GRP
    g_off = qkv_w + pool_w
    dc = d // N_CHIPS
    cx, cy, cc = _coords()
    s_me = 2 * cx + cy
    who = jnp.stack([cc, s_me, (s_me + 1) % N_CHIPS, (s_me + 2) % N_CHIPS, (s_me + 3) % N_CHIPS]).astype(jnp.int32)

    names = ["w_in", "w_attn_up", "w_pool_grp", "w_pool_up", "w_out", "w_ffn_in", "w_ffn_down"]
    kinds = dict(w_in="col", w_attn_up="col", w_pool_grp="row", w_pool_up="col", w_out="row", w_ffn_in="col", w_ffn_down="row")
    grp_shard = (N_GRP * (gw // N_CHIPS), gw)
    big_w = dict(w_in=w_in[0], w_attn_up=w_attn_up[0], w_pool_grp=w_pool_grp[0].reshape(grp_shard), w_pool_up=w_pool_up[0],
                 w_out=w_out[0], w_ffn_in=w_ffn_in[0], w_ffn_down=w_ffn_down[0])
    big_m = dict(w_in=m_w_in[0], w_attn_up=m_w_attn_up[0], w_pool_grp=m_w_pool_grp[0].reshape(grp_shard), w_pool_up=m_w_pool_up[0],
                 w_out=m_w_out[0], w_ffn_in=m_w_ffn_in[0], w_ffn_down=m_w_ffn_down[0])
    big_v = dict(w_in=v_w_in[0], w_attn_up=v_w_attn_up[0], w_pool_grp=v_w_pool_grp[0].reshape(grp_shard), w_pool_up=v_w_pool_up[0],
                 w_out=v_w_out[0], w_ffn_in=v_w_ffn_in[0], w_ffn_down=v_w_ffn_down[0])
    W = {n: _Weight(n, kinds[n], _cast_bf16(big_w[n], "cast_" + n)) for n in names}
    wins = {n: W[n].win for n in names}

    def whole(*ns):
        return [(W[n], W[n].win) for n in ns]

    ffn_in_parts = [(W["w_ffn_in"], win) for win in W["w_ffn_in"].win.split(4)]
    mid = ("w_attn_up", "w_pool_grp", "w_pool_up", "w_out")

    small_rows = _pad_rows(jnp.concatenate([meta_tokens, b_gate[0]], axis=0))
    gathered = _gather_small(small_rows)
    gathered = gathered.transpose(1, 0, 2).reshape(small_rows.shape[0], d)
    meta_full, b_gate_full = gathered[:N_META], gathered[N_META:N_META + 2]

    _gather_first_weight(W["w_in"])
    x2d, tgt2d = x[0], loss_target[0]
    g_in, b_in = ln_in_g.reshape(1, d), ln_in_b.reshape(1, d)
    h0, h0b = _ln_in_fwd(x2d, meta_full, g_in, b_in, nb)
    proj = _mm(h0b, W["w_in"].full, "nn", F32, 1408, 512, 4096, "mm_proj", side=_Side([_GatherIci(whole(*mid))]))
    cos, sin = _rope_tables(tp)
    n_rot = (attn_w + kv_w) // LANES
    qkv = _rope_fwd(proj, cos, sin, n_rot, qkv_w)
    q_hm = _heads_major(qkv[:, :attn_w], nq)
    k_hm = _heads_major(qkv[:, attn_w:attn_w + kv_w], N_KV)
    v_hm = _heads_major(qkv[:, attn_w + kv_w:], N_KV)
    sink4 = jnp.broadcast_to(attn_sinks.reshape(N_KV, grp, 1, 1), (N_KV, grp, BLK, 1))
    o_hm = _attn_fwd(q_hm, k_hm, v_hm, sink4, side=_Side([_GatherD2d(whole(*mid)), _GatherIci(ffn_in_parts[0:1])]))
    o = _heads_minor(o_hm)
    wf_grp = W["w_pool_grp"].full.reshape(N_CHIPS, N_GRP, gw // N_CHIPS, gw).transpose(1, 0, 2, 3).reshape(N_GRP, gw, gw)
    pooled, mx, pm = _pool_fwd(proj, wf_grp, pool_scale, qkv_w, pool_w,
                               side=_Side([_GatherD2d(ffn_in_parts[0:1]), _GatherIci(ffn_in_parts[1:2])]))
    a_out = _mm(o, W["w_attn_up"].full, "nn", F32, 1408, 1024, 2048, "mm_attn_up", side=_Side([_GatherD2d(ffn_in_parts[1:2])]))
    p_out = _mm(pm, W["w_pool_up"].full, "nn", F32, 1408, 1024, 2048, "mm_pool_up", side=_Side([_GatherIci(ffn_in_parts[2:3])]))
    mixed = _mix_fwd(proj, b_gate_full, a_out, p_out, g_off, side=_Side([_GatherD2d(ffn_in_parts[2:3]), _GatherIci(ffn_in_parts[3:4])]))
    z1 = _mm(mixed, W["w_out"].full, "nn", F32, 1408, 512, 4096, "mm_out", side=_Side([_GatherD2d(ffn_in_parts[3:4])]))
    r1, h1, h1b = _res_ln_fwd(h0, z1, ln1_g, ln1_b)
    ff = _mm(h1b, W["w_ffn_in"].full, "nn", F32, 1408, 512, 4096, "mm_ffn_in", side=_Side([_GatherIci(whole("w_ffn_down"))]))
    act = _swiglu_fwd(ff, side=_Side([_GatherD2d(whole("w_ffn_down"))]))
    wf_down = W["w_ffn_down"].full
    z2 = _mm(act, wf_down, "nn", F32, 704, 512, 5504, "mm_ffn_down")
    d_r2, d_r2b, loss_tile, dg2, db2 = _final_ln_loss(h1, z2, ln2_g, ln2_b, tgt2d)

    S = {n: _Shard(n) for n in names}

    def grads_of(name, grad, parts=1):
        return [_Grad(name, win, grad) for win in wins[name].split(parts)]

    def sibling(gs):
        return _ReduceSibling(gs)

    def chips(gs):
        for g in gs:
            _chip_sum(g, who)
        return _ReduceChips(gs)

    def share(gs):
        for g in gs:
            _final_sum(g, S[g.name], who)
        return _ShareReduced([(S[g.name], g.win) for g in gs])

    g6 = grads_of("w_ffn_down", _mm(act, d_r2b, "tn", F32, 256, 1024, tp, "mm_gw_ffn_down", j_outer=True))
    d_act = _mm(d_r2b, wf_down, "nt", F32, 1408, 256, 4096, "mm_d_act", side=_Side([sibling(g6)]))
    d_ff = _swiglu_bwd(ff, d_act)
    g5 = grads_of("w_ffn_in", _mm(h1b, d_ff, "tn", F32, 1024, 512, tp, "mm_gw_ffn_in", side=_Side([chips(g6)])), parts=2)
    d_h1_mm = _mm(d_ff, W["w_ffn_in"].full, "nt", F32, 1408, 2048, 512, "mm_d_h1", side=_Side([share(g6), sibling(g5)]))
    d_r1, d_r1b, dg1, db1 = _ln1_bwd(d_r2, d_h1_mm, r1, ln1_g)
    g4 = grads_of("w_out", _mm(mixed, d_r1b, "tn", F32, 1024, 512, tp, "mm_gw_out"))
    d_mixed = _mm(d_r1b, W["w_out"].full, "nt", F32, 1408, 512, 4096, "mm_d_mixed", side=_Side([sibling(g4)]))
    d_a, d_p, d_gl0, d_gl1, d_bgate = _mix_bwd(proj, b_gate_full, a_out, p_out, d_mixed, g_off, side=_Side([chips(g4)]))
    g1 = grads_of("w_attn_up", _mm(o, d_a, "tn", F32, 1024, 512, tp, "mm_gw_attn_up", side=_Side([share(g4)])))
    d_o = _mm(d_a, W["w_attn_up"].full, "nt", F32, 1408, 512, 4096, "mm_d_o", side=_Side([sibling(g1)]))
    g3 = grads_of("w_pool_up", _mm(pm, d_p, "tn", F32, 1024, 512, tp, "mm_gw_pool_up", side=_Side([chips(g1)])))
    d_pm = _mm(d_p, W["w_pool_up"].full, "nt", F32, 1408, 512, 4096, "mm_d_pm", side=_Side([sibling(g3)]))
    d_pooled, gw_grp, d_scale = _pool_bwd_mix(d_pm, mx, pooled, wf_grp, pool_scale, side=_Side([chips(g3), share(g1)]))
    gw_grp_sm = gw_grp.reshape(N_GRP, N_CHIPS, gw // N_CHIPS, gw).transpose(1, 0, 2, 3).reshape(N_CHIPS * grp_shard[0], gw)
    g2 = grads_of("w_pool_grp", gw_grp_sm)
    d_u = _pool_bwd_band(d_pooled, side=_Side([sibling(g2), share(g3)]))
    dq_hm, dk_cur, dk_prev, dk_meta, dv_cur, dv_prev, dv_meta, d_sink = _attn_bwd(
        q_hm, k_hm, v_hm, sink4, _heads_major(d_o, nq), side=_Side([chips(g5[0:1] + g2)]))
    d_qkv = _rope_bwd(_heads_minor(dq_hm), _heads_minor(dk_cur), _heads_minor(dk_prev), _heads_minor(dk_meta),
                      _heads_minor(dv_cur), _heads_minor(dv_prev), _heads_minor(dv_meta), cos, sin, side=_Side([share(g2)]))
    d_proj = jnp.concatenate([d_qkv, d_u, d_gl0, d_gl1], axis=1)
    g0 = grads_of("w_in", _mm(h0b, d_proj, "tn", F32, 1024, 512, tp, "mm_gw_in", side=_Side([chips(g5[1:2]), share(g5[0:1])])), parts=2)
    d_h0_mm = _mm(d_proj, W["w_in"].full, "nt", F32, 1408, 1024, 2560, "mm_d_h0", side=_Side([sibling(g0), share(g5[1:2])]))
    grad_x2d, d_meta, dg_in, db_in = _ln_in_bwd(d_r1, d_h0_mm, x2d, meta_full, g_in)

    small_parts = [d_meta, d_bgate, dg_in, db_in, dg1, db1, dg2, db2, _rows_of(d_scale, d), _rows_of(d_sink[:, :, 0, 0], d)]
    offs = [0]
    for p in small_parts:
        offs.append(offs[-1] + p.shape[0])
    red = _all_reduce_small(_pad_rows(jnp.concatenate(small_parts, axis=0)))
    r_meta, r_bgate, r_g_in, r_b_in, r_g1, r_b1, r_g2, r_b2, r_scale, r_sink = [red[offs[k]:offs[k + 1]] for k in range(len(small_parts))]
    col0 = s_me * dc
    g_meta = lax.dynamic_slice(r_meta, (0, col0), (N_META, dc))
    g_bgate = lax.dynamic_slice(r_bgate, (0, col0), (2, dc))
    g_scale = r_scale.reshape(-1)[:pool_w]
    g_sink = r_sink.reshape(-1)[:nq]

    upd = {}

    def adamw(n, side=None):
        upd[n] = _adamw(big_w[n], S[n].arr, big_m[n], big_v[n], "adamw_" + n, side=side)

    adamw("w_ffn_in", side=_Side([chips(g0[0:1])]))
    adamw("w_ffn_down", side=_Side([chips(g0[1:2]), share(g0[0:1])]))
    adamw("w_out", side=_Side([share(g0[1:2])]))
    for n in ("w_attn_up", "w_pool_grp", "w_pool_up", "w_in"):
        adamw(n)

    small_w = [meta_tokens, b_gate[0], ln_in_g, ln_in_b, attn_sinks, pool_scale, ln1_g, ln1_b, ln2_g, ln2_b]
    small_m = [m_meta_tokens, m_b_gate[0], m_ln_in_g, m_ln_in_b, m_attn_sinks, m_pool_scale, m_ln1_g, m_ln1_b, m_ln2_g, m_ln2_b]
    small_v = [v_meta_tokens, v_b_gate[0], v_ln_in_g, v_ln_in_b, v_attn_sinks, v_pool_scale, v_ln1_g, v_ln1_b, v_ln2_g, v_ln2_b]
    small_g = [g_meta, g_bgate, r_g_in, r_b_in, g_sink, g_scale, r_g1, r_b1, r_g2, r_b2]
    small_g = [g.reshape(w.shape) for g, w in zip(small_g, small_w)]

    def pack(parts):
        return _pad_rows(jnp.concatenate([_rows_of(p, dc) for p in parts], axis=0))

    s_delta, s_m, s_v = _adamw(pack(small_w), pack(small_g), pack(small_m), pack(small_v), "adamw_small")

    def unpack(packed):
        out, row = [], 0
        for w in small_w:
            nrow = -(-w.size // dc)
            out.append(packed[row:row + nrow].reshape(-1)[:w.size].reshape(w.shape))
            row += nrow
        return out

    s_delta, s_m, s_v = unpack(s_delta), unpack(s_m), unpack(s_v)

    order = ["meta_tokens", "ln_in_g", "ln_in_b", "w_in", "b_gate", "attn_sinks", "w_attn_up", "w_pool_grp", "pool_scale",
             "w_pool_up", "w_out", "ln1_g", "ln1_b", "w_ffn_in", "w_ffn_down", "ln2_g", "ln2_b"]
    small_names = ["meta_tokens", "b_gate", "ln_in_g", "ln_in_b", "attn_sinks", "pool_scale", "ln1_g", "ln1_b", "ln2_g", "ln2_b"]
    out_shapes = dict(meta_tokens=meta_tokens.shape, ln_in_g=ln_in_g.shape, ln_in_b=ln_in_b.shape, w_in=w_in.shape, b_gate=b_gate.shape,
                      attn_sinks=attn_sinks.shape, w_attn_up=w_attn_up.shape, w_pool_grp=w_pool_grp.shape, pool_scale=pool_scale.shape,
                      w_pool_up=w_pool_up.shape, w_out=w_out.shape, ln1_g=ln1_g.shape, ln1_b=ln1_b.shape, w_ffn_in=w_ffn_in.shape,
                      w_ffn_down=w_ffn_down.shape, ln2_g=ln2_g.shape, ln2_b=ln2_b.shape)
    grads, deltas, new_m, new_v = {}, {}, {}, {}
    for n in names:
        grads[n], (deltas[n], new_m[n], new_v[n]) = S[n].arr, upd[n]
    for k, n in enumerate(small_names):
        grads[n], deltas[n], new_m[n], new_v[n] = small_g[k], s_delta[k], s_m[k], s_v[k]

    loss = lax.psum(loss_tile[0, 0], ("x", "y", "c"))
    outs = [loss, grad_x2d.reshape(x.shape)]
    for group in (grads, deltas, new_m, new_v):
        outs += [group[n].reshape(out_shapes[n]) for n in order]
    return tuple(outs)
```

```python
import functools
import math

import jax
import jax.numpy as jnp
from jax import lax
from jax.experimental import pallas as pl
from jax.experimental.pallas import tpu as pltpu

F32 = jnp.float32
BF16 = jnp.bfloat16
MESH = pl.DeviceIdType.MESH
ANY = pl.BlockSpec(memory_space=pl.ANY)
VMEM_FULL = pl.BlockSpec(memory_space=pltpu.VMEM)

N_META = 16
HEAD_DIM = 64
N_KV = 4
BLK = 128
LEAD = (-N_META) % BLK
ROPE_DIM = HEAD_DIM // 4
ROPE_THETA = 500000.0
NEG_INF = -1e30
POOL_WINDOWS = (2, 4, 8, 16)
N_GRP = len(POOL_WINDOWS)
LN_EPS = 1e-5
DN_ALPHA = 2.0 ** 0.25
ADAM_LR = 0.001
ADAM_B1 = 0.9
ADAM_B2 = 0.999
ADAM_EPS = 1e-08
ADAM_WD = 0.01
ADAM_STEP = 10
N_CHIPS = 4
N_DEV = 8
LANES = 128
VMEM_LIMIT_MB = 56
HI = lax.Precision.HIGHEST


def _cparams(sem=None, vmem_mb=VMEM_LIMIT_MB):
    kw = dict(vmem_limit_bytes=vmem_mb << 20)
    if sem is not None:
        kw["dimension_semantics"] = sem
    return pltpu.CompilerParams(**kw)


def _pick(dim, *cands):
    for c in cands:
        if c <= dim and dim % c == 0:
            return c
    return dim


def _sds(shape, dtype):
    return jax.ShapeDtypeStruct(tuple(shape), dtype)


def _coords():
    return lax.axis_index("x"), lax.axis_index("y"), lax.axis_index("c")


def _other_chips(x, y):
    return [(1 - x, y), (x, 1 - y), (1 - x, 1 - y)]


class _Win:
    def __init__(self, kind, shard_shape, row0=0, nrows=None):
        self.kind, self.shard_shape, self.row0 = kind, tuple(shard_shape), row0
        self.nrows = shard_shape[0] if nrows is None else nrows
        self.half = self.nrows // 2

    @property
    def piece_shape(self):
        return (self.half, self.shard_shape[1])

    @property
    def full_shape(self):
        r, c = self.shard_shape
        return (r, N_CHIPS * c) if self.kind == "col" else (N_CHIPS * r, c)

    def in_full(self, ref, s, h, q=None):
        r, c = self.shard_shape
        start, size = self.row0 + h * self.half, self.half
        if q is not None:
            start, size = start + q * (self.half // 2), self.half // 2
        if self.kind == "col":
            return ref.at[pl.ds(start, size), pl.ds(s * c, c)]
        return ref.at[pl.ds(s * r + start, size), :]

    def in_shard(self, ref, h):
        return ref.at[pl.ds(self.row0 + h * self.half, self.half), :]

    def split(self, n):
        return [_Win(self.kind, self.shard_shape, self.row0 + q * (self.nrows // n), self.nrows // n) for q in range(n)]


class _Weight:
    def __init__(self, name, kind, shard):
        self.name, self.kind, self.shard, self.full = name, kind, shard, None
        self.win = _Win(kind, shard.shape)


class _Grad:
    def __init__(self, name, win, grad):
        self.name, self.win, self.grad = name, win, grad
        self.key = "%s@%d" % (name, win.row0)
        self.other = self.chip = self.landed = None


class _Shard:
    def __init__(self, name):
        self.name, self.arr = name, None


class _Ctx:
    def __init__(self, side, in_refs, out_refs, send_sems, recv_sems, local_sems, base, lbase):
        self.side, self.in_refs, self.out_refs = side, in_refs, out_refs
        self.send_sems, self.recv_sems, self.local_sems, self.base, self.lbase = send_sems, recv_sems, local_sems, base, lbase

    def ref(self, key):
        info = self.side.info[key]
        return self.in_refs[info["in"]] if info["in"] is not None else self.out_refs[info["out"]]

    def remote(self, k, src, dst, to):
        return pltpu.make_async_remote_copy(src_ref=src, dst_ref=dst, send_sem=self.send_sems.at[self.base + k],
                                            recv_sem=self.recv_sems.at[self.base + k], device_id=to, device_id_type=MESH)

    def local(self, k, src, dst):
        return pltpu.make_async_copy(src, dst, self.local_sems.at[self.lbase + k])


class _Side:
    def __init__(self, ops):
        self.ops, self.info, self.keys = ops, {}, []
        self.nsem = self.nlocal = 0
        self.bases = []
        for op in ops:
            op.register(self)
            self.bases.append((self.nsem, self.nlocal))
            self.nsem += op.nsem
            self.nlocal += op.nlocal
        self.inputs, self.out_shape, self.aliases = [], [], {}
        for key in self.keys:
            info = self.info[key]
            info["in"] = info["out"] = None
            if info["arr"] is not None:
                info["in"] = len(self.inputs)
                self.inputs.append(info["arr"])
            if info["write"]:
                info["out"] = len(self.out_shape)
                self.out_shape.append(info["sds"])
                if info["in"] is not None:
                    self.aliases[info["in"]] = info["out"]

    def need(self, key, arr=None, sds=None, write=False):
        if key not in self.info:
            self.keys.append(key)
            self.info[key] = dict(arr=arr, sds=sds if arr is None else _sds(arr.shape, arr.dtype), write=write)
        else:
            self.info[key]["write"] = self.info[key]["write"] or write

    def _ctx(self, k, in_refs, out_refs, sems):
        return _Ctx(self, in_refs, out_refs, sems[0], sems[1], sems[2], *self.bases[k])

    def start(self, in_refs, out_refs, sems):
        for k, op in enumerate(self.ops):
            op.start(self._ctx(k, in_refs, out_refs, sems))

    def finish(self, in_refs, out_refs, sems):
        for k, op in enumerate(self.ops):
            op.finish(self._ctx(k, in_refs, out_refs, sems))

    def scratch(self):
        return [pltpu.SemaphoreType.DMA((max(self.nsem, 1),)), pltpu.SemaphoreType.DMA((max(self.nsem, 1),)),
                pltpu.SemaphoreType.DMA((max(self.nlocal, 1),))]

    def commit(self, outs):
        res = {key: outs[self.info[key]["out"]] for key in self.keys if self.info[key]["write"]}
        for op in self.ops:
            op.commit(res)


NEIGHBOURS, DIAGONAL = (0, 1), (2,)


class _GatherIci:
    def __init__(self, pairs):
        self.pairs = pairs
        self.nsem, self.nlocal = 2 * len(pairs), 2 * len(pairs)

    def register(self, side):
        for w, win in self.pairs:
            side.need(("shard", w.name), arr=w.shard)
            side.need(("full", w.name), arr=w.full, sds=_sds(win.full_shape, BF16), write=True)

    def _copies(self, ctx):
        x, y, c = _coords()
        s_me = 2 * x + y
        local, sends, recvs = [], [], []
        for t, (w, win) in enumerate(self.pairs):
            shard, full = ctx.ref(("shard", w.name)), ctx.ref(("full", w.name))
            for h in range(2):
                local.append(ctx.local(2 * t + h, win.in_shard(shard, h), win.in_full(full, s_me, h)))
            for j in NEIGHBOURS:
                ox, oy = _other_chips(x, y)[j]
                sends.append(ctx.remote(2 * t + j, win.in_shard(shard, c), win.in_full(full, s_me, c), (ox, oy, c)))
                landing = win.in_full(full, 2 * ox + oy, c)
                recvs.append(ctx.remote(2 * t + j, landing, landing, (x, y, c)))
        return local, sends, recvs

    def start(self, ctx):
        local, sends, _ = self._copies(ctx)
        for cp in local + sends:
            cp.start()

    def finish(self, ctx):
        local, sends, recvs = self._copies(ctx)
        for cp in recvs:
            cp.wait_recv()
        for cp in sends:
            cp.wait_send()
        for cp in local:
            cp.wait()

    def commit(self, res):
        for w, _ in self.pairs:
            w.full = res[("full", w.name)]


class _GatherRing:
    def __init__(self, pairs):
        self.pairs = pairs
        self.nsem, self.nlocal = 2 * len(pairs), 0

    def register(self, side):
        for w, win in self.pairs:
            side.need(("full", w.name), arr=w.full, write=True)

    def _copies(self, ctx):
        x, y, c = _coords()
        s_x, s_y, s_d = 2 * (1 - x) + y, 2 * x + (1 - y), 2 * (1 - x) + (1 - y)
        sends, recvs = [], []
        for t, (w, win) in enumerate(self.pairs):
            full = ctx.ref(("full", w.name))
            for q, (s_from, to) in enumerate([(s_x, (x, 1 - y, c)), (s_y, (1 - x, y, c))]):
                passed, landing = win.in_full(full, s_from, c, q), win.in_full(full, s_d, c, q)
                sends.append(ctx.remote(2 * t + q, passed, passed, to))
                recvs.append(ctx.remote(2 * t + q, landing, landing, (x, y, c)))
        return sends, recvs

    def start(self, ctx):
        for cp in self._copies(ctx)[0]:
            cp.start()

    def finish(self, ctx):
        sends, recvs = self._copies(ctx)
        for cp in recvs:
            cp.wait_recv()
        for cp in sends:
            cp.wait_send()

    def commit(self, res):
        for w, _ in self.pairs:
            w.full = res[("full", w.name)]


class _GatherD2d:
    def __init__(self, pairs, which):
        self.pairs, self.which = pairs, which
        self.nsem, self.nlocal = 3 * len(pairs), 0

    def register(self, side):
        for w, win in self.pairs:
            side.need(("full", w.name), arr=w.full, write=True)

    def _copies(self, ctx):
        x, y, c = _coords()
        sends, recvs = [], []
        for t, (w, win) in enumerate(self.pairs):
            full = ctx.ref(("full", w.name))
            for j in self.which:
                ox, oy = _other_chips(x, y)[j]
                mine, theirs = win.in_full(full, 2 * ox + oy, c), win.in_full(full, 2 * ox + oy, 1 - c)
                sends.append(ctx.remote(3 * t + j, mine, mine, (x, y, 1 - c)))
                recvs.append(ctx.remote(3 * t + j, theirs, theirs, (x, y, c)))
        return sends, recvs

    def start(self, ctx):
        for cp in self._copies(ctx)[0]:
            cp.start()

    def finish(self, ctx):
        sends, recvs = self._copies(ctx)
        for cp in recvs:
            cp.wait_recv()
        for cp in sends:
            cp.wait_send()

    def commit(self, res):
        for w, _ in self.pairs:
            w.full = res[("full", w.name)]


class _ReduceSibling:
    def __init__(self, grads):
        self.grads = grads
        self.nsem, self.nlocal = N_CHIPS * len(grads), 0

    def register(self, side):
        for g in self.grads:
            side.need(("grad", g.name), arr=g.grad)
            side.need(("other", g.key), sds=_sds((N_CHIPS,) + g.win.piece_shape, F32), write=True)

    def _copies(self, ctx):
        x, y, c = _coords()
        out = []
        for t, g in enumerate(self.grads):
            grad, other = ctx.ref(("grad", g.name)), ctx.ref(("other", g.key))
            for s in range(N_CHIPS):
                out.append(ctx.remote(N_CHIPS * t + s, g.win.in_full(grad, s, 1 - c), other.at[s], (x, y, 1 - c)))
        return out

    def start(self, ctx):
        for cp in self._copies(ctx):
            cp.start()

    def finish(self, ctx):
        for cp in self._copies(ctx):
            cp.wait()

    def commit(self, res):
        for g in self.grads:
            g.other = res[("other", g.key)]


class _ReduceChips:
    def __init__(self, grads):
        self.grads = grads
        self.nsem, self.nlocal = 3 * len(grads), 0

    def register(self, side):
        for g in self.grads:
            side.need(("chip", g.key), arr=g.chip)
            side.need(("landed", g.key), sds=_sds(g.chip.shape, g.chip.dtype), write=True)

    def _copies(self, ctx):
        x, y, c = _coords()
        s_me = 2 * x + y
        out = []
        for t, g in enumerate(self.grads):
            chip, landed = ctx.ref(("chip", g.key)), ctx.ref(("landed", g.key))
            for j, (ox, oy) in enumerate(_other_chips(x, y)):
                out.append(ctx.remote(3 * t + j, chip.at[2 * ox + oy], landed.at[s_me], (ox, oy, c)))
        return out

    def start(self, ctx):
        for cp in self._copies(ctx):
            cp.start()

    def finish(self, ctx):
        for cp in self._copies(ctx):
            cp.wait()

    def commit(self, res):
        for g in self.grads:
            g.landed = res[("landed", g.key)]


class _ShareReduced:
    def __init__(self, items):
        self.items = items
        self.nsem, self.nlocal = len(items), 0

    def register(self, side):
        for sh, _ in self.items:
            side.need(("reduced", sh.name), arr=sh.arr, write=True)

    def _copies(self, ctx):
        x, y, c = _coords()
        sends, recvs = [], []
        for t, (sh, win) in enumerate(self.items):
            ref = ctx.ref(("reduced", sh.name))
            sends.append(ctx.remote(t, win.in_shard(ref, c), win.in_shard(ref, c), (x, y, 1 - c)))
            recvs.append(ctx.remote(t, win.in_shard(ref, 1 - c), win.in_shard(ref, 1 - c), (x, y, c)))
        return sends, recvs

    def start(self, ctx):
        for cp in self._copies(ctx)[0]:
            cp.start()

    def finish(self, ctx):
        sends, recvs = self._copies(ctx)
        for cp in recvs:
            cp.wait_recv()
        for cp in sends:
            cp.wait_send()

    def commit(self, res):
        for sh, _ in self.items:
            sh.arr = res[("reduced", sh.name)]


def _gcall(body, name, grid, in_specs, out_specs, out_shape, args, scratch=(), sem=None, side=None):
    single = not isinstance(out_shape, (list, tuple))
    out_shapes = [out_shape] if single else list(out_shape)
    out_specs = [out_specs] if single else list(out_specs)
    if side is None or not side.ops:
        res = pl.pallas_call(body, name=name, grid=grid, in_specs=list(in_specs), out_specs=out_specs, out_shape=out_shapes,
                             scratch_shapes=list(scratch), compiler_params=_cparams(sem))(*args)
        return res[0] if single else res
    n_in, n_out, n_scr = len(args), len(out_shapes), len(scratch)
    ns_in, ns_out = len(side.inputs), len(side.out_shape)

    def wrapped(*refs):
        a, si = refs[:n_in], refs[n_in:n_in + ns_in]
        o = refs[n_in + ns_in:n_in + ns_in + n_out]
        so = refs[n_in + ns_in + n_out:n_in + ns_in + n_out + ns_out]
        rest = refs[n_in + ns_in + n_out + ns_out:]
        scr, sems = rest[:n_scr], rest[n_scr:]
        ids = [pl.program_id(k) for k in range(len(grid))]
        first = functools.reduce(jnp.logical_and, [i == 0 for i in ids])
        last = functools.reduce(jnp.logical_and, [i == g - 1 for i, g in zip(ids, grid)])

        @pl.when(first)
        def _():
            side.start(si, so, sems)

        body(*a, *o, *scr)

        @pl.when(last)
        def _():
            side.finish(si, so, sems)

    res = pl.pallas_call(
        wrapped, name=name, grid=grid,
        in_specs=list(in_specs) + [ANY] * ns_in,
        out_specs=out_specs + [ANY] * ns_out,
        out_shape=out_shapes + side.out_shape,
        scratch_shapes=list(scratch) + side.scratch(),
        input_output_aliases={n_in + i: n_out + j for i, j in side.aliases.items()},
        compiler_params=_cparams(("arbitrary",) * len(grid)),
    )(*args, *side.inputs)
    side.commit(res[n_out:])
    return res[0] if single else res[:n_out]


_DOT_DIMS = {
    "nn": (((1,), (0,)), ((), ())),
    "nt": (((1,), (1,)), ((), ())),
    "tn": (((0,), (0,)), ((), ())),
}


def _mm(a, b, mode, out_dtype, tm, tn, tk, name, j_outer=False, side=None):
    if mode == "nn":
        (m, k), n = a.shape, b.shape[1]
    elif mode == "nt":
        (m, k), n = a.shape, b.shape[0]
    else:
        (k, m), n = a.shape, b.shape[1]
    tm, tn, tk = _pick(m, tm), _pick(n, tn), _pick(k, tk)
    gi, gj, gk = m // tm, n // tn, k // tk
    dims = _DOT_DIMS[mode]

    def ij(g0, g1):
        return (g1, g0) if j_outer else (g0, g1)

    if mode == "tn":
        a_spec = pl.BlockSpec((tk, tm), lambda g0, g1, kk: (kk, ij(g0, g1)[0]))
    else:
        a_spec = pl.BlockSpec((tm, tk), lambda g0, g1, kk: (ij(g0, g1)[0], kk))
    if mode == "nt":
        b_spec = pl.BlockSpec((tn, tk), lambda g0, g1, kk: (ij(g0, g1)[1], kk))
    else:
        b_spec = pl.BlockSpec((tk, tn), lambda g0, g1, kk: (kk, ij(g0, g1)[1]))
    o_spec = pl.BlockSpec((tm, tn), lambda g0, g1, kk: ij(g0, g1))

    def body(a_ref, b_ref, o_ref, *scr):
        p = lax.dot_general(a_ref[...], b_ref[...], dims, preferred_element_type=F32)
        if gk == 1:
            o_ref[...] = p.astype(out_dtype)
        else:
            acc = scr[0]
            kk = pl.program_id(2)

            @pl.when(kk == 0)
            def _():
                acc[...] = p

            @pl.when(kk > 0)
            def _():
                acc[...] += p

            @pl.when(kk == gk - 1)
            def _():
                o_ref[...] = acc[...].astype(out_dtype)

    return _gcall(body, name, (gj, gi, gk) if j_outer else (gi, gj, gk), [a_spec, b_spec], o_spec, _sds((m, n), out_dtype), (a, b),
                  scratch=[pltpu.VMEM((tm, tn), F32)] if gk > 1 else [], sem=("parallel", "parallel", "arbitrary"), side=side)


def _stream_block(i, x_ref, meta_ref):
    d = x_ref.shape[-1]
    first = jnp.concatenate([jnp.zeros((LEAD, d), F32), meta_ref[...]], axis=0)
    return jnp.where(i == 0, first, x_ref[...])


def _norm(xb):
    mu = jnp.mean(xb, axis=-1, keepdims=True)
    xc = xb - mu
    var = jnp.mean(xc * xc, axis=-1, keepdims=True)
    rstd = lax.rsqrt(var + LN_EPS)
    return xc * rstd, rstd


def _ln_bwd_rows(dy, xhat, rstd, g):
    dyg = dy * g
    m1 = jnp.mean(dyg, axis=-1, keepdims=True)
    m2 = jnp.mean(dyg * xhat, axis=-1, keepdims=True)
    return rstd * (dyg - m1 - xhat * m2)


def _ln_in_fwd(x2d, meta, g, b, nb, side=None):
    seq, d = x2d.shape

    def body(x_ref, meta_ref, g_ref, b_ref, h_ref, hb_ref):
        xb = _stream_block(pl.program_id(0), x_ref, meta_ref)
        xhat, _ = _norm(xb)
        y = xhat * g_ref[...] + b_ref[...]
        h_ref[...] = y
        hb_ref[...] = y.astype(BF16)

    row = pl.BlockSpec((BLK, d), lambda i: (i, 0))
    vec = pl.BlockSpec((1, d), lambda i: (0, 0))
    return _gcall(body, "ln_in_fwd", (nb,),
                  [pl.BlockSpec((BLK, d), lambda i: (jnp.maximum(i - 1, 0), 0)), pl.BlockSpec((N_META, d), lambda i: (0, 0)), vec, vec],
                  [row, row], [_sds((nb * BLK, d), F32), _sds((nb * BLK, d), BF16)], (x2d, meta, g, b), sem=("parallel",), side=side)


def _res_ln_fwd(h, z, g, b, side=None):
    tp, d = h.shape

    def body(h_ref, z_ref, g_ref, b_ref, r_ref, y_ref, yb_ref):
        r = DN_ALPHA * h_ref[...] + z_ref[...]
        xhat, _ = _norm(r)
        y = xhat * g_ref[...] + b_ref[...]
        r_ref[...] = r
        y_ref[...] = y
        yb_ref[...] = y.astype(BF16)

    row = pl.BlockSpec((BLK, d), lambda i: (i, 0))
    vec = pl.BlockSpec((1, d), lambda i: (0, 0))
    return _gcall(body, "res_ln1_fwd", (tp // BLK,), [row, row, vec, vec], [row, row, row],
                  [_sds((tp, d), F32), _sds((tp, d), F32), _sds((tp, d), BF16)], (h, z, g, b), sem=("parallel",), side=side)


def _final_ln_loss(h1, z2, g, b, tgt):
    tp, d = h1.shape

    def body(h_ref, z_ref, g_ref, b_ref, t_ref, dr_ref, drb_ref, loss_ref, dg_ref, db_ref):
        i = pl.program_id(0)
        r = DN_ALPHA * h_ref[...] + z_ref[...]
        xhat, rstd = _norm(r)
        y = xhat * g_ref[...] + b_ref[...]
        err = jnp.where(i >= 1, y - t_ref[...], 0.0)
        dy = err * (1.0 / d)
        dr = _ln_bwd_rows(dy, xhat, rstd, g_ref[...])
        dr_ref[...] = dr
        drb_ref[...] = dr.astype(BF16)

        @pl.when(i == 0)
        def _():
            loss_ref[...] = jnp.zeros_like(loss_ref)
            dg_ref[...] = jnp.zeros_like(dg_ref)
            db_ref[...] = jnp.zeros_like(db_ref)

        loss_ref[...] += 0.5 * jnp.sum(jnp.sum(err * err, axis=-1, keepdims=True) * (1.0 / d), axis=0, keepdims=True)
        dg_ref[...] += jnp.sum(dy * xhat, axis=0, keepdims=True)
        db_ref[...] += jnp.sum(dy, axis=0, keepdims=True)

    row = pl.BlockSpec((BLK, d), lambda i: (i, 0))
    vec = pl.BlockSpec((1, d), lambda i: (0, 0))
    return _gcall(body, "final_ln_loss", (tp // BLK,),
                  [row, row, vec, vec, pl.BlockSpec((BLK, d), lambda i: (jnp.maximum(i - 1, 0), 0))],
                  [row, row, pl.BlockSpec((8, LANES), lambda i: (0, 0)), vec, vec],
                  [_sds((tp, d), F32), _sds((tp, d), BF16), _sds((8, LANES), F32), _sds((1, d), F32), _sds((1, d), F32)],
                  (h1, z2, g, b, tgt), sem=("arbitrary",))


def _ln1_bwd(d_res, d_mm, r, g, side=None):
    tp, d = r.shape

    def body(a_ref, m_ref, r_ref, g_ref, dr_ref, drb_ref, dg_ref, db_ref):
        dy = DN_ALPHA * a_ref[...] + m_ref[...]
        xhat, rstd = _norm(r_ref[...])
        dr = _ln_bwd_rows(dy, xhat, rstd, g_ref[...])
        dr_ref[...] = dr
        drb_ref[...] = dr.astype(BF16)

        @pl.when(pl.program_id(0) == 0)
        def _():
            dg_ref[...] = jnp.zeros_like(dg_ref)
            db_ref[...] = jnp.zeros_like(db_ref)

        dg_ref[...] += jnp.sum(dy * xhat, axis=0, keepdims=True)
        db_ref[...] += jnp.sum(dy, axis=0, keepdims=True)

    row = pl.BlockSpec((BLK, d), lambda i: (i, 0))
    vec = pl.BlockSpec((1, d), lambda i: (0, 0))
    return _gcall(body, "ln1_bwd", (tp // BLK,), [row, row, row, vec], [row, row, vec, vec],
                  [_sds((tp, d), F32), _sds((tp, d), BF16), _sds((1, d), F32), _sds((1, d), F32)], (d_res, d_mm, r, g),
                  sem=("arbitrary",), side=side)


def _ln_in_bwd(d_res, d_mm, x2d, meta, g, side=None):
    seq, d = x2d.shape
    nb = d_res.shape[0] // BLK

    def body(a_ref, m_ref, x_ref, meta_ref, g_ref, gx_ref, gm_ref, dg_ref, db_ref):
        i = pl.program_id(0)
        dy = DN_ALPHA * a_ref[...] + m_ref[...]
        xhat, rstd = _norm(_stream_block(i, x_ref, meta_ref))
        dx = _ln_bwd_rows(dy, xhat, rstd, g_ref[...])
        gx_ref[...] = dx

        @pl.when(i == 0)
        def _():
            gm_ref[...] = dx[LEAD:, :]
            dg_ref[...] = jnp.zeros_like(dg_ref)
            db_ref[...] = jnp.zeros_like(db_ref)

        dg_ref[...] += jnp.sum(dy * xhat, axis=0, keepdims=True)
        db_ref[...] += jnp.sum(dy, axis=0, keepdims=True)

    row = pl.BlockSpec((BLK, d), lambda i: (i, 0))
    xrow = pl.BlockSpec((BLK, d), lambda i: (jnp.maximum(i - 1, 0), 0))
    vec = pl.BlockSpec((1, d), lambda i: (0, 0))
    met = pl.BlockSpec((N_META, d), lambda i: (0, 0))
    return _gcall(body, "ln_in_bwd", (nb,), [row, row, xrow, met, vec], [xrow, met, vec, vec],
                  [_sds((seq, d), F32), _sds((N_META, d), F32), _sds((1, d), F32), _sds((1, d), F32)], (d_res, d_mm, x2d, meta, g),
                  sem=("arbitrary",), side=side)


def _rope_tables(tp):
    half = ROPE_DIM // 2
    inv_freq = ROPE_THETA ** (-jnp.arange(half, dtype=F32) * 2.0 / ROPE_DIM)
    pos = (jnp.arange(tp) - LEAD).astype(F32)
    ang = pos[:, None] * inv_freq[None, :]
    cos, sin = jnp.cos(ang), jnp.sin(ang)
    ones = jnp.ones((tp, HEAD_DIM - ROPE_DIM), F32)
    cos_h = jnp.concatenate([cos, cos, ones], axis=1)
    sin_h = jnp.concatenate([-sin, sin, 0.0 * ones], axis=1)
    reps = LANES // HEAD_DIM
    return jnp.tile(cos_h, (1, reps)), jnp.tile(sin_h, (1, reps))


def _rope_partner(x):
    half = ROPE_DIM // 2
    lane = lax.broadcasted_iota(jnp.int32, x.shape, 1) % HEAD_DIM
    upper = jnp.where(lane < ROPE_DIM, pltpu.roll(x, half, 1), 0.0)
    return jnp.where(lane < half, pltpu.roll(x, LANES - half, 1), upper)


def _rope_fwd(proj, cos, sin, n_rot, width):
    tp = proj.shape[0]

    def body(p_ref, c_ref, s_ref, o_ref):
        c, s = c_ref[...], s_ref[...]
        for j in range(width // LANES):
            sl = slice(j * LANES, (j + 1) * LANES)
            xj = p_ref[:, sl]
            if j < n_rot:
                xj = xj * c + _rope_partner(xj) * s
            o_ref[:, sl] = xj.astype(BF16)

    tab = pl.BlockSpec((BLK, LANES), lambda i: (i, 0))
    blk = pl.BlockSpec((BLK, width), lambda i: (i, 0))
    return _gcall(body, "rope_fwd", (tp // BLK,), [blk, tab, tab], blk, _sds((tp, width), BF16), (proj, cos, sin), sem=("parallel",))


def _rope_bwd(dq, dk_cur, dk_prev, dk_meta, dv_cur, dv_prev, dv_meta, cos, sin, side=None):
    tp, aw = dq.shape
    kw = dk_cur.shape[1]
    nb = tp // BLK

    def body(dq_ref, kc_ref, kp_ref, km_ref, vc_ref, vp_ref, vm_ref, c_ref, s_ref, o_ref):
        i = pl.program_id(0)
        c, s = c_ref[...], s_ref[...]
        has_next = i + 1 < nb

        def unrot(g):
            return g * c + _rope_partner(g * s)

        def kv_sum(cur, prv, met):
            return cur[...] + jnp.where(has_next, prv[...], 0.0) + jnp.where(i == 0, met[...], 0.0)

        for j in range(aw // LANES):
            sl = slice(j * LANES, (j + 1) * LANES)
            o_ref[:, sl] = unrot(dq_ref[:, sl]).astype(BF16)
        dk = kv_sum(kc_ref, kp_ref, km_ref)
        dv = kv_sum(vc_ref, vp_ref, vm_ref)
        for j in range(kw // LANES):
            sl = slice(j * LANES, (j + 1) * LANES)
            o_ref[:, aw + j * LANES:aw + (j + 1) * LANES] = unrot(dk[:, sl]).astype(BF16)
            o_ref[:, aw + kw + j * LANES:aw + kw + (j + 1) * LANES] = dv[:, sl].astype(BF16)

    cur = pl.BlockSpec((BLK, kw), lambda i: (i, 0))
    nxt = pl.BlockSpec((BLK, kw), lambda i: (jnp.minimum(i + 1, nb - 1), 0))
    met = pl.BlockSpec((BLK, kw), lambda i: (0, 0))
    tab = pl.BlockSpec((BLK, LANES), lambda i: (i, 0))
    return _gcall(body, "rope_bwd", (nb,), [pl.BlockSpec((BLK, aw), lambda i: (i, 0)), cur, nxt, met, cur, nxt, met, tab, tab],
                  pl.BlockSpec((BLK, aw + 2 * kw), lambda i: (i, 0)), _sds((tp, aw + 2 * kw), BF16),
                  (dq, dk_cur, dk_prev, dk_meta, dv_cur, dv_prev, dv_meta, cos, sin), sem=("parallel",), side=side)


def _attn_probs(n, q_ref, km_ref, kp_ref, kc_ref, sink_ref, grp):
    scale = HEAD_DIM ** -0.5
    qs = q_ref[...].reshape(grp * BLK, HEAD_DIM)
    kcat = jnp.concatenate([km_ref[...], kp_ref[...], kc_ref[...]], axis=0)
    s = lax.dot_general(qs, kcat, _DOT_DIMS["nt"], preferred_element_type=F32) * scale
    s = s.reshape(grp, BLK, 3 * BLK)
    r = lax.broadcasted_iota(jnp.int32, (1, BLK, 3 * BLK), 1)
    j = lax.broadcasted_iota(jnp.int32, (1, BLK, 3 * BLK), 2)
    q_idx = n * BLK + r
    meta_ok = (j >= LEAD) & (j < BLK) & (q_idx >= j)
    k_idx = (n - 1) * BLK + (j - BLK)
    diff = q_idx - k_idx
    band_ok = (j >= BLK) & (diff >= 0) & (diff < BLK) & (k_idx >= LEAD + N_META)
    s = jnp.where(meta_ok | band_ok, s, NEG_INF)
    sink = sink_ref[...]
    m = jnp.maximum(jnp.max(s, axis=-1, keepdims=True), sink)
    p = jnp.exp(s - m)
    e_sink = jnp.exp(sink - m)
    inv = 1.0 / (jnp.sum(p, axis=-1, keepdims=True) + e_sink)
    return qs, kcat, p * inv, e_sink * inv


def _attn_specs(grp):
    qspec = pl.BlockSpec((grp, BLK, HEAD_DIM), lambda kk, n: (kk, n, 0))
    kmeta = pl.BlockSpec((None, BLK, HEAD_DIM), lambda kk, n: (kk, 0, 0))
    kprev = pl.BlockSpec((None, BLK, HEAD_DIM), lambda kk, n: (kk, jnp.maximum(n - 1, 0), 0))
    kcur = pl.BlockSpec((None, BLK, HEAD_DIM), lambda kk, n: (kk, n, 0))
    sink = pl.BlockSpec((None, grp, BLK, 1), lambda kk, n: (kk, 0, 0, 0))
    return qspec, kmeta, kprev, kcur, sink


def _attn_fwd(q_hm, k_hm, v_hm, sink4, side=None):
    nq, tp, _ = q_hm.shape
    nkv = k_hm.shape[0]
    grp = nq // nkv

    def body(q_ref, km_ref, kp_ref, kc_ref, vm_ref, vp_ref, vc_ref, sink_ref, o_ref):
        n = pl.program_id(1)
        _, _, pn, _ = _attn_probs(n, q_ref, km_ref, kp_ref, kc_ref, sink_ref, grp)
        vcat = jnp.concatenate([vm_ref[...], vp_ref[...], vc_ref[...]], axis=0)
        o = jnp.dot(pn.reshape(grp * BLK, 3 * BLK).astype(BF16), vcat, preferred_element_type=F32)
        o_ref[...] = o.reshape(grp, BLK, HEAD_DIM).astype(BF16)

    qspec, kmeta, kprev, kcur, sink = _attn_specs(grp)
    return _gcall(body, "attn_fwd", (nkv, tp // BLK), [qspec, kmeta, kprev, kcur, kmeta, kprev, kcur, sink], qspec,
                  _sds((nq, tp, HEAD_DIM), BF16), (q_hm, k_hm, k_hm, k_hm, v_hm, v_hm, v_hm, sink4), sem=("parallel", "parallel"), side=side)


def _attn_bwd(q_hm, k_hm, v_hm, sink4, do_hm, side=None):
    nq, tp, _ = q_hm.shape
    nkv = k_hm.shape[0]
    grp = nq // nkv
    scale = HEAD_DIM ** -0.5

    def body(q_ref, km_ref, kp_ref, kc_ref, vm_ref, vp_ref, vc_ref, sink_ref, do_ref,
             dq_ref, dkc_ref, dkp_ref, dkm_ref, dvc_ref, dvp_ref, dvm_ref, dsk_ref):
        n = pl.program_id(1)
        qs, kcat, pn, p_sink = _attn_probs(n, q_ref, km_ref, kp_ref, kc_ref, sink_ref, grp)
        vcat = jnp.concatenate([vm_ref[...], vp_ref[...], vc_ref[...]], axis=0)
        pn2 = pn.reshape(grp * BLK, 3 * BLK)
        pnb = pn2.astype(BF16)
        dob = do_ref[...].reshape(grp * BLK, HEAD_DIM).astype(BF16)
        dp = lax.dot_general(dob, vcat, _DOT_DIMS["nt"], preferred_element_type=F32)
        delta = jnp.sum(pn2 * dp, axis=-1, keepdims=True)
        ds = (pn2 * (dp - delta) * scale).astype(BF16)
        dq_ref[...] = jnp.dot(ds, kcat, preferred_element_type=F32).reshape(grp, BLK, HEAD_DIM)
        dk = lax.dot_general(ds, qs, _DOT_DIMS["tn"], preferred_element_type=F32)
        dv = lax.dot_general(pnb, dob, _DOT_DIMS["tn"], preferred_element_type=F32)
        dkp_ref[...] = dk[BLK:2 * BLK]
        dkc_ref[...] = dk[2 * BLK:]
        dvp_ref[...] = dv[BLK:2 * BLK]
        dvc_ref[...] = dv[2 * BLK:]
        dsk = -jnp.sum(p_sink * delta.reshape(grp, BLK, 1), axis=1, keepdims=True)

        @pl.when(n == 0)
        def _():
            dkm_ref[...] = jnp.zeros_like(dkm_ref)
            dvm_ref[...] = jnp.zeros_like(dvm_ref)
            dsk_ref[...] = jnp.zeros_like(dsk_ref)

        dkm_ref[...] += dk[:BLK]
        dvm_ref[...] += dv[:BLK]
        dsk_ref[...] += jnp.broadcast_to(dsk, (grp, BLK, 1))

    qspec, kmeta, kprev, kcur, sink = _attn_specs(grp)
    kv_shape = _sds((nkv, tp, HEAD_DIM), F32)
    meta_shape = _sds((nkv, BLK, HEAD_DIM), F32)
    return _gcall(body, "attn_bwd", (nkv, tp // BLK), [qspec, kmeta, kprev, kcur, kmeta, kprev, kcur, sink, qspec],
                  [qspec, kcur, kcur, kmeta, kcur, kcur, kmeta, sink],
                  [_sds((nq, tp, HEAD_DIM), F32), kv_shape, kv_shape, meta_shape, kv_shape, kv_shape, meta_shape,
                   _sds((nkv, grp, BLK, 1), F32)],
                  (q_hm, k_hm, k_hm, k_hm, v_hm, v_hm, v_hm, sink4, do_hm), sem=("parallel", "arbitrary"), side=side)


def _pool_coef(row_blk, col_blk, w):
    r = lax.broadcasted_iota(jnp.int32, (BLK, BLK), 0)
    j = lax.broadcasted_iota(jnp.int32, (BLK, BLK), 1)
    t = row_blk * BLK + r - LEAD
    tj = col_blk * BLK + j - LEAD
    dist = t - tj
    inwin = (dist >= 0) & (dist < w) & (tj >= 0)
    count = jnp.maximum(jnp.minimum(t + 1, w), 1).astype(F32)
    return jnp.where(inwin, 1.0 / count, 0.0) - jnp.where((dist == 0) & (tj >= 0), 1.0, 0.0)


def _pool_fwd(proj, wg, scale, u_off, pool_w, side=None):
    tp = proj.shape[0]
    gw = pool_w // N_GRP
    nb = tp // BLK
    cb = u_off // gw

    def body(up_ref, uc_ref, wg_ref, sc_ref, pooled_ref, mx_ref, pm_ref):
        n, g = pl.program_id(0), pl.program_id(1)
        w = jnp.left_shift(2, g)
        pooled = (jnp.dot(_pool_coef(n, n - 1, w), up_ref[...], precision=HI, preferred_element_type=F32)
                  + jnp.dot(_pool_coef(n, n, w), uc_ref[...], precision=HI, preferred_element_type=F32))
        pb = pooled.astype(BF16)
        mx = jnp.dot(pb, wg_ref[...], preferred_element_type=F32)
        pooled_ref[...] = pb
        mx_ref[...] = mx
        pm_ref[...] = (mx * sc_ref[...]).astype(BF16)

    blk = pl.BlockSpec((BLK, gw), lambda n, g: (n, g))
    return _gcall(body, "pool_fwd", (nb, N_GRP),
                  [pl.BlockSpec((BLK, gw), lambda n, g: (jnp.maximum(n - 1, 0), cb + g)),
                   pl.BlockSpec((BLK, gw), lambda n, g: (n, cb + g)),
                   pl.BlockSpec((None, gw, gw), lambda n, g: (g, 0, 0)),
                   pl.BlockSpec((1, gw), lambda n, g: (0, g))],
                  [blk, blk, blk], [_sds((tp, pool_w), BF16), _sds((tp, pool_w), F32), _sds((tp, pool_w), BF16)],
                  (proj, proj, wg, scale), sem=("parallel", "parallel"), side=side)


def _pool_bwd_mix(d_pm, mx, pooled, wg, scale, side=None):
    tp, pool_w = d_pm.shape
    gw = pool_w // N_GRP

    def body(d_ref, mx_ref, pl_ref, wg_ref, sc_ref, dp_ref, dwg_ref, dsc_ref):
        n = pl.program_id(1)
        d = d_ref[...]
        dmx = (d * sc_ref[...]).astype(BF16)
        dp_ref[...] = lax.dot_general(dmx, wg_ref[...], _DOT_DIMS["nt"], preferred_element_type=F32)

        @pl.when(n == 0)
        def _():
            dwg_ref[...] = jnp.zeros_like(dwg_ref)
            dsc_ref[...] = jnp.zeros_like(dsc_ref)

        dwg_ref[...] += lax.dot_general(pl_ref[...], dmx, _DOT_DIMS["tn"], preferred_element_type=F32)
        dsc_ref[...] += jnp.sum(d * mx_ref[...], axis=0, keepdims=True)

    blk = pl.BlockSpec((BLK, gw), lambda g, n: (n, g))
    wspec = pl.BlockSpec((None, gw, gw), lambda g, n: (g, 0, 0))
    sspec = pl.BlockSpec((1, gw), lambda g, n: (0, g))
    return _gcall(body, "pool_bwd_mix", (N_GRP, tp // BLK), [blk, blk, blk, wspec, sspec], [blk, wspec, sspec],
                  [_sds((tp, pool_w), F32), _sds((N_GRP, gw, gw), F32), _sds((1, pool_w), F32)], (d_pm, mx, pooled, wg, scale),
                  sem=("parallel", "arbitrary"), side=side)


def _pool_bwd_band(dp, side=None):
    tp, pool_w = dp.shape
    gw = pool_w // N_GRP
    nb = tp // BLK

    def body(dc_ref, dn_ref, du_ref):
        n, g = pl.program_id(0), pl.program_id(1)
        w = jnp.left_shift(2, g)
        dnext = jnp.where(n + 1 < nb, dn_ref[...], 0.0)
        du = (lax.dot_general(_pool_coef(n, n, w), dc_ref[...], _DOT_DIMS["tn"], precision=HI, preferred_element_type=F32)
              + lax.dot_general(_pool_coef(n + 1, n, w), dnext, _DOT_DIMS["tn"], precision=HI, preferred_element_type=F32))
        du_ref[...] = du.astype(BF16)

    blk = pl.BlockSpec((BLK, gw), lambda n, g: (n, g))
    return _gcall(body, "pool_bwd_band", (nb, N_GRP), [blk, pl.BlockSpec((BLK, gw), lambda n, g: (jnp.minimum(n + 1, nb - 1), g))],
                  blk, _sds((tp, pool_w), BF16), (dp, dp), sem=("parallel", "parallel"), side=side)


def _gate_tiles(tp, d, g_off):
    tc = _pick(math.gcd(g_off, d), 512, 256, 128)
    tr = _pick(tp, 384, 128)
    return tr, tc


def _mix_fwd(proj, b_gate, a_out, p_out, g_off, side=None):
    tp, d = a_out.shape
    tr, tc = _gate_tiles(tp, d, g_off)
    c0, c1 = g_off // tc, (g_off + d) // tc

    def body(g0_ref, g1_ref, b_ref, a_ref, p_ref, o_ref):
        g0 = jax.nn.sigmoid(g0_ref[...] + b_ref[0:1, :])
        g1 = jax.nn.sigmoid(g1_ref[...] + b_ref[1:2, :])
        o_ref[...] = (g0 * a_ref[...] + g1 * p_ref[...]).astype(BF16)

    blk = pl.BlockSpec((tr, tc), lambda i, j: (i, j))
    return _gcall(body, "mix_fwd", (tp // tr, d // tc),
                  [pl.BlockSpec((tr, tc), lambda i, j: (i, c0 + j)), pl.BlockSpec((tr, tc), lambda i, j: (i, c1 + j)),
                   pl.BlockSpec((2, tc), lambda i, j: (0, j)), blk, blk],
                  blk, _sds((tp, d), BF16), (proj, proj, b_gate, a_out, p_out), sem=("parallel", "parallel"), side=side)


def _mix_bwd(proj, b_gate, a_out, p_out, d_mixed, g_off, side=None):
    tp, d = a_out.shape
    tr, tc = _gate_tiles(tp, d, g_off)
    c0, c1 = g_off // tc, (g_off + d) // tc

    def body(g0_ref, g1_ref, b_ref, a_ref, p_ref, d_ref, da_ref, dp_ref, dl0_ref, dl1_ref, db_ref):
        g0 = jax.nn.sigmoid(g0_ref[...] + b_ref[0:1, :])
        g1 = jax.nn.sigmoid(g1_ref[...] + b_ref[1:2, :])
        dm = d_ref[...]
        da_ref[...] = (dm * g0).astype(BF16)
        dp_ref[...] = (dm * g1).astype(BF16)
        dl0 = dm * a_ref[...] * g0 * (1.0 - g0)
        dl1 = dm * p_ref[...] * g1 * (1.0 - g1)
        dl0_ref[...] = dl0.astype(BF16)
        dl1_ref[...] = dl1.astype(BF16)

        @pl.when(pl.program_id(1) == 0)
        def _():
            db_ref[...] = jnp.zeros_like(db_ref)

        db_ref[...] += jnp.concatenate([jnp.sum(dl0, axis=0, keepdims=True), jnp.sum(dl1, axis=0, keepdims=True)], axis=0)

    blk = pl.BlockSpec((tr, tc), lambda j, i: (i, j))
    big = _sds((tp, d), BF16)
    return _gcall(body, "mix_bwd", (d // tc, tp // tr),
                  [pl.BlockSpec((tr, tc), lambda j, i: (i, c0 + j)), pl.BlockSpec((tr, tc), lambda j, i: (i, c1 + j)),
                   pl.BlockSpec((2, tc), lambda j, i: (0, j)), blk, blk, blk],
                  [blk, blk, blk, blk, pl.BlockSpec((2, tc), lambda j, i: (0, j))], [big, big, big, big, _sds((2, d), F32)],
                  (proj, proj, b_gate, a_out, p_out, d_mixed), sem=("parallel", "arbitrary"), side=side)


SWIGLU_ROWS = 64


def _swiglu_fwd(ff, side=None):
    tp, f2 = ff.shape
    f = f2 // 2
    tr = _pick(tp, SWIGLU_ROWS)

    def body(x_ref, o_ref):
        gate, up = x_ref[:, :f], x_ref[:, f:]
        o_ref[...] = (gate * jax.nn.sigmoid(gate) * up).astype(BF16)

    return _gcall(body, "swiglu_fwd", (tp // tr,), [pl.BlockSpec((tr, f2), lambda i: (i, 0))], pl.BlockSpec((tr, f), lambda i: (i, 0)),
                  _sds((tp, f), BF16), (ff,), sem=("parallel",), side=side)


def _swiglu_bwd(ff, d_act, side=None):
    tp, f2 = ff.shape
    f = f2 // 2
    tr = _pick(tp, SWIGLU_ROWS)

    def body(x_ref, d_ref, o_ref):
        gate, up = x_ref[:, :f], x_ref[:, f:]
        d = d_ref[...]
        sg = jax.nn.sigmoid(gate)
        silu = gate * sg
        o_ref[:, :f] = (d * up * (sg + silu * (1.0 - sg))).astype(BF16)
        o_ref[:, f:] = (d * silu).astype(BF16)

    return _gcall(body, "swiglu_bwd", (tp // tr,), [pl.BlockSpec((tr, f2), lambda i: (i, 0)), pl.BlockSpec((tr, f), lambda i: (i, 0))],
                  pl.BlockSpec((tr, f2), lambda i: (i, 0)), _sds((tp, f2), BF16), (ff, d_act), sem=("parallel",), side=side)


def _tile2(rows, cols, max_bytes=3 << 20):
    tc = _pick(cols, 1024, 640, 512)
    for tr in (512, 344, 256, 128, 64, 32, 16, 8):
        if rows % tr == 0 and tr * tc * 4 <= max_bytes:
            return tr, tc
    return rows, tc


def _cast_bf16(w, name, side=None):
    r, c = w.shape
    tr, tc = _tile2(r, c)

    def body(x_ref, o_ref):
        o_ref[...] = x_ref[...].astype(BF16)

    blk = pl.BlockSpec((tr, tc), lambda i, j: (i, j))
    return _gcall(body, name, (r // tr, c // tc), [blk], blk, _sds((r, c), BF16), (w,), sem=("parallel", "parallel"), side=side)


def _adamw(w, g, m, v, name, side=None):
    r, c = w.shape
    tr, tc = _tile2(r, c, 1 << 20)

    def body(w_ref, g_ref, m_ref, v_ref, d_ref, nm_ref, nv_ref):
        gg = g_ref[...]
        nm = ADAM_B1 * m_ref[...] + (1.0 - ADAM_B1) * gg
        nv = ADAM_B2 * v_ref[...] + (1.0 - ADAM_B2) * jnp.square(gg)
        m_hat = nm / (1.0 - ADAM_B1 ** ADAM_STEP)
        v_hat = nv / (1.0 - ADAM_B2 ** ADAM_STEP)
        d_ref[...] = -ADAM_LR * (m_hat / (jnp.sqrt(v_hat) + ADAM_EPS) + ADAM_WD * w_ref[...])
        nm_ref[...] = nm
        nv_ref[...] = nv

    blk = pl.BlockSpec((tr, tc), lambda i, j: (i, j))
    shp = _sds((r, c), F32)
    return _gcall(body, name, (r // tr, c // tc), [blk] * 4, [blk] * 3, [shp] * 3, (w, g, m, v), sem=("parallel", "parallel"), side=side)


def _piece_block_index(win, tr, tc):
    r, c = win.shard_shape
    if win.kind == "col":
        return lambda s, h, i, j: (win.row0 // tr + h * (win.half // tr) + i, s * (c // tc) + j)
    return lambda s, h, i, j: ((s * r + win.row0) // tr + h * (win.half // tr) + i, j)


def _chip_sum(g, who):
    pr, pc = g.win.piece_shape
    tr, tc = _tile2(pr, pc)
    full_idx = _piece_block_index(g.win, tr, tc)

    def body(who_ref, g_ref, o_ref, out_ref):
        out_ref[...] = (g_ref[...] + o_ref[...]).astype(BF16)

    slot = pl.BlockSpec((None, tr, tc), lambda s, i, j, who_ref: (s, i, j))
    g.chip = pl.pallas_call(
        body,
        name="chip_sum_" + g.key.replace("@", "_"),
        grid_spec=pltpu.PrefetchScalarGridSpec(
            num_scalar_prefetch=1,
            grid=(N_CHIPS, pr // tr, pc // tc),
            in_specs=[pl.BlockSpec((tr, tc), lambda s, i, j, who_ref: full_idx(s, who_ref[0], i, j)), slot],
            out_specs=slot,
        ),
        out_shape=_sds((N_CHIPS, pr, pc), BF16),
        compiler_params=_cparams(("parallel", "parallel", "parallel")),
    )(who, g.grad, g.other)


def _final_sum(g, shard, who):
    win = g.win
    pr, pc = win.piece_shape
    tr, tc = _tile2(pr, pc)
    full_idx = _piece_block_index(win, tr, tc)
    has_prev = shard.arr is not None

    def body(who_ref, g_ref, o_ref, l1_ref, l2_ref, l3_ref, *rest):
        out_ref = rest[-1]
        acc = g_ref[...] + o_ref[...]
        for l_ref in (l1_ref, l2_ref, l3_ref):
            acc = acc + l_ref[...].astype(F32)
        out_ref[...] = acc

    def landed_spec(k):
        return pl.BlockSpec((None, tr, tc), lambda i, j, who_ref: (who_ref[1 + k], i, j))

    in_specs = [pl.BlockSpec((tr, tc), lambda i, j, who_ref: full_idx(who_ref[1], who_ref[0], i, j)),
                pl.BlockSpec((None, tr, tc), lambda i, j, who_ref: (who_ref[1], i, j)),
                landed_spec(1), landed_spec(2), landed_spec(3)]
    args = [who, g.grad, g.other, g.landed, g.landed, g.landed]
    if has_prev:
        in_specs.append(ANY)
        args.append(shard.arr)
    shard.arr = pl.pallas_call(
        body,
        name="final_sum_" + g.key.replace("@", "_"),
        grid_spec=pltpu.PrefetchScalarGridSpec(
            num_scalar_prefetch=1,
            grid=(pr // tr, pc // tc),
            in_specs=in_specs,
            out_specs=pl.BlockSpec((tr, tc), lambda i, j, who_ref: (win.row0 // tr + who_ref[0] * (pr // tr) + i, j)),
        ),
        out_shape=_sds(win.shard_shape, F32),
        input_output_aliases={6: 0} if has_prev else {},
        compiler_params=_cparams(("parallel", "parallel")),
    )(*args)


def _gather_small(packed):
    r, c = packed.shape

    def body(in_ref, out_ref, send_sems, recv_sems):
        x, y, c_ = _coords()
        s_me = 2 * x + y
        out_ref[s_me] = in_ref[...]
        copies = []
        for j, (ox, oy) in enumerate(_other_chips(x, y)):
            cp = pltpu.make_async_remote_copy(src_ref=in_ref, dst_ref=out_ref.at[s_me], send_sem=send_sems.at[j],
                                              recv_sem=recv_sems.at[j], device_id=(ox, oy, c_), device_id_type=MESH)
            cp.start()
            copies.append(cp)
        for j, (ox, oy) in enumerate(_other_chips(x, y)):
            copies[j].wait_send()
            pltpu.make_async_remote_copy(src_ref=in_ref, dst_ref=out_ref.at[2 * ox + oy], send_sem=send_sems.at[j],
                                         recv_sem=recv_sems.at[j], device_id=(x, y, c_), device_id_type=MESH).wait_recv()

    return pl.pallas_call(
        body,
        name="gather_small",
        in_specs=[VMEM_FULL],
        out_specs=VMEM_FULL,
        out_shape=_sds((N_CHIPS, r, c), F32),
        scratch_shapes=[pltpu.SemaphoreType.DMA((3,)), pltpu.SemaphoreType.DMA((3,))],
    )(packed)


def _all_reduce_small(packed):
    r, c = packed.shape

    def body(in_ref, out_ref, slots, send_sems, recv_sems):
        x, y, c_ = _coords()
        me = 4 * x + 2 * y + c_
        slots[me] = in_ref[...]
        copies = []
        for k in range(1, N_DEV):
            peer = me ^ k
            cp = pltpu.make_async_remote_copy(src_ref=in_ref, dst_ref=slots.at[me], send_sem=send_sems.at[k - 1],
                                              recv_sem=recv_sems.at[k - 1],
                                              device_id=(peer // 4, (peer // 2) % 2, peer % 2), device_id_type=MESH)
            cp.start()
            copies.append(cp)
        for k in range(1, N_DEV):
            copies[k - 1].wait_send()
            pltpu.make_async_remote_copy(src_ref=in_ref, dst_ref=slots.at[me ^ k], send_sem=send_sems.at[k - 1],
                                         recv_sem=recv_sems.at[k - 1], device_id=(x, y, c_), device_id_type=MESH).wait_recv()
        acc = slots[0]
        for d in range(1, N_DEV):
            acc = acc + slots[d]
        out_ref[...] = acc

    return pl.pallas_call(
        body,
        name="all_reduce_small",
        in_specs=[VMEM_FULL],
        out_specs=VMEM_FULL,
        out_shape=_sds((r, c), F32),
        scratch_shapes=[pltpu.VMEM((N_DEV, r, c), F32), pltpu.SemaphoreType.DMA((N_DEV - 1,)), pltpu.SemaphoreType.DMA((N_DEV - 1,))],
    )(packed)


def _rows_of(a, width):
    flat = a.reshape(-1)
    n = -(-flat.shape[0] // width) * width
    return jnp.pad(flat, (0, n - flat.shape[0])).reshape(-1, width)


def _pad_rows(a, mult=8):
    n = -(-a.shape[0] // mult) * mult
    return jnp.pad(a, ((0, n - a.shape[0]), (0, 0)))


def _heads_major(a, nh):
    tp = a.shape[0]
    return a.reshape(tp, nh, HEAD_DIM).transpose(1, 0, 2)


def _heads_minor(a):
    nh, tp, hd = a.shape
    return a.transpose(1, 0, 2).reshape(tp, nh * hd)


def kernel(x, meta_tokens, ln_in_g, ln_in_b, w_in, b_gate, attn_sinks, w_attn_up, w_pool_grp, pool_scale, w_pool_up, w_out, ln1_g, ln1_b, w_ffn_in, w_ffn_down, ln2_g, ln2_b, loss_target, m_meta_tokens, m_ln_in_g, m_ln_in_b, m_w_in, m_b_gate, m_attn_sinks, m_w_attn_up, m_w_pool_grp, m_pool_scale, m_w_pool_up, m_w_out, m_ln1_g, m_ln1_b, m_w_ffn_in, m_w_ffn_down, m_ln2_g, m_ln2_b, v_meta_tokens, v_ln_in_g, v_ln_in_b, v_w_in, v_b_gate, v_attn_sinks, v_w_attn_up, v_w_pool_grp, v_pool_scale, v_w_pool_up, v_w_out, v_ln1_g, v_ln1_b, v_w_ffn_in, v_w_ffn_down, v_ln2_g, v_ln2_b):
    seq, d = x.shape[1], x.shape[2]
    tp = LEAD + N_META + seq
    nb = tp // BLK
    nq = attn_sinks.shape[1]
    grp = nq // N_KV
    attn_w = nq * HEAD_DIM
    kv_w = N_KV * HEAD_DIM
    qkv_w = attn_w + 2 * kv_w
    pool_w = pool_scale.shape[1]
    gw = pool_w // N_GRP
    g_off = qkv_w + pool_w
    dc = d // N_CHIPS
    cx, cy, cc = _coords()
    s_me = 2 * cx + cy
    who = jnp.stack([cc, s_me, (s_me + 1) % N_CHIPS, (s_me + 2) % N_CHIPS, (s_me + 3) % N_CHIPS]).astype(jnp.int32)

    names = ["w_in", "w_attn_up", "w_pool_grp", "w_pool_up", "w_out", "w_ffn_in", "w_ffn_down"]
    kinds = dict(w_in="col", w_attn_up="col", w_pool_grp="row", w_pool_up="col", w_out="row", w_ffn_in="col", w_ffn_down="row")
    grp_shard = (N_GRP * (gw // N_CHIPS), gw)
    big_w = dict(w_in=w_in[0], w_attn_up=w_attn_up[0], w_pool_grp=w_pool_grp[0].reshape(grp_shard), w_pool_up=w_pool_up[0],
                 w_out=w_out[0], w_ffn_in=w_ffn_in[0], w_ffn_down=w_ffn_down[0])
    big_m = dict(w_in=m_w_in[0], w_attn_up=m_w_attn_up[0], w_pool_grp=m_w_pool_grp[0].reshape(grp_shard), w_pool_up=m_w_pool_up[0],
                 w_out=m_w_out[0], w_ffn_in=m_w_ffn_in[0], w_ffn_down=m_w_ffn_down[0])
    big_v = dict(w_in=v_w_in[0], w_attn_up=v_w_attn_up[0], w_pool_grp=v_w_pool_grp[0].reshape(grp_shard), w_pool_up=v_w_pool_up[0],
                 w_out=v_w_out[0], w_ffn_in=v_w_ffn_in[0], w_ffn_down=v_w_ffn_down[0])
    small_rows = _pad_rows(jnp.concatenate([meta_tokens, b_gate[0]], axis=0))
    gathered = _gather_small(small_rows)
    gathered = gathered.transpose(1, 0, 2).reshape(small_rows.shape[0], d)
    meta_full, b_gate_full = gathered[:N_META], gathered[N_META:N_META + 2]

    W = {}

    def cast(n, side=None):
        W[n] = _Weight(n, kinds[n], _cast_bf16(big_w[n], "cast_" + n, side=side))

    def whole(*ns):
        return [(W[n], W[n].win) for n in ns]

    def legs(first=(), second=(), third=()):
        ops = [_GatherD2d(third, DIAGONAL)] if third else []
        ops += [_GatherRing(second), _GatherD2d(second, NEIGHBOURS)] if second else []
        ops += [_GatherIci(first)] if first else []
        return _Side(ops)

    mid = ("w_attn_up", "w_pool_grp", "w_pool_up", "w_out")
    for n in ("w_in",) + mid:
        cast(n)
    cast("w_ffn_in", side=legs(first=whole("w_in")))
    cast("w_ffn_down", side=legs(first=whole(*mid), second=whole("w_in")))
    wins = {n: W[n].win for n in names}
    ffn_in_parts = [(W["w_ffn_in"], win) for win in W["w_ffn_in"].win.split(4)]
    x2d, tgt2d = x[0], loss_target[0]
    g_in, b_in = ln_in_g.reshape(1, d), ln_in_b.reshape(1, d)
    h0, h0b = _ln_in_fwd(x2d, meta_full, g_in, b_in, nb, side=legs(first=ffn_in_parts[0:1], second=whole(*mid), third=whole("w_in")))
    proj = _mm(h0b, W["w_in"].full, "nn", F32, 1408, 512, 4096, "mm_proj",
               side=legs(first=ffn_in_parts[1:4], second=ffn_in_parts[0:1], third=whole(*mid)))
    cos, sin = _rope_tables(tp)
    n_rot = (attn_w + kv_w) // LANES
    qkv = _rope_fwd(proj, cos, sin, n_rot, qkv_w)
    q_hm = _heads_major(qkv[:, :attn_w], nq)
    k_hm = _heads_major(qkv[:, attn_w:attn_w + kv_w], N_KV)
    v_hm = _heads_major(qkv[:, attn_w + kv_w:], N_KV)
    sink4 = jnp.broadcast_to(attn_sinks.reshape(N_KV, grp, 1, 1), (N_KV, grp, BLK, 1))
    o_hm = _attn_fwd(q_hm, k_hm, v_hm, sink4, side=legs(first=whole("w_ffn_down"), second=ffn_in_parts[1:4], third=ffn_in_parts[0:1]))
    o = _heads_minor(o_hm)
    wf_grp = W["w_pool_grp"].full.reshape(N_CHIPS, N_GRP, gw // N_CHIPS, gw).transpose(1, 0, 2, 3).reshape(N_GRP, gw, gw)
    pooled, mx, pm = _pool_fwd(proj, wf_grp, pool_scale, qkv_w, pool_w, side=legs(second=whole("w_ffn_down"), third=ffn_in_parts[1:4]))
    a_out = _mm(o, W["w_attn_up"].full, "nn", F32, 1408, 1024, 2048, "mm_attn_up", side=legs(third=whole("w_ffn_down")))
    p_out = _mm(pm, W["w_pool_up"].full, "nn", F32, 1408, 1024, 2048, "mm_pool_up")
    mixed = _mix_fwd(proj, b_gate_full, a_out, p_out, g_off)
    z1 = _mm(mixed, W["w_out"].full, "nn", F32, 1408, 512, 4096, "mm_out")
    r1, h1, h1b = _res_ln_fwd(h0, z1, ln1_g, ln1_b)
    ff = _mm(h1b, W["w_ffn_in"].full, "nn", F32, 1408, 512, 4096, "mm_ffn_in")
    act = _swiglu_fwd(ff)
    wf_down = W["w_ffn_down"].full
    z2 = _mm(act, wf_down, "nn", F32, 704, 1024, 5504, "mm_ffn_down")
    d_r2, d_r2b, loss_tile, dg2, db2 = _final_ln_loss(h1, z2, ln2_g, ln2_b, tgt2d)

    S = {n: _Shard(n) for n in names}

    def grads_of(name, grad, parts=1):
        return [_Grad(name, win, grad) for win in wins[name].split(parts)]

    def sibling(gs):
        return _ReduceSibling(gs)

    def chips(gs):
        for g in gs:
            _chip_sum(g, who)
        return _ReduceChips(gs)

    def share(gs):
        for g in gs:
            _final_sum(g, S[g.name], who)
        return _ShareReduced([(S[g.name], g.win) for g in gs])

    g6 = grads_of("w_ffn_down", _mm(act, d_r2b, "tn", F32, 256, 1024, tp, "mm_gw_ffn_down", j_outer=True))
    d_act = _mm(d_r2b, wf_down, "nt", F32, 1408, 256, 4096, "mm_d_act", side=_Side([sibling(g6)]))
    d_ff = _swiglu_bwd(ff, d_act)
    g5 = grads_of("w_ffn_in", _mm(h1b, d_ff, "tn", F32, 1024, 512, tp, "mm_gw_ffn_in", side=_Side([chips(g6)])), parts=2)
    d_h1_mm = _mm(d_ff, W["w_ffn_in"].full, "nt", F32, 704, 1024, 5504, "mm_d_h1", side=_Side([share(g6), sibling(g5)]))
    d_r1, d_r1b, dg1, db1 = _ln1_bwd(d_r2, d_h1_mm, r1, ln1_g)
    g4 = grads_of("w_out", _mm(mixed, d_r1b, "tn", F32, 1024, 512, tp, "mm_gw_out"))
    d_mixed = _mm(d_r1b, W["w_out"].full, "nt", F32, 1408, 512, 4096, "mm_d_mixed", side=_Side([sibling(g4)]))
    d_a, d_p, d_gl0, d_gl1, d_bgate = _mix_bwd(proj, b_gate_full, a_out, p_out, d_mixed, g_off, side=_Side([chips(g4)]))
    g1 = grads_of("w_attn_up", _mm(o, d_a, "tn", F32, 1024, 512, tp, "mm_gw_attn_up", side=_Side([share(g4)])))
    d_o = _mm(d_a, W["w_attn_up"].full, "nt", F32, 1408, 512, 4096, "mm_d_o", side=_Side([sibling(g1)]))
    g3 = grads_of("w_pool_up", _mm(pm, d_p, "tn", F32, 1024, 512, tp, "mm_gw_pool_up", side=_Side([chips(g1)])))
    d_pm = _mm(d_p, W["w_pool_up"].full, "nt", F32, 1408, 512, 4096, "mm_d_pm", side=_Side([sibling(g3)]))
    d_pooled, gw_grp, d_scale = _pool_bwd_mix(d_pm, mx, pooled, wf_grp, pool_scale, side=_Side([chips(g3), share(g1)]))
    gw_grp_sm = gw_grp.reshape(N_GRP, N_CHIPS, gw // N_CHIPS, gw).transpose(1, 0, 2, 3).reshape(N_CHIPS * grp_shard[0], gw)
    g2 = grads_of("w_pool_grp", gw_grp_sm)
    d_u = _pool_bwd_band(d_pooled, side=_Side([sibling(g2), share(g3)]))
    dq_hm, dk_cur, dk_prev, dk_meta, dv_cur, dv_prev, dv_meta, d_sink = _attn_bwd(
        q_hm, k_hm, v_hm, sink4, _heads_major(d_o, nq), side=_Side([chips(g5[0:1] + g2)]))
    d_qkv = _rope_bwd(_heads_minor(dq_hm), _heads_minor(dk_cur), _heads_minor(dk_prev), _heads_minor(dk_meta),
                      _heads_minor(dv_cur), _heads_minor(dv_prev), _heads_minor(dv_meta), cos, sin, side=_Side([share(g2)]))
    d_proj = jnp.concatenate([d_qkv, d_u, d_gl0, d_gl1], axis=1)
    g0 = grads_of("w_in", _mm(h0b, d_proj, "tn", F32, 1024, 512, tp, "mm_gw_in", side=_Side([chips(g5[1:2]), share(g5[0:1])])), parts=2)
    d_h0_mm = _mm(d_proj, W["w_in"].full, "nt", F32, 1408, 1024, 2560, "mm_d_h0", side=_Side([sibling(g0), share(g5[1:2])]))
    grad_x2d, d_meta, dg_in, db_in = _ln_in_bwd(d_r1, d_h0_mm, x2d, meta_full, g_in)

    small_parts = [d_meta, d_bgate, dg_in, db_in, dg1, db1, dg2, db2, _rows_of(d_scale, d), _rows_of(d_sink[:, :, 0, 0], d)]
    offs = [0]
    for p in small_parts:
        offs.append(offs[-1] + p.shape[0])
    red = _all_reduce_small(_pad_rows(jnp.concatenate(small_parts, axis=0)))
    r_meta, r_bgate, r_g_in, r_b_in, r_g1, r_b1, r_g2, r_b2, r_scale, r_sink = [red[offs[k]:offs[k + 1]] for k in range(len(small_parts))]
    col0 = s_me * dc
    g_meta = lax.dynamic_slice(r_meta, (0, col0), (N_META, dc))
    g_bgate = lax.dynamic_slice(r_bgate, (0, col0), (2, dc))
    g_scale = r_scale.reshape(-1)[:pool_w]
    g_sink = r_sink.reshape(-1)[:nq]

    upd = {}

    def adamw(n, side=None):
        upd[n] = _adamw(big_w[n], S[n].arr, big_m[n], big_v[n], "adamw_" + n, side=side)

    adamw("w_ffn_in", side=_Side([chips(g0[0:1])]))
    adamw("w_ffn_down", side=_Side([chips(g0[1:2]), share(g0[0:1])]))
    adamw("w_out", side=_Side([share(g0[1:2])]))
    for n in ("w_attn_up", "w_pool_grp", "w_pool_up", "w_in"):
        adamw(n)

    small_w = [meta_tokens, b_gate[0], ln_in_g, ln_in_b, attn_sinks, pool_scale, ln1_g, ln1_b, ln2_g, ln2_b]
    small_m = [m_meta_tokens, m_b_gate[0], m_ln_in_g, m_ln_in_b, m_attn_sinks, m_pool_scale, m_ln1_g, m_ln1_b, m_ln2_g, m_ln2_b]
    small_v = [v_meta_tokens, v_b_gate[0], v_ln_in_g, v_ln_in_b, v_attn_sinks, v_pool_scale, v_ln1_g, v_ln1_b, v_ln2_g, v_ln2_b]
    small_g = [g_meta, g_bgate, r_g_in, r_b_in, g_sink, g_scale, r_g1, r_b1, r_g2, r_b2]
    small_g = [g.reshape(w.shape) for g, w in zip(small_g, small_w)]

    def pack(parts):
        return _pad_rows(jnp.concatenate([_rows_of(p, dc) for p in parts], axis=0))

    s_delta, s_m, s_v = _adamw(pack(small_w), pack(small_g), pack(small_m), pack(small_v), "adamw_small")

    def unpack(packed):
        out, row = [], 0
        for w in small_w:
            nrow = -(-w.size // dc)
            out.append(packed[row:row + nrow].reshape(-1)[:w.size].reshape(w.shape))
            row += nrow
        return out

    s_delta, s_m, s_v = unpack(s_delta), unpack(s_m), unpack(s_v)

    order = ["meta_tokens", "ln_in_g", "ln_in_b", "w_in", "b_gate", "attn_sinks", "w_attn_up", "w_pool_grp", "pool_scale",
             "w_pool_up", "w_out", "ln1_g", "ln1_b", "w_ffn_in", "w_ffn_down", "ln2_g", "ln2_b"]
    small_names = ["meta_tokens", "b_gate", "ln_in_g", "ln_in_b", "attn_sinks", "pool_scale", "ln1_g", "ln1_b", "ln2_g", "ln2_b"]
    out_shapes = dict(meta_tokens=meta_tokens.shape, ln_in_g=ln_in_g.shape, ln_in_b=ln_in_b.shape, w_in=w_in.shape, b_gate=b_gate.shape,
                      attn_sinks=attn_sinks.shape, w_attn_up=w_attn_up.shape, w_pool_grp=w_pool_grp.shape, pool_scale=pool_scale.shape,
                      w_pool_up=w_pool_up.shape, w_out=w_out.shape, ln1_g=ln1_g.shape, ln1_b=ln1_b.shape, w_ffn_in=w_ffn_in.shape,
                      w_ffn_down=w_ffn_down.shape, ln2_g=ln2_g.shape, ln2_b=ln2_b.shape)
    grads, deltas, new_m, new_v = {}, {}, {}, {}
    for n in names:
        grads[n], (deltas[n], new_m[n], new_v[n]) = S[n].arr, upd[n]
    for k, n in enumerate(small_names):
        grads[n], deltas[n], new_m[n], new_v[n] = small_g[k], s_delta[k], s_m[k], s_v[k]

    loss = lax.psum(loss_tile[0, 0], ("x", "y", "c"))
    outs = [loss, grad_x2d.reshape(x.shape)]
    for group in (grads, deltas, new_m, new_v):
        outs += [group[n].reshape(out_shapes[n]) for n in order]
    return tuple(outs)
```

```python
import functools
import math

import jax
import jax.numpy as jnp
from jax import lax
from jax.experimental import pallas as pl
from jax.experimental.pallas import tpu as pltpu

F32 = jnp.float32
BF16 = jnp.bfloat16
MESH = pl.DeviceIdType.MESH
ANY = pl.BlockSpec(memory_space=pl.ANY)
VMEM_FULL = pl.BlockSpec(memory_space=pltpu.VMEM)

N_META = 16
HEAD_DIM = 64
N_KV = 4
BLK = 128
LEAD = (-N_META) % BLK
ROPE_DIM = HEAD_DIM // 4
ROPE_THETA = 500000.0
NEG_INF = -1e30
POOL_WINDOWS = (2, 4, 8, 16)
N_GRP = len(POOL_WINDOWS)
LN_EPS = 1e-5
DN_ALPHA = 2.0 ** 0.25
ADAM_LR = 0.001
ADAM_B1 = 0.9
ADAM_B2 = 0.999
ADAM_EPS = 1e-08
ADAM_WD = 0.01
ADAM_STEP = 10
N_CHIPS = 4
N_DEV = 8
LANES = 128
VMEM_LIMIT_MB = 56
HI = lax.Precision.HIGHEST


def _cparams(sem=None, vmem_mb=VMEM_LIMIT_MB):
    kw = dict(vmem_limit_bytes=vmem_mb << 20)
    if sem is not None:
        kw["dimension_semantics"] = sem
    return pltpu.CompilerParams(**kw)


def _pick(dim, *cands):
    for c in cands:
        if c <= dim and dim % c == 0:
            return c
    return dim


def _sds(shape, dtype):
    return jax.ShapeDtypeStruct(tuple(shape), dtype)


def _coords():
    return lax.axis_index("x"), lax.axis_index("y"), lax.axis_index("c")


def _other_chips(x, y):
    return [(1 - x, y), (x, 1 - y), (1 - x, 1 - y)]


class _Win:
    def __init__(self, kind, shard_shape, row0=0, nrows=None):
        self.kind, self.shard_shape, self.row0 = kind, tuple(shard_shape), row0
        self.nrows = shard_shape[0] if nrows is None else nrows
        self.half = self.nrows // 2

    @property
    def piece_shape(self):
        return (self.half, self.shard_shape[1])

    @property
    def full_shape(self):
        r, c = self.shard_shape
        return (r, N_CHIPS * c) if self.kind == "col" else (N_CHIPS * r, c)

    def in_full(self, ref, s, h, q=None):
        r, c = self.shard_shape
        start, size = self.row0 + h * self.half, self.half
        if q is not None:
            start, size = start + q * (self.half // 2), self.half // 2
        if self.kind == "col":
            return ref.at[pl.ds(start, size), pl.ds(s * c, c)]
        return ref.at[pl.ds(s * r + start, size), :]

    def in_shard(self, ref, h):
        return ref.at[pl.ds(self.row0 + h * self.half, self.half), :]

    def split(self, n):
        return [_Win(self.kind, self.shard_shape, self.row0 + q * (self.nrows // n), self.nrows // n) for q in range(n)]


class _Weight:
    def __init__(self, name, win, full):
        self.name, self.win, self.full = name, win, full


class _Grad:
    def __init__(self, name, win, grad):
        self.name, self.win, self.grad = name, win, grad
        self.key = "%s@%d" % (name, win.row0)
        self.other = self.chip = self.landed = None


class _Shard:
    def __init__(self, name):
        self.name, self.arr = name, None


COPY_STREAMS = 8
BF16_ROWS = 16


def _stream_views(ref):
    rows, cols = ref.shape
    n = COPY_STREAMS
    if rows % (n * BF16_ROWS) == 0:
        return [ref.at[pl.ds(i * (rows // n), rows // n), :] for i in range(n)]
    if cols % (n * LANES) == 0:
        return [ref.at[:, pl.ds(i * (cols // n), cols // n)] for i in range(n)]
    return [ref]


class _StreamedCopy:
    def __init__(self, make, src, dst):
        self.whole = make(src, dst)
        self.parts = [make(s, d) for s, d in zip(_stream_views(src), _stream_views(dst))]

    def start(self):
        for cp in self.parts:
            cp.start()

    def wait(self):
        self.whole.wait()

    def wait_send(self):
        self.whole.wait_send()

    def wait_recv(self):
        self.whole.wait_recv()


class _Ctx:
    def __init__(self, side, in_refs, out_refs, send_sems, recv_sems, local_sems, base, lbase):
        self.side, self.in_refs, self.out_refs = side, in_refs, out_refs
        self.send_sems, self.recv_sems, self.local_sems, self.base, self.lbase = send_sems, recv_sems, local_sems, base, lbase

    def ref(self, key):
        info = self.side.info[key]
        return self.in_refs[info["in"]] if info["in"] is not None else self.out_refs[info["out"]]

    def remote(self, k, src, dst, to):
        def make(s, d):
            return pltpu.make_async_remote_copy(src_ref=s, dst_ref=d, send_sem=self.send_sems.at[self.base + k],
                                                recv_sem=self.recv_sems.at[self.base + k], device_id=to, device_id_type=MESH)

        return _StreamedCopy(make, src, dst)


class _Side:
    def __init__(self, ops):
        self.ops, self.info, self.keys = ops, {}, []
        self.nsem = self.nlocal = 0
        self.bases = []
        for op in ops:
            op.register(self)
            self.bases.append((self.nsem, self.nlocal))
            self.nsem += op.nsem
            self.nlocal += op.nlocal
        self.inputs, self.out_shape, self.aliases = [], [], {}
        for key in self.keys:
            info = self.info[key]
            info["in"] = info["out"] = None
            if info["arr"] is not None:
                info["in"] = len(self.inputs)
                self.inputs.append(info["arr"])
            if info["write"]:
                info["out"] = len(self.out_shape)
                self.out_shape.append(info["sds"])
                if info["in"] is not None:
                    self.aliases[info["in"]] = info["out"]

    def need(self, key, arr=None, sds=None, write=False):
        if key not in self.info:
            self.keys.append(key)
            self.info[key] = dict(arr=arr, sds=sds if arr is None else _sds(arr.shape, arr.dtype), write=write)
        else:
            self.info[key]["write"] = self.info[key]["write"] or write

    def _ctx(self, k, in_refs, out_refs, sems):
        return _Ctx(self, in_refs, out_refs, sems[0], sems[1], sems[2], *self.bases[k])

    def start(self, in_refs, out_refs, sems):
        for k, op in enumerate(self.ops):
            op.start(self._ctx(k, in_refs, out_refs, sems))

    def finish(self, in_refs, out_refs, sems):
        for k, op in enumerate(self.ops):
            op.finish(self._ctx(k, in_refs, out_refs, sems))

    def scratch(self):
        return [pltpu.SemaphoreType.DMA((max(self.nsem, 1),)), pltpu.SemaphoreType.DMA((max(self.nsem, 1),)),
                pltpu.SemaphoreType.DMA((max(self.nlocal, 1),))]

    def commit(self, outs):
        res = {key: outs[self.info[key]["out"]] for key in self.keys if self.info[key]["write"]}
        for op in self.ops:
            op.commit(res)


NEIGHBOURS, DIAGONAL = (0, 1), (2,)


class _GatherIci:
    def __init__(self, pairs):
        self.pairs = pairs
        self.nsem, self.nlocal = 2 * len(pairs), 0

    def register(self, side):
        for w, win in self.pairs:
            side.need(("full", w.name), arr=w.full, write=True)

    def _copies(self, ctx):
        x, y, c = _coords()
        s_me = 2 * x + y
        sends, recvs = [], []
        for t, (w, win) in enumerate(self.pairs):
            full = ctx.ref(("full", w.name))
            for j in NEIGHBOURS:
                ox, oy = _other_chips(x, y)[j]
                mine, landing = win.in_full(full, s_me, c), win.in_full(full, 2 * ox + oy, c)
                sends.append(ctx.remote(2 * t + j, mine, mine, (ox, oy, c)))
                recvs.append(ctx.remote(2 * t + j, landing, landing, (x, y, c)))
        return sends, recvs

    def start(self, ctx):
        for cp in self._copies(ctx)[0]:
            cp.start()

    def finish(self, ctx):
        sends, recvs = self._copies(ctx)
        for cp in recvs:
            cp.wait_recv()
        for cp in sends:
            cp.wait_send()

    def commit(self, res):
        for w, _ in self.pairs:
            w.full = res[("full", w.name)]


class _GatherRing:
    def __init__(self, pairs):
        self.pairs = pairs
        self.nsem, self.nlocal = 2 * len(pairs), 0

    def register(self, side):
        for w, win in self.pairs:
            side.need(("full", w.name), arr=w.full, write=True)

    def _copies(self, ctx):
        x, y, c = _coords()
        s_x, s_y, s_d = 2 * (1 - x) + y, 2 * x + (1 - y), 2 * (1 - x) + (1 - y)
        sends, recvs = [], []
        for t, (w, win) in enumerate(self.pairs):
            full = ctx.ref(("full", w.name))
            for q, (s_from, to) in enumerate([(s_x, (x, 1 - y, c)), (s_y, (1 - x, y, c))]):
                passed, landing = win.in_full(full, s_from, c, q), win.in_full(full, s_d, c, q)
                sends.append(ctx.remote(2 * t + q, passed, passed, to))
                recvs.append(ctx.remote(2 * t + q, landing, landing, (x, y, c)))
        return sends, recvs

    def start(self, ctx):
        for cp in self._copies(ctx)[0]:
            cp.start()

    def finish(self, ctx):
        sends, recvs = self._copies(ctx)
        for cp in recvs:
            cp.wait_recv()
        for cp in sends:
            cp.wait_send()

    def commit(self, res):
        for w, _ in self.pairs:
            w.full = res[("full", w.name)]


class _GatherD2d:
    def __init__(self, pairs, which):
        self.pairs, self.which = pairs, which
        self.nsem, self.nlocal = 3 * len(pairs), 0

    def register(self, side):
        for w, win in self.pairs:
            side.need(("full", w.name), arr=w.full, write=True)

    def _copies(self, ctx):
        x, y, c = _coords()
        sends, recvs = [], []
        for t, (w, win) in enumerate(self.pairs):
            full = ctx.ref(("full", w.name))
            for j in self.which:
                ox, oy = _other_chips(x, y)[j]
                mine, theirs = win.in_full(full, 2 * ox + oy, c), win.in_full(full, 2 * ox + oy, 1 - c)
                sends.append(ctx.remote(3 * t + j, mine, mine, (x, y, 1 - c)))
                recvs.append(ctx.remote(3 * t + j, theirs, theirs, (x, y, c)))
        return sends, recvs

    def start(self, ctx):
        for cp in self._copies(ctx)[0]:
            cp.start()

    def finish(self, ctx):
        sends, recvs = self._copies(ctx)
        for cp in recvs:
            cp.wait_recv()
        for cp in sends:
            cp.wait_send()

    def commit(self, res):
        for w, _ in self.pairs:
            w.full = res[("full", w.name)]


class _ReduceSibling:
    def __init__(self, grads):
        self.grads = grads
        self.nsem, self.nlocal = N_CHIPS * len(grads), 0

    def register(self, side):
        for g in self.grads:
            side.need(("grad", g.name), arr=g.grad)
            side.need(("other", g.key), sds=_sds((N_CHIPS,) + g.win.piece_shape, F32), write=True)

    def _copies(self, ctx):
        x, y, c = _coords()
        out = []
        for t, g in enumerate(self.grads):
            grad, other = ctx.ref(("grad", g.name)), ctx.ref(("other", g.key))
            for s in range(N_CHIPS):
                out.append(ctx.remote(N_CHIPS * t + s, g.win.in_full(grad, s, 1 - c), other.at[s], (x, y, 1 - c)))
        return out

    def start(self, ctx):
        for cp in self._copies(ctx):
            cp.start()

    def finish(self, ctx):
        for cp in self._copies(ctx):
            cp.wait()

    def commit(self, res):
        for g in self.grads:
            g.other = res[("other", g.key)]


class _ReduceChips:
    def __init__(self, grads):
        self.grads = grads
        self.nsem, self.nlocal = 3 * len(grads), 0

    def register(self, side):
        for g in self.grads:
            side.need(("chip", g.key), arr=g.chip)
            side.need(("landed", g.key), sds=_sds(g.chip.shape, g.chip.dtype), write=True)

    def _copies(self, ctx):
        x, y, c = _coords()
        s_me = 2 * x + y
        out = []
        for t, g in enumerate(self.grads):
            chip, landed = ctx.ref(("chip", g.key)), ctx.ref(("landed", g.key))
            for j, (ox, oy) in enumerate(_other_chips(x, y)):
                out.append(ctx.remote(3 * t + j, chip.at[2 * ox + oy], landed.at[s_me], (ox, oy, c)))
        return out

    def start(self, ctx):
        for cp in self._copies(ctx):
            cp.start()

    def finish(self, ctx):
        for cp in self._copies(ctx):
            cp.wait()

    def commit(self, res):
        for g in self.grads:
            g.landed = res[("landed", g.key)]


class _ShareReduced:
    def __init__(self, items):
        self.items = items
        self.nsem, self.nlocal = len(items), 0

    def register(self, side):
        for sh, _ in self.items:
            side.need(("reduced", sh.name), arr=sh.arr, write=True)

    def _copies(self, ctx):
        x, y, c = _coords()
        sends, recvs = [], []
        for t, (sh, win) in enumerate(self.items):
            ref = ctx.ref(("reduced", sh.name))
            sends.append(ctx.remote(t, win.in_shard(ref, c), win.in_shard(ref, c), (x, y, 1 - c)))
            recvs.append(ctx.remote(t, win.in_shard(ref, 1 - c), win.in_shard(ref, 1 - c), (x, y, c)))
        return sends, recvs

    def start(self, ctx):
        for cp in self._copies(ctx)[0]:
            cp.start()

    def finish(self, ctx):
        sends, recvs = self._copies(ctx)
        for cp in recvs:
            cp.wait_recv()
        for cp in sends:
            cp.wait_send()

    def commit(self, res):
        for sh, _ in self.items:
            sh.arr = res[("reduced", sh.name)]


def _gcall(body, name, grid, in_specs, out_specs, out_shape, args, scratch=(), sem=None, side=None, prefetch=()):
    single = not isinstance(out_shape, (list, tuple))
    out_shapes = [out_shape] if single else list(out_shape)
    out_specs = [out_specs] if single else list(out_specs)
    hosted = side is not None and bool(side.ops)
    n_pf, n_in, n_out, n_scr = len(prefetch), len(args), len(out_shapes), len(scratch)
    ns_in, ns_out = (len(side.inputs), len(side.out_shape)) if hosted else (0, 0)

    def wrapped(*refs):
        pf, refs = refs[:n_pf], refs[n_pf:]
        a, si = refs[:n_in], refs[n_in:n_in + ns_in]
        o = refs[n_in + ns_in:n_in + ns_in + n_out]
        so = refs[n_in + ns_in + n_out:n_in + ns_in + n_out + ns_out]
        rest = refs[n_in + ns_in + n_out + ns_out:]
        scr, sems = rest[:n_scr], rest[n_scr:]
        ids = [pl.program_id(k) for k in range(len(grid))]
        first = functools.reduce(jnp.logical_and, [i == 0 for i in ids])
        last = functools.reduce(jnp.logical_and, [i == g - 1 for i, g in zip(ids, grid)])

        @pl.when(first)
        def _():
            side.start(si, so, sems)

        body(*pf, *a, *o, *scr)

        @pl.when(last)
        def _():
            side.finish(si, so, sems)

    res = pl.pallas_call(
        wrapped if hosted else body, name=name,
        grid_spec=pltpu.PrefetchScalarGridSpec(
            num_scalar_prefetch=n_pf, grid=grid,
            in_specs=list(in_specs) + [ANY] * ns_in,
            out_specs=out_specs + [ANY] * ns_out,
            scratch_shapes=list(scratch) + (side.scratch() if hosted else []),
        ),
        out_shape=out_shapes + (side.out_shape if hosted else []),
        input_output_aliases={n_pf + n_in + i: n_out + j for i, j in side.aliases.items()} if hosted else {},
        compiler_params=_cparams(("arbitrary",) * len(grid) if hosted else sem),
    )(*prefetch, *args, *(side.inputs if hosted else []))
    if hosted:
        side.commit(res[n_out:])
    return res[0] if single else res[:n_out]


_DOT_DIMS = {
    "nn": (((1,), (0,)), ((), ())),
    "nt": (((1,), (1,)), ((), ())),
    "tn": (((0,), (0,)), ((), ())),
}


def _mm(a, b, mode, out_dtype, tm, tn, tk, name, j_outer=False, side=None):
    if mode == "nn":
        (m, k), n = a.shape, b.shape[1]
    elif mode == "nt":
        (m, k), n = a.shape, b.shape[0]
    else:
        (k, m), n = a.shape, b.shape[1]
    tm, tn, tk = _pick(m, tm), _pick(n, tn), _pick(k, tk)
    gi, gj, gk = m // tm, n // tn, k // tk
    dims = _DOT_DIMS[mode]

    def ij(g0, g1):
        return (g1, g0) if j_outer else (g0, g1)

    if mode == "tn":
        a_spec = pl.BlockSpec((tk, tm), lambda g0, g1, kk: (kk, ij(g0, g1)[0]))
    else:
        a_spec = pl.BlockSpec((tm, tk), lambda g0, g1, kk: (ij(g0, g1)[0], kk))
    if mode == "nt":
        b_spec = pl.BlockSpec((tn, tk), lambda g0, g1, kk: (ij(g0, g1)[1], kk))
    else:
        b_spec = pl.BlockSpec((tk, tn), lambda g0, g1, kk: (kk, ij(g0, g1)[1]))
    o_spec = pl.BlockSpec((tm, tn), lambda g0, g1, kk: ij(g0, g1))

    def body(a_ref, b_ref, o_ref, *scr):
        p = lax.dot_general(a_ref[...], b_ref[...], dims, preferred_element_type=F32)
        if gk == 1:
            o_ref[...] = p.astype(out_dtype)
        else:
            acc = scr[0]
            kk = pl.program_id(2)

            @pl.when(kk == 0)
            def _():
                acc[...] = p

            @pl.when(kk > 0)
            def _():
                acc[...] += p

            @pl.when(kk == gk - 1)
            def _():
                o_ref[...] = acc[...].astype(out_dtype)

    return _gcall(body, name, (gj, gi, gk) if j_outer else (gi, gj, gk), [a_spec, b_spec], o_spec, _sds((m, n), out_dtype), (a, b),
                  scratch=[pltpu.VMEM((tm, tn), F32)] if gk > 1 else [], sem=("parallel", "parallel", "arbitrary"), side=side)


def _stream_block(i, x_ref, meta_ref):
    d = x_ref.shape[-1]
    first = jnp.concatenate([jnp.zeros((LEAD, d), F32), meta_ref[...]], axis=0)
    return jnp.where(i == 0, first, x_ref[...])


def _norm(xb):
    mu = jnp.mean(xb, axis=-1, keepdims=True)
    xc = xb - mu
    var = jnp.mean(xc * xc, axis=-1, keepdims=True)
    rstd = lax.rsqrt(var + LN_EPS)
    return xc * rstd, rstd


def _ln_bwd_rows(dy, xhat, rstd, g):
    dyg = dy * g
    m1 = jnp.mean(dyg, axis=-1, keepdims=True)
    m2 = jnp.mean(dyg * xhat, axis=-1, keepdims=True)
    return rstd * (dyg - m1 - xhat * m2)


def _ln_in_fwd(x2d, meta, g, b, nb, side=None):
    seq, d = x2d.shape

    def body(x_ref, meta_ref, g_ref, b_ref, h_ref, hb_ref):
        xb = _stream_block(pl.program_id(0), x_ref, meta_ref)
        xhat, _ = _norm(xb)
        y = xhat * g_ref[...] + b_ref[...]
        h_ref[...] = y
        hb_ref[...] = y.astype(BF16)

    row = pl.BlockSpec((BLK, d), lambda i: (i, 0))
    vec = pl.BlockSpec((1, d), lambda i: (0, 0))
    return _gcall(body, "ln_in_fwd", (nb,),
                  [pl.BlockSpec((BLK, d), lambda i: (jnp.maximum(i - 1, 0), 0)), pl.BlockSpec((N_META, d), lambda i: (0, 0)), vec, vec],
                  [row, row], [_sds((nb * BLK, d), F32), _sds((nb * BLK, d), BF16)], (x2d, meta, g, b), sem=("parallel",), side=side)


def _res_ln_fwd(h, z, g, b, side=None):
    tp, d = h.shape

    def body(h_ref, z_ref, g_ref, b_ref, r_ref, y_ref, yb_ref):
        r = DN_ALPHA * h_ref[...] + z_ref[...]
        xhat, _ = _norm(r)
        y = xhat * g_ref[...] + b_ref[...]
        r_ref[...] = r
        y_ref[...] = y
        yb_ref[...] = y.astype(BF16)

    row = pl.BlockSpec((BLK, d), lambda i: (i, 0))
    vec = pl.BlockSpec((1, d), lambda i: (0, 0))
    return _gcall(body, "res_ln1_fwd", (tp // BLK,), [row, row, vec, vec], [row, row, row],
                  [_sds((tp, d), F32), _sds((tp, d), F32), _sds((tp, d), BF16)], (h, z, g, b), sem=("parallel",), side=side)


def _final_ln_loss(h1, z2, g, b, tgt):
    tp, d = h1.shape

    def body(h_ref, z_ref, g_ref, b_ref, t_ref, dr_ref, drb_ref, loss_ref, dg_ref, db_ref):
        i = pl.program_id(0)
        r = DN_ALPHA * h_ref[...] + z_ref[...]
        xhat, rstd = _norm(r)
        y = xhat * g_ref[...] + b_ref[...]
        err = jnp.where(i >= 1, y - t_ref[...], 0.0)
        dy = err * (1.0 / d)
        dr = _ln_bwd_rows(dy, xhat, rstd, g_ref[...])
        dr_ref[...] = dr
        drb_ref[...] = dr.astype(BF16)

        @pl.when(i == 0)
        def _():
            loss_ref[...] = jnp.zeros_like(loss_ref)
            dg_ref[...] = jnp.zeros_like(dg_ref)
            db_ref[...] = jnp.zeros_like(db_ref)

        loss_ref[...] += 0.5 * jnp.sum(jnp.sum(err * err, axis=-1, keepdims=True) * (1.0 / d), axis=0, keepdims=True)
        dg_ref[...] += jnp.sum(dy * xhat, axis=0, keepdims=True)
        db_ref[...] += jnp.sum(dy, axis=0, keepdims=True)

    row = pl.BlockSpec((BLK, d), lambda i: (i, 0))
    vec = pl.BlockSpec((1, d), lambda i: (0, 0))
    return _gcall(body, "final_ln_loss", (tp // BLK,),
                  [row, row, vec, vec, pl.BlockSpec((BLK, d), lambda i: (jnp.maximum(i - 1, 0), 0))],
                  [row, row, pl.BlockSpec((8, LANES), lambda i: (0, 0)), vec, vec],
                  [_sds((tp, d), F32), _sds((tp, d), BF16), _sds((8, LANES), F32), _sds((1, d), F32), _sds((1, d), F32)],
                  (h1, z2, g, b, tgt), sem=("arbitrary",))


def _ln1_bwd(d_res, d_mm, r, g, side=None):
    tp, d = r.shape

    def body(a_ref, m_ref, r_ref, g_ref, dr_ref, drb_ref, dg_ref, db_ref):
        dy = DN_ALPHA * a_ref[...] + m_ref[...]
        xhat, rstd = _norm(r_ref[...])
        dr = _ln_bwd_rows(dy, xhat, rstd, g_ref[...])
        dr_ref[...] = dr
        drb_ref[...] = dr.astype(BF16)

        @pl.when(pl.program_id(0) == 0)
        def _():
            dg_ref[...] = jnp.zeros_like(dg_ref)
            db_ref[...] = jnp.zeros_like(db_ref)

        dg_ref[...] += jnp.sum(dy * xhat, axis=0, keepdims=True)
        db_ref[...] += jnp.sum(dy, axis=0, keepdims=True)

    row = pl.BlockSpec((BLK, d), lambda i: (i, 0))
    vec = pl.BlockSpec((1, d), lambda i: (0, 0))
    return _gcall(body, "ln1_bwd", (tp // BLK,), [row, row, row, vec], [row, row, vec, vec],
                  [_sds((tp, d), F32), _sds((tp, d), BF16), _sds((1, d), F32), _sds((1, d), F32)], (d_res, d_mm, r, g),
                  sem=("arbitrary",), side=side)


def _ln_in_bwd(d_res, d_mm, x2d, meta, g, side=None):
    seq, d = x2d.shape
    nb = d_res.shape[0] // BLK

    def body(a_ref, m_ref, x_ref, meta_ref, g_ref, gx_ref, gm_ref, dg_ref, db_ref):
        i = pl.program_id(0)
        dy = DN_ALPHA * a_ref[...] + m_ref[...]
        xhat, rstd = _norm(_stream_block(i, x_ref, meta_ref))
        dx = _ln_bwd_rows(dy, xhat, rstd, g_ref[...])
        gx_ref[...] = dx

        @pl.when(i == 0)
        def _():
            gm_ref[...] = dx[LEAD:, :]
            dg_ref[...] = jnp.zeros_like(dg_ref)
            db_ref[...] = jnp.zeros_like(db_ref)

        dg_ref[...] += jnp.sum(dy * xhat, axis=0, keepdims=True)
        db_ref[...] += jnp.sum(dy, axis=0, keepdims=True)

    row = pl.BlockSpec((BLK, d), lambda i: (i, 0))
    xrow = pl.BlockSpec((BLK, d), lambda i: (jnp.maximum(i - 1, 0), 0))
    vec = pl.BlockSpec((1, d), lambda i: (0, 0))
    met = pl.BlockSpec((N_META, d), lambda i: (0, 0))
    return _gcall(body, "ln_in_bwd", (nb,), [row, row, xrow, met, vec], [xrow, met, vec, vec],
                  [_sds((seq, d), F32), _sds((N_META, d), F32), _sds((1, d), F32), _sds((1, d), F32)], (d_res, d_mm, x2d, meta, g),
                  sem=("arbitrary",), side=side)


def _rope_tables(tp):
    half = ROPE_DIM // 2
    inv_freq = ROPE_THETA ** (-jnp.arange(half, dtype=F32) * 2.0 / ROPE_DIM)
    pos = (jnp.arange(tp) - LEAD).astype(F32)
    ang = pos[:, None] * inv_freq[None, :]
    cos, sin = jnp.cos(ang), jnp.sin(ang)
    ones = jnp.ones((tp, HEAD_DIM - ROPE_DIM), F32)
    cos_h = jnp.concatenate([cos, cos, ones], axis=1)
    sin_h = jnp.concatenate([-sin, sin, 0.0 * ones], axis=1)
    reps = LANES // HEAD_DIM
    return jnp.tile(cos_h, (1, reps)), jnp.tile(sin_h, (1, reps))


def _rope_partner(x):
    half = ROPE_DIM // 2
    lane = lax.broadcasted_iota(jnp.int32, x.shape, 1) % HEAD_DIM
    upper = jnp.where(lane < ROPE_DIM, pltpu.roll(x, half, 1), 0.0)
    return jnp.where(lane < half, pltpu.roll(x, LANES - half, 1), upper)


def _rope_fwd(proj, cos, sin, n_rot, width):
    tp = proj.shape[0]

    def body(p_ref, c_ref, s_ref, o_ref):
        c, s = c_ref[...], s_ref[...]
        for j in range(width // LANES):
            sl = slice(j * LANES, (j + 1) * LANES)
            xj = p_ref[:, sl]
            if j < n_rot:
                xj = xj * c + _rope_partner(xj) * s
            o_ref[:, sl] = xj.astype(BF16)

    tab = pl.BlockSpec((BLK, LANES), lambda i: (i, 0))
    blk = pl.BlockSpec((BLK, width), lambda i: (i, 0))
    return _gcall(body, "rope_fwd", (tp // BLK,), [blk, tab, tab], blk, _sds((tp, width), BF16), (proj, cos, sin), sem=("parallel",))


def _rope_bwd(dq, dk_cur, dk_prev, dk_meta, dv_cur, dv_prev, dv_meta, cos, sin, side=None):
    tp, aw = dq.shape
    kw = dk_cur.shape[1]
    nb = tp // BLK

    def body(dq_ref, kc_ref, kp_ref, km_ref, vc_ref, vp_ref, vm_ref, c_ref, s_ref, o_ref):
        i = pl.program_id(0)
        c, s = c_ref[...], s_ref[...]
        has_next = i + 1 < nb

        def unrot(g):
            return g * c + _rope_partner(g * s)

        def kv_sum(cur, prv, met):
            return cur[...] + jnp.where(has_next, prv[...], 0.0) + jnp.where(i == 0, met[...], 0.0)

        for j in range(aw // LANES):
            sl = slice(j * LANES, (j + 1) * LANES)
            o_ref[:, sl] = unrot(dq_ref[:, sl]).astype(BF16)
        dk = kv_sum(kc_ref, kp_ref, km_ref)
        dv = kv_sum(vc_ref, vp_ref, vm_ref)
        for j in range(kw // LANES):
            sl = slice(j * LANES, (j + 1) * LANES)
            o_ref[:, aw + j * LANES:aw + (j + 1) * LANES] = unrot(dk[:, sl]).astype(BF16)
            o_ref[:, aw + kw + j * LANES:aw + kw + (j + 1) * LANES] = dv[:, sl].astype(BF16)

    cur = pl.BlockSpec((BLK, kw), lambda i: (i, 0))
    nxt = pl.BlockSpec((BLK, kw), lambda i: (jnp.minimum(i + 1, nb - 1), 0))
    met = pl.BlockSpec((BLK, kw), lambda i: (0, 0))
    tab = pl.BlockSpec((BLK, LANES), lambda i: (i, 0))
    return _gcall(body, "rope_bwd", (nb,), [pl.BlockSpec((BLK, aw), lambda i: (i, 0)), cur, nxt, met, cur, nxt, met, tab, tab],
                  pl.BlockSpec((BLK, aw + 2 * kw), lambda i: (i, 0)), _sds((tp, aw + 2 * kw), BF16),
                  (dq, dk_cur, dk_prev, dk_meta, dv_cur, dv_prev, dv_meta, cos, sin), sem=("parallel",), side=side)


def _attn_probs(n, q_ref, km_ref, kp_ref, kc_ref, sink_ref, grp):
    scale = HEAD_DIM ** -0.5
    qs = q_ref[...].reshape(grp * BLK, HEAD_DIM)
    kcat = jnp.concatenate([km_ref[...], kp_ref[...], kc_ref[...]], axis=0)
    s = lax.dot_general(qs, kcat, _DOT_DIMS["nt"], preferred_element_type=F32) * scale
    s = s.reshape(grp, BLK, 3 * BLK)
    r = lax.broadcasted_iota(jnp.int32, (1, BLK, 3 * BLK), 1)
    j = lax.broadcasted_iota(jnp.int32, (1, BLK, 3 * BLK), 2)
    q_idx = n * BLK + r
    meta_ok = (j >= LEAD) & (j < BLK) & (q_idx >= j)
    k_idx = (n - 1) * BLK + (j - BLK)
    diff = q_idx - k_idx
    band_ok = (j >= BLK) & (diff >= 0) & (diff < BLK) & (k_idx >= LEAD + N_META)
    s = jnp.where(meta_ok | band_ok, s, NEG_INF)
    sink = sink_ref[...]
    m = jnp.maximum(jnp.max(s, axis=-1, keepdims=True), sink)
    p = jnp.exp(s - m)
    e_sink = jnp.exp(sink - m)
    inv = 1.0 / (jnp.sum(p, axis=-1, keepdims=True) + e_sink)
    return qs, kcat, p * inv, e_sink * inv


def _attn_specs(grp):
    qspec = pl.BlockSpec((grp, BLK, HEAD_DIM), lambda kk, n: (kk, n, 0))
    kmeta = pl.BlockSpec((None, BLK, HEAD_DIM), lambda kk, n: (kk, 0, 0))
    kprev = pl.BlockSpec((None, BLK, HEAD_DIM), lambda kk, n: (kk, jnp.maximum(n - 1, 0), 0))
    kcur = pl.BlockSpec((None, BLK, HEAD_DIM), lambda kk, n: (kk, n, 0))
    sink = pl.BlockSpec((None, grp, BLK, 1), lambda kk, n: (kk, 0, 0, 0))
    return qspec, kmeta, kprev, kcur, sink


def _attn_fwd(q_hm, k_hm, v_hm, sink4, side=None):
    nq, tp, _ = q_hm.shape
    nkv = k_hm.shape[0]
    grp = nq // nkv

    def body(q_ref, km_ref, kp_ref, kc_ref, vm_ref, vp_ref, vc_ref, sink_ref, o_ref):
        n = pl.program_id(1)
        _, _, pn, _ = _attn_probs(n, q_ref, km_ref, kp_ref, kc_ref, sink_ref, grp)
        vcat = jnp.concatenate([vm_ref[...], vp_ref[...], vc_ref[...]], axis=0)
        o = jnp.dot(pn.reshape(grp * BLK, 3 * BLK).astype(BF16), vcat, preferred_element_type=F32)
        o_ref[...] = o.reshape(grp, BLK, HEAD_DIM).astype(BF16)

    qspec, kmeta, kprev, kcur, sink = _attn_specs(grp)
    return _gcall(body, "attn_fwd", (nkv, tp // BLK), [qspec, kmeta, kprev, kcur, kmeta, kprev, kcur, sink], qspec,
                  _sds((nq, tp, HEAD_DIM), BF16), (q_hm, k_hm, k_hm, k_hm, v_hm, v_hm, v_hm, sink4), sem=("parallel", "parallel"), side=side)


def _attn_bwd(q_hm, k_hm, v_hm, sink4, do_hm, side=None):
    nq, tp, _ = q_hm.shape
    nkv = k_hm.shape[0]
    grp = nq // nkv
    scale = HEAD_DIM ** -0.5

    def body(q_ref, km_ref, kp_ref, kc_ref, vm_ref, vp_ref, vc_ref, sink_ref, do_ref,
             dq_ref, dkc_ref, dkp_ref, dkm_ref, dvc_ref, dvp_ref, dvm_ref, dsk_ref):
        n = pl.program_id(1)
        qs, kcat, pn, p_sink = _attn_probs(n, q_ref, km_ref, kp_ref, kc_ref, sink_ref, grp)
        vcat = jnp.concatenate([vm_ref[...], vp_ref[...], vc_ref[...]], axis=0)
        pn2 = pn.reshape(grp * BLK, 3 * BLK)
        pnb = pn2.astype(BF16)
        dob = do_ref[...].reshape(grp * BLK, HEAD_DIM).astype(BF16)
        dp = lax.dot_general(dob, vcat, _DOT_DIMS["nt"], preferred_element_type=F32)
        delta = jnp.sum(pn2 * dp, axis=-1, keepdims=True)
        ds = (pn2 * (dp - delta) * scale).astype(BF16)
        dq_ref[...] = jnp.dot(ds, kcat, preferred_element_type=F32).reshape(grp, BLK, HEAD_DIM)
        dk = lax.dot_general(ds, qs, _DOT_DIMS["tn"], preferred_element_type=F32)
        dv = lax.dot_general(pnb, dob, _DOT_DIMS["tn"], preferred_element_type=F32)
        dkp_ref[...] = dk[BLK:2 * BLK]
        dkc_ref[...] = dk[2 * BLK:]
        dvp_ref[...] = dv[BLK:2 * BLK]
        dvc_ref[...] = dv[2 * BLK:]
        dsk = -jnp.sum(p_sink * delta.reshape(grp, BLK, 1), axis=1, keepdims=True)

        @pl.when(n == 0)
        def _():
            dkm_ref[...] = jnp.zeros_like(dkm_ref)
            dvm_ref[...] = jnp.zeros_like(dvm_ref)
            dsk_ref[...] = jnp.zeros_like(dsk_ref)

        dkm_ref[...] += dk[:BLK]
        dvm_ref[...] += dv[:BLK]
        dsk_ref[...] += jnp.broadcast_to(dsk, (grp, BLK, 1))

    qspec, kmeta, kprev, kcur, sink = _attn_specs(grp)
    kv_shape = _sds((nkv, tp, HEAD_DIM), F32)
    meta_shape = _sds((nkv, BLK, HEAD_DIM), F32)
    return _gcall(body, "attn_bwd", (nkv, tp // BLK), [qspec, kmeta, kprev, kcur, kmeta, kprev, kcur, sink, qspec],
                  [qspec, kcur, kcur, kmeta, kcur, kcur, kmeta, sink],
                  [_sds((nq, tp, HEAD_DIM), F32), kv_shape, kv_shape, meta_shape, kv_shape, kv_shape, meta_shape,
                   _sds((nkv, grp, BLK, 1), F32)],
                  (q_hm, k_hm, k_hm, k_hm, v_hm, v_hm, v_hm, sink4, do_hm), sem=("parallel", "arbitrary"), side=side)


def _pool_coef(row_blk, col_blk, w):
    r = lax.broadcasted_iota(jnp.int32, (BLK, BLK), 0)
    j = lax.broadcasted_iota(jnp.int32, (BLK, BLK), 1)
    t = row_blk * BLK + r - LEAD
    tj = col_blk * BLK + j - LEAD
    dist = t - tj
    inwin = (dist >= 0) & (dist < w) & (tj >= 0)
    count = jnp.maximum(jnp.minimum(t + 1, w), 1).astype(F32)
    return jnp.where(inwin, 1.0 / count, 0.0) - jnp.where((dist == 0) & (tj >= 0), 1.0, 0.0)


def _pool_fwd(proj, wg, scale, u_off, pool_w, side=None):
    tp = proj.shape[0]
    gw = pool_w // N_GRP
    nb = tp // BLK
    cb = u_off // gw

    def body(up_ref, uc_ref, wg_ref, sc_ref, pooled_ref, mx_ref, pm_ref):
        n, g = pl.program_id(0), pl.program_id(1)
        w = jnp.left_shift(2, g)
        pooled = (jnp.dot(_pool_coef(n, n - 1, w), up_ref[...], precision=HI, preferred_element_type=F32)
                  + jnp.dot(_pool_coef(n, n, w), uc_ref[...], precision=HI, preferred_element_type=F32))
        pb = pooled.astype(BF16)
        mx = jnp.dot(pb, wg_ref[...], preferred_element_type=F32)
        pooled_ref[...] = pb
        mx_ref[...] = mx
        pm_ref[...] = (mx * sc_ref[...]).astype(BF16)

    blk = pl.BlockSpec((BLK, gw), lambda n, g: (n, g))
    return _gcall(body, "pool_fwd", (nb, N_GRP),
                  [pl.BlockSpec((BLK, gw), lambda n, g: (jnp.maximum(n - 1, 0), cb + g)),
                   pl.BlockSpec((BLK, gw), lambda n, g: (n, cb + g)),
                   pl.BlockSpec((None, gw, gw), lambda n, g: (g, 0, 0)),
                   pl.BlockSpec((1, gw), lambda n, g: (0, g))],
                  [blk, blk, blk], [_sds((tp, pool_w), BF16), _sds((tp, pool_w), F32), _sds((tp, pool_w), BF16)],
                  (proj, proj, wg, scale), sem=("parallel", "parallel"), side=side)


def _pool_bwd_mix(d_pm, mx, pooled, wg, scale, side=None):
    tp, pool_w = d_pm.shape
    gw = pool_w // N_GRP

    def body(d_ref, mx_ref, pl_ref, wg_ref, sc_ref, dp_ref, dwg_ref, dsc_ref):
        n = pl.program_id(1)
        d = d_ref[...]
        dmx = (d * sc_ref[...]).astype(BF16)
        dp_ref[...] = lax.dot_general(dmx, wg_ref[...], _DOT_DIMS["nt"], preferred_element_type=F32)

        @pl.when(n == 0)
        def _():
            dwg_ref[...] = jnp.zeros_like(dwg_ref)
            dsc_ref[...] = jnp.zeros_like(dsc_ref)

        dwg_ref[...] += lax.dot_general(pl_ref[...], dmx, _DOT_DIMS["tn"], preferred_element_type=F32)
        dsc_ref[...] += jnp.sum(d * mx_ref[...], axis=0, keepdims=True)

    blk = pl.BlockSpec((BLK, gw), lambda g, n: (n, g))
    wspec = pl.BlockSpec((None, gw, gw), lambda g, n: (g, 0, 0))
    sspec = pl.BlockSpec((1, gw), lambda g, n: (0, g))
    return _gcall(body, "pool_bwd_mix", (N_GRP, tp // BLK), [blk, blk, blk, wspec, sspec], [blk, wspec, sspec],
                  [_sds((tp, pool_w), F32), _sds((N_GRP, gw, gw), F32), _sds((1, pool_w), F32)], (d_pm, mx, pooled, wg, scale),
                  sem=("parallel", "arbitrary"), side=side)


def _pool_bwd_band(dp, side=None):
    tp, pool_w = dp.shape
    gw = pool_w // N_GRP
    nb = tp // BLK

    def body(dc_ref, dn_ref, du_ref):
        n, g = pl.program_id(0), pl.program_id(1)
        w = jnp.left_shift(2, g)
        dnext = jnp.where(n + 1 < nb, dn_ref[...], 0.0)
        du = (lax.dot_general(_pool_coef(n, n, w), dc_ref[...], _DOT_DIMS["tn"], precision=HI, preferred_element_type=F32)
              + lax.dot_general(_pool_coef(n + 1, n, w), dnext, _DOT_DIMS["tn"], precision=HI, preferred_element_type=F32))
        du_ref[...] = du.astype(BF16)

    blk = pl.BlockSpec((BLK, gw), lambda n, g: (n, g))
    return _gcall(body, "pool_bwd_band", (nb, N_GRP), [blk, pl.BlockSpec((BLK, gw), lambda n, g: (jnp.minimum(n + 1, nb - 1), g))],
                  blk, _sds((tp, pool_w), BF16), (dp, dp), sem=("parallel", "parallel"), side=side)


def _gate_tiles(tp, d, g_off):
    tc = _pick(math.gcd(g_off, d), 512, 256, 128)
    tr = _pick(tp, 384, 128)
    return tr, tc


def _mix_fwd(proj, b_gate, a_out, p_out, g_off, side=None):
    tp, d = a_out.shape
    tr, tc = _gate_tiles(tp, d, g_off)
    c0, c1 = g_off // tc, (g_off + d) // tc

    def body(g0_ref, g1_ref, b_ref, a_ref, p_ref, o_ref):
        g0 = jax.nn.sigmoid(g0_ref[...] + b_ref[0:1, :])
        g1 = jax.nn.sigmoid(g1_ref[...] + b_ref[1:2, :])
        o_ref[...] = (g0 * a_ref[...] + g1 * p_ref[...]).astype(BF16)

    blk = pl.BlockSpec((tr, tc), lambda i, j: (i, j))
    return _gcall(body, "mix_fwd", (tp // tr, d // tc),
                  [pl.BlockSpec((tr, tc), lambda i, j: (i, c0 + j)), pl.BlockSpec((tr, tc), lambda i, j: (i, c1 + j)),
                   pl.BlockSpec((2, tc), lambda i, j: (0, j)), blk, blk],
                  blk, _sds((tp, d), BF16), (proj, proj, b_gate, a_out, p_out), sem=("parallel", "parallel"), side=side)


def _mix_bwd(proj, b_gate, a_out, p_out, d_mixed, g_off, side=None):
    tp, d = a_out.shape
    tr, tc = _gate_tiles(tp, d, g_off)
    c0, c1 = g_off // tc, (g_off + d) // tc

    def body(g0_ref, g1_ref, b_ref, a_ref, p_ref, d_ref, da_ref, dp_ref, dl0_ref, dl1_ref, db_ref):
        g0 = jax.nn.sigmoid(g0_ref[...] + b_ref[0:1, :])
        g1 = jax.nn.sigmoid(g1_ref[...] + b_ref[1:2, :])
        dm = d_ref[...]
        da_ref[...] = (dm * g0).astype(BF16)
        dp_ref[...] = (dm * g1).astype(BF16)
        dl0 = dm * a_ref[...] * g0 * (1.0 - g0)
        dl1 = dm * p_ref[...] * g1 * (1.0 - g1)
        dl0_ref[...] = dl0.astype(BF16)
        dl1_ref[...] = dl1.astype(BF16)

        @pl.when(pl.program_id(1) == 0)
        def _():
            db_ref[...] = jnp.zeros_like(db_ref)

        db_ref[...] += jnp.concatenate([jnp.sum(dl0, axis=0, keepdims=True), jnp.sum(dl1, axis=0, keepdims=True)], axis=0)

    blk = pl.BlockSpec((tr, tc), lambda j, i: (i, j))
    big = _sds((tp, d), BF16)
    return _gcall(body, "mix_bwd", (d // tc, tp // tr),
                  [pl.BlockSpec((tr, tc), lambda j, i: (i, c0 + j)), pl.BlockSpec((tr, tc), lambda j, i: (i, c1 + j)),
                   pl.BlockSpec((2, tc), lambda j, i: (0, j)), blk, blk, blk],
                  [blk, blk, blk, blk, pl.BlockSpec((2, tc), lambda j, i: (0, j))], [big, big, big, big, _sds((2, d), F32)],
                  (proj, proj, b_gate, a_out, p_out, d_mixed), sem=("parallel", "arbitrary"), side=side)


SWIGLU_ROWS = 64


def _swiglu_fwd(ff, side=None):
    tp, f2 = ff.shape
    f = f2 // 2
    tr = _pick(tp, SWIGLU_ROWS)

    def body(x_ref, o_ref):
        gate, up = x_ref[:, :f], x_ref[:, f:]
        o_ref[...] = (gate * jax.nn.sigmoid(gate) * up).astype(BF16)

    return _gcall(body, "swiglu_fwd", (tp // tr,), [pl.BlockSpec((tr, f2), lambda i: (i, 0))], pl.BlockSpec((tr, f), lambda i: (i, 0)),
                  _sds((tp, f), BF16), (ff,), sem=("parallel",), side=side)


def _swiglu_bwd(ff, d_act, side=None):
    tp, f2 = ff.shape
    f = f2 // 2
    tr = _pick(tp, SWIGLU_ROWS)

    def body(x_ref, d_ref, o_ref):
        gate, up = x_ref[:, :f], x_ref[:, f:]
        d = d_ref[...]
        sg = jax.nn.sigmoid(gate)
        silu = gate * sg
        o_ref[:, :f] = (d * up * (sg + silu * (1.0 - sg))).astype(BF16)
        o_ref[:, f:] = (d * silu).astype(BF16)

    return _gcall(body, "swiglu_bwd", (tp // tr,), [pl.BlockSpec((tr, f2), lambda i: (i, 0)), pl.BlockSpec((tr, f), lambda i: (i, 0))],
                  pl.BlockSpec((tr, f2), lambda i: (i, 0)), _sds((tp, f2), BF16), (ff, d_act), sem=("parallel",), side=side)


def _tile2(rows, cols, max_bytes=3 << 20):
    tc = _pick(cols, 1024, 640, 512)
    for tr in (512, 344, 256, 128, 64, 32, 16, 8):
        if rows % tr == 0 and tr * tc * 4 <= max_bytes:
            return tr, tc
    return rows, tc


def _cast_into_full(w, win, who, name, side=None):
    r, c = w.shape
    tr, tc = _tile2(r, c)

    def body(who_ref, x_ref, o_ref):
        o_ref[...] = x_ref[...].astype(BF16)

    if win.kind == "col":
        own = pl.BlockSpec((tr, tc), lambda i, j, who_ref: (i, who_ref[1] * (c // tc) + j))
    else:
        own = pl.BlockSpec((tr, tc), lambda i, j, who_ref: (who_ref[1] * (r // tr) + i, j))
    return _gcall(body, name, (r // tr, c // tc), [pl.BlockSpec((tr, tc), lambda i, j, who_ref: (i, j))], own,
                  _sds(win.full_shape, BF16), (w,), sem=("parallel", "parallel"), side=side, prefetch=(who,))


def _adamw(w, g, m, v, name, side=None):
    r, c = w.shape
    tr, tc = _tile2(r, c, 1 << 20)

    def body(w_ref, g_ref, m_ref, v_ref, d_ref, nm_ref, nv_ref):
        gg = g_ref[...]
        nm = ADAM_B1 * m_ref[...] + (1.0 - ADAM_B1) * gg
        nv = ADAM_B2 * v_ref[...] + (1.0 - ADAM_B2) * jnp.square(gg)
        m_hat = nm / (1.0 - ADAM_B1 ** ADAM_STEP)
        v_hat = nv / (1.0 - ADAM_B2 ** ADAM_STEP)
        d_ref[...] = -ADAM_LR * (m_hat / (jnp.sqrt(v_hat) + ADAM_EPS) + ADAM_WD * w_ref[...])
        nm_ref[...] = nm
        nv_ref[...] = nv

    blk = pl.BlockSpec((tr, tc), lambda i, j: (i, j))
    shp = _sds((r, c), F32)
    return _gcall(body, name, (r // tr, c // tc), [blk] * 4, [blk] * 3, [shp] * 3, (w, g, m, v), sem=("parallel", "parallel"), side=side)


def _piece_block_index(win, tr, tc):
    r, c = win.shard_shape
    if win.kind == "col":
        return lambda s, h, i, j: (win.row0 // tr + h * (win.half // tr) + i, s * (c // tc) + j)
    return lambda s, h, i, j: ((s * r + win.row0) // tr + h * (win.half // tr) + i, j)


def _chip_sum(g, who):
    pr, pc = g.win.piece_shape
    tr, tc = _tile2(pr, pc)
    full_idx = _piece_block_index(g.win, tr, tc)

    def body(who_ref, g_ref, o_ref, out_ref):
        out_ref[...] = (g_ref[...] + o_ref[...]).astype(BF16)

    slot = pl.BlockSpec((None, tr, tc), lambda s, i, j, who_ref: (s, i, j))
    g.chip = pl.pallas_call(
        body,
        name="chip_sum_" + g.key.replace("@", "_"),
        grid_spec=pltpu.PrefetchScalarGridSpec(
            num_scalar_prefetch=1,
            grid=(N_CHIPS, pr // tr, pc // tc),
            in_specs=[pl.BlockSpec((tr, tc), lambda s, i, j, who_ref: full_idx(s, who_ref[0], i, j)), slot],
            out_specs=slot,
        ),
        out_shape=_sds((N_CHIPS, pr, pc), BF16),
        compiler_params=_cparams(("parallel", "parallel", "parallel")),
    )(who, g.grad, g.other)


def _final_sum(g, shard, who):
    win = g.win
    pr, pc = win.piece_shape
    tr, tc = _tile2(pr, pc)
    full_idx = _piece_block_index(win, tr, tc)
    has_prev = shard.arr is not None

    def body(who_ref, g_ref, o_ref, l1_ref, l2_ref, l3_ref, *rest):
        out_ref = rest[-1]
        acc = g_ref[...] + o_ref[...]
        for l_ref in (l1_ref, l2_ref, l3_ref):
            acc = acc + l_ref[...].astype(F32)
        out_ref[...] = acc

    def landed_spec(k):
        return pl.BlockSpec((None, tr, tc), lambda i, j, who_ref: (who_ref[1 + k], i, j))

    in_specs = [pl.BlockSpec((tr, tc), lambda i, j, who_ref: full_idx(who_ref[1], who_ref[0], i, j)),
                pl.BlockSpec((None, tr, tc), lambda i, j, who_ref: (who_ref[1], i, j)),
                landed_spec(1), landed_spec(2), landed_spec(3)]
    args = [who, g.grad, g.other, g.landed, g.landed, g.landed]
    if has_prev:
        in_specs.append(ANY)
        args.append(shard.arr)
    shard.arr = pl.pallas_call(
        body,
        name="final_sum_" + g.key.replace("@", "_"),
        grid_spec=pltpu.PrefetchScalarGridSpec(
            num_scalar_prefetch=1,
            grid=(pr // tr, pc // tc),
            in_specs=in_specs,
            out_specs=pl.BlockSpec((tr, tc), lambda i, j, who_ref: (win.row0 // tr + who_ref[0] * (pr // tr) + i, j)),
        ),
        out_shape=_sds(win.shard_shape, F32),
        input_output_aliases={6: 0} if has_prev else {},
        compiler_params=_cparams(("parallel", "parallel")),
    )(*args)


def _gather_small(packed):
    r, c = packed.shape

    def body(in_ref, out_ref, send_sems, recv_sems):
        x, y, c_ = _coords()
        s_me = 2 * x + y
        out_ref[s_me] = in_ref[...]
        copies = []
        for j, (ox, oy) in enumerate(_other_chips(x, y)):
            cp = pltpu.make_async_remote_copy(src_ref=in_ref, dst_ref=out_ref.at[s_me], send_sem=send_sems.at[j],
                                              recv_sem=recv_sems.at[j], device_id=(ox, oy, c_), device_id_type=MESH)
            cp.start()
            copies.append(cp)
        for j, (ox, oy) in enumerate(_other_chips(x, y)):
            copies[j].wait_send()
            pltpu.make_async_remote_copy(src_ref=in_ref, dst_ref=out_ref.at[2 * ox + oy], send_sem=send_sems.at[j],
                                         recv_sem=recv_sems.at[j], device_id=(x, y, c_), device_id_type=MESH).wait_recv()

    return pl.pallas_call(
        body,
        name="gather_small",
        in_specs=[VMEM_FULL],
        out_specs=VMEM_FULL,
        out_shape=_sds((N_CHIPS, r, c), F32),
        scratch_shapes=[pltpu.SemaphoreType.DMA((3,)), pltpu.SemaphoreType.DMA((3,))],
    )(packed)


def _all_reduce_small(packed):
    r, c = packed.shape

    def body(in_ref, out_ref, slots, send_sems, recv_sems):
        x, y, c_ = _coords()
        me = 4 * x + 2 * y + c_
        slots[me] = in_ref[...]
        copies = []
        for k in range(1, N_DEV):
            peer = me ^ k
            cp = pltpu.make_async_remote_copy(src_ref=in_ref, dst_ref=slots.at[me], send_sem=send_sems.at[k - 1],
                                              recv_sem=recv_sems.at[k - 1],
                                              device_id=(peer // 4, (peer // 2) % 2, peer % 2), device_id_type=MESH)
            cp.start()
            copies.append(cp)
        for k in range(1, N_DEV):
            copies[k - 1].wait_send()
            pltpu.make_async_remote_copy(src_ref=in_ref, dst_ref=slots.at[me ^ k], send_sem=send_sems.at[k - 1],
                                         recv_sem=recv_sems.at[k - 1], device_id=(x, y, c_), device_id_type=MESH).wait_recv()
        acc = slots[0]
        for d in range(1, N_DEV):
            acc = acc + slots[d]
        out_ref[...] = acc

    return pl.pallas_call(
        body,
        name="all_reduce_small",
        in_specs=[VMEM_FULL],
        out_specs=VMEM_FULL,
        out_shape=_sds((r, c), F32),
        scratch_shapes=[pltpu.VMEM((N_DEV, r, c), F32), pltpu.SemaphoreType.DMA((N_DEV - 1,)), pltpu.SemaphoreType.DMA((N_DEV - 1,))],
    )(packed)


def _rows_of(a, width):
    flat = a.reshape(-1)
    n = -(-flat.shape[0] // width) * width
    return jnp.pad(flat, (0, n - flat.shape[0])).reshape(-1, width)


def _pad_rows(a, mult=8):
    n = -(-a.shape[0] // mult) * mult
    return jnp.pad(a, ((0, n - a.shape[0]), (0, 0)))


def _heads_major(a, nh):
    tp = a.shape[0]
    return a.reshape(tp, nh, HEAD_DIM).transpose(1, 0, 2)


def _heads_minor(a):
    nh, tp, hd = a.shape
    return a.transpose(1, 0, 2).reshape(tp, nh * hd)


def kernel(x, meta_tokens, ln_in_g, ln_in_b, w_in, b_gate, attn_sinks, w_attn_up, w_pool_grp, pool_scale, w_pool_up, w_out, ln1_g, ln1_b, w_ffn_in, w_ffn_down, ln2_g, ln2_b, loss_target, m_meta_tokens, m_ln_in_g, m_ln_in_b, m_w_in, m_b_gate, m_attn_sinks, m_w_attn_up, m_w_pool_grp, m_pool_scale, m_w_pool_up, m_w_out, m_ln1_g, m_ln1_b, m_w_ffn_in, m_w_ffn_down, m_ln2_g, m_ln2_b, v_meta_tokens, v_ln_in_g, v_ln_in_b, v_w_in, v_b_gate, v_attn_sinks, v_w_attn_up, v_w_pool_grp, v_pool_scale, v_w_pool_up, v_w_out, v_ln1_g, v_ln1_b, v_w_ffn_in, v_w_ffn_down, v_ln2_g, v_ln2_b):
    seq, d = x.shape[1], x.shape[2]
    tp = LEAD + N_META + seq
    nb = tp // BLK
    nq = attn_sinks.shape[1]
    grp = nq // N_KV
    attn_w = nq * HEAD_DIM
    kv_w = N_KV * HEAD_DIM
    qkv_w = attn_w + 2 * kv_w
    pool_w = pool_scale.shape[1]
    gw = pool_w // N_GRP
    g_off = qkv_w + pool_w
    dc = d // N_CHIPS
    cx, cy, cc = _coords()
    s_me = 2 * cx + cy
    who = jnp.stack([cc, s_me, (s_me + 1) % N_CHIPS, (s_me + 2) % N_CHIPS, (s_me + 3) % N_CHIPS]).astype(jnp.int32)

    names = ["w_in", "w_attn_up", "w_pool_grp", "w_pool_up", "w_out", "w_ffn_in", "w_ffn_down"]
    kinds = dict(w_in="col", w_attn_up="col", w_pool_grp="row", w_pool_up="col", w_out="row", w_ffn_in="col", w_ffn_down="row")
    grp_shard = (N_GRP * (gw // N_CHIPS), gw)
    big_w = dict(w_in=w_in[0], w_attn_up=w_attn_up[0], w_pool_grp=w_pool_grp[0].reshape(grp_shard), w_pool_up=w_pool_up[0],
                 w_out=w_out[0], w_ffn_in=w_ffn_in[0], w_ffn_down=w_ffn_down[0])
    big_m = dict(w_in=m_w_in[0], w_attn_up=m_w_attn_up[0], w_pool_grp=m_w_pool_grp[0].reshape(grp_shard), w_pool_up=m_w_pool_up[0],
                 w_out=m_w_out[0], w_ffn_in=m_w_ffn_in[0], w_ffn_down=m_w_ffn_down[0])
    big_v = dict(w_in=v_w_in[0], w_attn_up=v_w_attn_up[0], w_pool_grp=v_w_pool_grp[0].reshape(grp_shard), w_pool_up=v_w_pool_up[0],
                 w_out=v_w_out[0], w_ffn_in=v_w_ffn_in[0], w_ffn_down=v_w_ffn_down[0])
    small_rows = _pad_rows(jnp.concatenate([meta_tokens, b_gate[0]], axis=0))
    gathered = _gather_small(small_rows)
    gathered = gathered.transpose(1, 0, 2).reshape(small_rows.shape[0], d)
    meta_full, b_gate_full = gathered[:N_META], gathered[N_META:N_META + 2]

    W = {}

    def cast(n, side=None):
        win = _Win(kinds[n], big_w[n].shape)
        W[n] = _Weight(n, win, _cast_into_full(big_w[n], win, who, "cast_" + n, side=side))

    def whole(*ns):
        return [(W[n], W[n].win) for n in ns]

    def legs(first=(), second=(), third=()):
        ops = [_GatherD2d(third, DIAGONAL)] if third else []
        ops += [_GatherRing(second), _GatherD2d(second, NEIGHBOURS)] if second else []
        ops += [_GatherIci(first)] if first else []
        return _Side(ops)

    mid = ("w_attn_up", "w_pool_grp", "w_pool_up", "w_out")
    for n in ("w_in",) + mid:
        cast(n)
    cast("w_ffn_in", side=legs(first=whole("w_in")))
    cast("w_ffn_down", side=legs(first=whole(*mid), second=whole("w_in")))
    wins = {n: W[n].win for n in names}
    ffn_in_parts = [(W["w_ffn_in"], win) for win in W["w_ffn_in"].win.split(4)]
    x2d, tgt2d = x[0], loss_target[0]
    g_in, b_in = ln_in_g.reshape(1, d), ln_in_b.reshape(1, d)
    h0, h0b = _ln_in_fwd(x2d, meta_full, g_in, b_in, nb, side=legs(first=ffn_in_parts[0:1], second=whole(*mid), third=whole("w_in")))
    proj = _mm(h0b, W["w_in"].full, "nn", F32, 1408, 512, 4096, "mm_proj",
               side=legs(first=ffn_in_parts[1:4], second=ffn_in_parts[0:1], third=whole(*mid)))
    cos, sin = _rope_tables(tp)
    n_rot = (attn_w + kv_w) // LANES
    qkv = _rope_fwd(proj, cos, sin, n_rot, qkv_w)
    q_hm = _heads_major(qkv[:, :attn_w], nq)
    k_hm = _heads_major(qkv[:, attn_w:attn_w + kv_w], N_KV)
    v_hm = _heads_major(qkv[:, attn_w + kv_w:], N_KV)
    sink4 = jnp.broadcast_to(attn_sinks.reshape(N_KV, grp, 1, 1), (N_KV, grp, BLK, 1))
    o_hm = _attn_fwd(q_hm, k_hm, v_hm, sink4, side=legs(first=whole("w_ffn_down"), second=ffn_in_parts[1:4], third=ffn_in_parts[0:1]))
    o = _heads_minor(o_hm)
    wf_grp = W["w_pool_grp"].full.reshape(N_CHIPS, N_GRP, gw // N_CHIPS, gw).transpose(1, 0, 2, 3).reshape(N_GRP, gw, gw)
    pooled, mx, pm = _pool_fwd(proj, wf_grp, pool_scale, qkv_w, pool_w, side=legs(second=whole("w_ffn_down"), third=ffn_in_parts[1:4]))
    a_out = _mm(o, W["w_attn_up"].full, "nn", F32, 1408, 1024, 2048, "mm_attn_up", side=legs(third=whole("w_ffn_down")))
    p_out = _mm(pm, W["w_pool_up"].full, "nn", F32, 1408, 1024, 2048, "mm_pool_up")
    mixed = _mix_fwd(proj, b_gate_full, a_out, p_out, g_off)
    z1 = _mm(mixed, W["w_out"].full, "nn", F32, 1408, 512, 4096, "mm_out")
    r1, h1, h1b = _res_ln_fwd(h0, z1, ln1_g, ln1_b)
    ff = _mm(h1b, W["w_ffn_in"].full, "nn", F32, 1408, 512, 4096, "mm_ffn_in")
    act = _swiglu_fwd(ff)
    wf_down = W["w_ffn_down"].full
    z2 = _mm(act, wf_down, "nn", F32, 704, 1024, 5504, "mm_ffn_down")
    d_r2, d_r2b, loss_tile, dg2, db2 = _final_ln_loss(h1, z2, ln2_g, ln2_b, tgt2d)

    S = {n: _Shard(n) for n in names}

    def grads_of(name, grad, parts=1):
        return [_Grad(name, win, grad) for win in wins[name].split(parts)]

    def sibling(gs):
        return _ReduceSibling(gs)

    def chips(gs):
        for g in gs:
            _chip_sum(g, who)
        return _ReduceChips(gs)

    def share(gs):
        for g in gs:
            _final_sum(g, S[g.name], who)
        return _ShareReduced([(S[g.name], g.win) for g in gs])

    g6 = grads_of("w_ffn_down", _mm(act, d_r2b, "tn", F32, 256, 1024, tp, "mm_gw_ffn_down", j_outer=True))
    d_act = _mm(d_r2b, wf_down, "nt", F32, 1408, 256, 4096, "mm_d_act", side=_Side([sibling(g6)]))
    d_ff = _swiglu_bwd(ff, d_act)
    g5 = grads_of("w_ffn_in", _mm(h1b, d_ff, "tn", F32, 1024, 512, tp, "mm_gw_ffn_in", side=_Side([chips(g6)])), parts=2)
    d_h1_mm = _mm(d_ff, W["w_ffn_in"].full, "nt", F32, 704, 1024, 5504, "mm_d_h1", side=_Side([share(g6), sibling(g5)]))
    d_r1, d_r1b, dg1, db1 = _ln1_bwd(d_r2, d_h1_mm, r1, ln1_g)
    g4 = grads_of("w_out", _mm(mixed, d_r1b, "tn", F32, 1024, 512, tp, "mm_gw_out"))
    d_mixed = _mm(d_r1b, W["w_out"].full, "nt", F32, 1408, 512, 4096, "mm_d_mixed", side=_Side([sibling(g4)]))
    d_a, d_p, d_gl0, d_gl1, d_bgate = _mix_bwd(proj, b_gate_full, a_out, p_out, d_mixed, g_off, side=_Side([chips(g4)]))
    g1 = grads_of("w_attn_up", _mm(o, d_a, "tn", F32, 1024, 512, tp, "mm_gw_attn_up", side=_Side([share(g4)])))
    d_o = _mm(d_a, W["w_attn_up"].full, "nt", F32, 1408, 512, 4096, "mm_d_o", side=_Side([sibling(g1)]))
    g3 = grads_of("w_pool_up", _mm(pm, d_p, "tn", F32, 1024, 512, tp, "mm_gw_pool_up", side=_Side([chips(g1)])))
    d_pm = _mm(d_p, W["w_pool_up"].full, "nt", F32, 1408, 512, 4096, "mm_d_pm", side=_Side([sibling(g3)]))
    d_pooled, gw_grp, d_scale = _pool_bwd_mix(d_pm, mx, pooled, wf_grp, pool_scale, side=_Side([chips(g3), share(g1)]))
    gw_grp_sm = gw_grp.reshape(N_GRP, N_CHIPS, gw // N_CHIPS, gw).transpose(1, 0, 2, 3).reshape(N_CHIPS * grp_shard[0], gw)
    g2 = grads_of("w_pool_grp", gw_grp_sm)
    d_u = _pool_bwd_band(d_pooled, side=_Side([sibling(g2), share(g3)]))
    dq_hm, dk_cur, dk_prev, dk_meta, dv_cur, dv_prev, dv_meta, d_sink = _attn_bwd(
        q_hm, k_hm, v_hm, sink4, _heads_major(d_o, nq), side=_Side([chips(g5[0:1] + g2)]))
    d_qkv = _rope_bwd(_heads_minor(dq_hm), _heads_minor(dk_cur), _heads_minor(dk_prev), _heads_minor(dk_meta),
                      _heads_minor(dv_cur), _heads_minor(dv_prev), _heads_minor(dv_meta), cos, sin, side=_Side([share(g2)]))
    d_proj = jnp.concatenate([d_qkv, d_u, d_gl0, d_gl1], axis=1)
    g0 = grads_of("w_in", _mm(h0b, d_proj, "tn", F32, 1024, 512, tp, "mm_gw_in", side=_Side([chips(g5[1:2]), share(g5[0:1])])), parts=2)
    d_h0_mm = _mm(d_proj, W["w_in"].full, "nt", F32, 1408, 1024, 2560, "mm_d_h0", side=_Side([sibling(g0), share(g5[1:2])]))
    grad_x2d, d_meta, dg_in, db_in = _ln_in_bwd(d_r1, d_h0_mm, x2d, meta_full, g_in)

    small_parts = [d_meta, d_bgate, dg_in, db_in, dg1, db1, dg2, db2, _rows_of(d_scale, d), _rows_of(d_sink[:, :, 0, 0], d)]
    offs = [0]
    for p in small_parts:
        offs.append(offs[-1] + p.shape[0])
    red = _all_reduce_small(_pad_rows(jnp.concatenate(small_parts, axis=0)))
    r_meta, r_bgate, r_g_in, r_b_in, r_g1, r_b1, r_g2, r_b2, r_scale, r_sink = [red[offs[k]:offs[k + 1]] for k in range(len(small_parts))]
    col0 = s_me * dc
    g_meta = lax.dynamic_slice(r_meta, (0, col0), (N_META, dc))
    g_bgate = lax.dynamic_slice(r_bgate, (0, col0), (2, dc))
    g_scale = r_scale.reshape(-1)[:pool_w]
    g_sink = r_sink.reshape(-1)[:nq]

    upd = {}

    def adamw(n, side=None):
        upd[n] = _adamw(big_w[n], S[n].arr, big_m[n], big_v[n], "adamw_" + n, side=side)

    adamw("w_ffn_in", side=_Side([chips(g0[0:1])]))
    adamw("w_ffn_down", side=_Side([chips(g0[1:2]), share(g0[0:1])]))
    adamw("w_out", side=_Side([share(g0[1:2])]))
    for n in ("w_attn_up", "w_pool_grp", "w_pool_up", "w_in"):
        adamw(n)

    small_w = [meta_tokens, b_gate[0], ln_in_g, ln_in_b, attn_sinks, pool_scale, ln1_g, ln1_b, ln2_g, ln2_b]
    small_m = [m_meta_tokens, m_b_gate[0], m_ln_in_g, m_ln_in_b, m_attn_sinks, m_pool_scale, m_ln1_g, m_ln1_b, m_ln2_g, m_ln2_b]
    small_v = [v_meta_tokens, v_b_gate[0], v_ln_in_g, v_ln_in_b, v_attn_sinks, v_pool_scale, v_ln1_g, v_ln1_b, v_ln2_g, v_ln2_b]
    small_g = [g_meta, g_bgate, r_g_in, r_b_in, g_sink, g_scale, r_g1, r_b1, r_g2, r_b2]
    small_g = [g.reshape(w.shape) for g, w in zip(small_g, small_w)]

    def pack(parts):
        return _pad_rows(jnp.concatenate([_rows_of(p, dc) for p in parts], axis=0))

    s_delta, s_m, s_v = _adamw(pack(small_w), pack(small_g), pack(small_m), pack(small_v), "adamw_small")

    def unpack(packed):
        out, row = [], 0
        for w in small_w:
            nrow = -(-w.size // dc)
            out.append(packed[row:row + nrow].reshape(-1)[:w.size].reshape(w.shape))
            row += nrow
        return out

    s_delta, s_m, s_v = unpack(s_delta), unpack(s_m), unpack(s_v)

    order = ["meta_tokens", "ln_in_g", "ln_in_b", "w_in", "b_gate", "attn_sinks", "w_attn_up", "w_pool_grp", "pool_scale",
             "w_pool_up", "w_out", "ln1_g", "ln1_b", "w_ffn_in", "w_ffn_down", "ln2_g", "ln2_b"]
    small_names = ["meta_tokens", "b_gate", "ln_in_g", "ln_in_b", "attn_sinks", "pool_scale", "ln1_g", "ln1_b", "ln2_g", "ln2_b"]
    out_shapes = dict(meta_tokens=meta_tokens.shape, ln_in_g=ln_in_g.shape, ln_in_b=ln_in_b.shape, w_in=w_in.shape, b_gate=b_gate.shape,
                      attn_sinks=attn_sinks.shape, w_attn_up=w_attn_up.shape, w_pool_grp=w_pool_grp.shape, pool_scale=pool_scale.shape,
                      w_pool_up=w_pool_up.shape, w_out=w_out.shape, ln1_g=ln1_g.shape, ln1_b=ln1_b.shape, w_ffn_in=w_ffn_in.shape,
                      w_ffn_down=w_ffn_down.shape, ln2_g=ln2_g.shape, ln2_b=ln2_b.shape)
    grads, deltas, new_m, new_v = {}, {}, {}, {}
    for n in names:
        grads[n], (deltas[n], new_m[n], new_v[n]) = S[n].arr, upd[n]
    for k, n in enumerate(small_names):
        grads[n], deltas[n], new_m[n], new_v[n] = small_g[k], s_delta[k], s_m[k], s_v[k]

    loss = lax.psum(loss_tile[0, 0], ("x", "y", "c"))
    outs = [loss, grad_x2d.reshape(x.shape)]
    for group in (grads, deltas, new_m, new_v):
        outs += [group[n].reshape(out_shapes[n]) for n in order]
    return tuple(outs)
```

```python
import functools
import math

import jax
import jax.numpy as jnp
from jax import lax
from jax.experimental import pallas as pl
from jax.experimental.pallas import tpu as pltpu

F32 = jnp.float32
BF16 = jnp.bfloat16
MESH = pl.DeviceIdType.MESH
ANY = pl.BlockSpec(memory_space=pl.ANY)
VMEM_FULL = pl.BlockSpec(memory_space=pltpu.VMEM)

N_META = 16
HEAD_DIM = 64
N_KV = 4
BLK = 128
LEAD = (-N_META) % BLK
ROPE_DIM = HEAD_DIM // 4
ROPE_THETA = 500000.0
NEG_INF = -1e30
POOL_WINDOWS = (2, 4, 8, 16)
N_GRP = len(POOL_WINDOWS)
LN_EPS = 1e-5
DN_ALPHA = 2.0 ** 0.25
ADAM_LR = 0.001
ADAM_B1 = 0.9
ADAM_B2 = 0.999
ADAM_EPS = 1e-08
ADAM_WD = 0.01
ADAM_STEP = 10
N_CHIPS = 4
N_DEV = 8
LANES = 128
VMEM_LIMIT_MB = 56
HI = lax.Precision.HIGHEST


def _cparams(sem=None, vmem_mb=VMEM_LIMIT_MB):
    kw = dict(vmem_limit_bytes=vmem_mb << 20)
    if sem is not None:
        kw["dimension_semantics"] = sem
    return pltpu.CompilerParams(**kw)


def _pick(dim, *cands):
    for c in cands:
        if c <= dim and dim % c == 0:
            return c
    return dim


def _sds(shape, dtype):
    return jax.ShapeDtypeStruct(tuple(shape), dtype)


def _coords():
    return lax.axis_index("x"), lax.axis_index("y"), lax.axis_index("c")


def _other_chips(x, y):
    return [(1 - x, y), (x, 1 - y), (1 - x, 1 - y)]


class _Win:
    def __init__(self, kind, shard_shape, row0=0, nrows=None):
        self.kind, self.shard_shape, self.row0 = kind, tuple(shard_shape), row0
        self.nrows = shard_shape[0] if nrows is None else nrows
        self.half = self.nrows // 2

    @property
    def piece_shape(self):
        return (self.half, self.shard_shape[1])

    @property
    def full_shape(self):
        r, c = self.shard_shape
        return (r, N_CHIPS * c) if self.kind == "col" else (N_CHIPS * r, c)

    def in_full(self, ref, s, h, q=None):
        r, c = self.shard_shape
        start, size = self.row0 + h * self.half, self.half
        if q is not None:
            start, size = start + q * (self.half // 2), self.half // 2
        if self.kind == "col":
            return ref.at[pl.ds(start, size), pl.ds(s * c, c)]
        return ref.at[pl.ds(s * r + start, size), :]

    def in_shard(self, ref, h):
        return ref.at[pl.ds(self.row0 + h * self.half, self.half), :]

    def split(self, n):
        return [_Win(self.kind, self.shard_shape, self.row0 + q * (self.nrows // n), self.nrows // n) for q in range(n)]


class _Weight:
    def __init__(self, name, win, full):
        self.name, self.win, self.full = name, win, full


class _Grad:
    def __init__(self, name, win, grad, grad_row0=0):
        self.name, self.win, self.grad = name, win, grad
        self.key = "%s@%d" % (name, win.row0)
        self.grad_key = "%s@%d" % (name, grad_row0)
        self.src = _Win(win.kind, win.shard_shape, win.row0 - grad_row0, win.nrows)
        self.other = self.chip = self.landed = None


class _Shard:
    def __init__(self, name):
        self.name, self.arr = name, None


COPY_STREAMS = 8
BF16_ROWS = 16


def _stream_views(ref):
    rows, cols = ref.shape
    n = COPY_STREAMS
    if rows % (n * BF16_ROWS) == 0:
        return [ref.at[pl.ds(i * (rows // n), rows // n), :] for i in range(n)]
    if cols % (n * LANES) == 0:
        return [ref.at[:, pl.ds(i * (cols // n), cols // n)] for i in range(n)]
    return [ref]


class _StreamedCopy:
    def __init__(self, make, src, dst):
        self.whole = make(src, dst)
        self.parts = [make(s, d) for s, d in zip(_stream_views(src), _stream_views(dst))]

    def start(self):
        for cp in self.parts:
            cp.start()

    def wait(self):
        self.whole.wait()

    def wait_send(self):
        self.whole.wait_send()

    def wait_recv(self):
        self.whole.wait_recv()


class _Ctx:
    def __init__(self, side, in_refs, out_refs, send_sems, recv_sems, local_sems, base, lbase):
        self.side, self.in_refs, self.out_refs = side, in_refs, out_refs
        self.send_sems, self.recv_sems, self.local_sems, self.base, self.lbase = send_sems, recv_sems, local_sems, base, lbase

    def ref(self, key):
        info = self.side.info[key]
        return self.in_refs[info["in"]] if info["in"] is not None else self.out_refs[info["out"]]

    def remote(self, k, src, dst, to):
        def make(s, d):
            return pltpu.make_async_remote_copy(src_ref=s, dst_ref=d, send_sem=self.send_sems.at[self.base + k],
                                                recv_sem=self.recv_sems.at[self.base + k], device_id=to, device_id_type=MESH)

        return _StreamedCopy(make, src, dst)


class _Side:
    def __init__(self, ops):
        self.ops, self.info, self.keys = ops, {}, []
        self.nsem = self.nlocal = 0
        self.bases = []
        for op in ops:
            op.register(self)
            self.bases.append((self.nsem, self.nlocal))
            self.nsem += op.nsem
            self.nlocal += op.nlocal
        self.inputs, self.out_shape, self.aliases = [], [], {}
        for key in self.keys:
            info = self.info[key]
            info["in"] = info["out"] = None
            if info["arr"] is not None:
                info["in"] = len(self.inputs)
                self.inputs.append(info["arr"])
            if info["write"]:
                info["out"] = len(self.out_shape)
                self.out_shape.append(info["sds"])
                if info["in"] is not None:
                    self.aliases[info["in"]] = info["out"]

    def need(self, key, arr=None, sds=None, write=False):
        if key not in self.info:
            self.keys.append(key)
            self.info[key] = dict(arr=arr, sds=sds if arr is None else _sds(arr.shape, arr.dtype), write=write)
        else:
            self.info[key]["write"] = self.info[key]["write"] or write

    def _ctx(self, k, in_refs, out_refs, sems):
        return _Ctx(self, in_refs, out_refs, sems[0], sems[1], sems[2], *self.bases[k])

    def start(self, in_refs, out_refs, sems):
        for k, op in enumerate(self.ops):
            op.start(self._ctx(k, in_refs, out_refs, sems))

    def finish(self, in_refs, out_refs, sems):
        for k, op in enumerate(self.ops):
            op.finish(self._ctx(k, in_refs, out_refs, sems))

    def scratch(self):
        return [pltpu.SemaphoreType.DMA((max(self.nsem, 1),)), pltpu.SemaphoreType.DMA((max(self.nsem, 1),)),
                pltpu.SemaphoreType.DMA((max(self.nlocal, 1),))]

    def commit(self, outs):
        res = {key: outs[self.info[key]["out"]] for key in self.keys if self.info[key]["write"]}
        for op in self.ops:
            op.commit(res)


NEIGHBOURS, DIAGONAL = (0, 1), (2,)


class _GatherIci:
    def __init__(self, pairs):
        self.pairs = pairs
        self.nsem, self.nlocal = 2 * len(pairs), 0

    def register(self, side):
        for w, win in self.pairs:
            side.need(("full", w.name), arr=w.full, write=True)

    def _copies(self, ctx):
        x, y, c = _coords()
        s_me = 2 * x + y
        sends, recvs = [], []
        for t, (w, win) in enumerate(self.pairs):
            full = ctx.ref(("full", w.name))
            for j in NEIGHBOURS:
                ox, oy = _other_chips(x, y)[j]
                mine, landing = win.in_full(full, s_me, c), win.in_full(full, 2 * ox + oy, c)
                sends.append(ctx.remote(2 * t + j, mine, mine, (ox, oy, c)))
                recvs.append(ctx.remote(2 * t + j, landing, landing, (x, y, c)))
        return sends, recvs

    def start(self, ctx):
        for cp in self._copies(ctx)[0]:
            cp.start()

    def finish(self, ctx):
        sends, recvs = self._copies(ctx)
        for cp in recvs:
            cp.wait_recv()
        for cp in sends:
            cp.wait_send()

    def commit(self, res):
        for w, _ in self.pairs:
            w.full = res[("full", w.name)]


class _GatherRing:
    def __init__(self, pairs):
        self.pairs = pairs
        self.nsem, self.nlocal = 2 * len(pairs), 0

    def register(self, side):
        for w, win in self.pairs:
            side.need(("full", w.name), arr=w.full, write=True)

    def _copies(self, ctx):
        x, y, c = _coords()
        s_x, s_y, s_d = 2 * (1 - x) + y, 2 * x + (1 - y), 2 * (1 - x) + (1 - y)
        sends, recvs = [], []
        for t, (w, win) in enumerate(self.pairs):
            full = ctx.ref(("full", w.name))
            for q, (s_from, to) in enumerate([(s_x, (x, 1 - y, c)), (s_y, (1 - x, y, c))]):
                passed, landing = win.in_full(full, s_from, c, q), win.in_full(full, s_d, c, q)
                sends.append(ctx.remote(2 * t + q, passed, passed, to))
                recvs.append(ctx.remote(2 * t + q, landing, landing, (x, y, c)))
        return sends, recvs

    def start(self, ctx):
        for cp in self._copies(ctx)[0]:
            cp.start()

    def finish(self, ctx):
        sends, recvs = self._copies(ctx)
        for cp in recvs:
            cp.wait_recv()
        for cp in sends:
            cp.wait_send()

    def commit(self, res):
        for w, _ in self.pairs:
            w.full = res[("full", w.name)]


class _GatherD2d:
    def __init__(self, pairs, which):
        self.pairs, self.which = pairs, which
        self.nsem, self.nlocal = 3 * len(pairs), 0

    def register(self, side):
        for w, win in self.pairs:
            side.need(("full", w.name), arr=w.full, write=True)

    def _copies(self, ctx):
        x, y, c = _coords()
        sends, recvs = [], []
        for t, (w, win) in enumerate(self.pairs):
            full = ctx.ref(("full", w.name))
            for j in self.which:
                ox, oy = _other_chips(x, y)[j]
                mine, theirs = win.in_full(full, 2 * ox + oy, c), win.in_full(full, 2 * ox + oy, 1 - c)
                sends.append(ctx.remote(3 * t + j, mine, mine, (x, y, 1 - c)))
                recvs.append(ctx.remote(3 * t + j, theirs, theirs, (x, y, c)))
        return sends, recvs

    def start(self, ctx):
        for cp in self._copies(ctx)[0]:
            cp.start()

    def finish(self, ctx):
        sends, recvs = self._copies(ctx)
        for cp in recvs:
            cp.wait_recv()
        for cp in sends:
            cp.wait_send()

    def commit(self, res):
        for w, _ in self.pairs:
            w.full = res[("full", w.name)]


class _ReduceSibling:
    def __init__(self, grads):
        self.grads = grads
        self.nsem, self.nlocal = N_CHIPS * len(grads), 0

    def register(self, side):
        for g in self.grads:
            side.need(("grad", g.grad_key), arr=g.grad)
            side.need(("other", g.key), sds=_sds((N_CHIPS,) + g.win.piece_shape, F32), write=True)

    def _copies(self, ctx):
        x, y, c = _coords()
        out = []
        for t, g in enumerate(self.grads):
            grad, other = ctx.ref(("grad", g.grad_key)), ctx.ref(("other", g.key))
            for s in range(N_CHIPS):
                out.append(ctx.remote(N_CHIPS * t + s, g.src.in_full(grad, s, 1 - c), other.at[s], (x, y, 1 - c)))
        return out

    def start(self, ctx):
        for cp in self._copies(ctx):
            cp.start()

    def finish(self, ctx):
        for cp in self._copies(ctx):
            cp.wait()

    def commit(self, res):
        for g in self.grads:
            g.other = res[("other", g.key)]


class _ReduceChips:
    def __init__(self, grads):
        self.grads = grads
        self.nsem, self.nlocal = 3 * len(grads), 0

    def register(self, side):
        for g in self.grads:
            side.need(("chip", g.key), arr=g.chip)
            side.need(("landed", g.key), sds=_sds(g.chip.shape, g.chip.dtype), write=True)

    def _copies(self, ctx):
        x, y, c = _coords()
        s_me = 2 * x + y
        out = []
        for t, g in enumerate(self.grads):
            chip, landed = ctx.ref(("chip", g.key)), ctx.ref(("landed", g.key))
            for j, (ox, oy) in enumerate(_other_chips(x, y)):
                out.append(ctx.remote(3 * t + j, chip.at[2 * ox + oy], landed.at[s_me], (ox, oy, c)))
        return out

    def start(self, ctx):
        for cp in self._copies(ctx):
            cp.start()

    def finish(self, ctx):
        for cp in self._copies(ctx):
            cp.wait()

    def commit(self, res):
        for g in self.grads:
            g.landed = res[("landed", g.key)]


class _ShareReduced:
    def __init__(self, items):
        self.items = items
        self.nsem, self.nlocal = len(items), 0

    def register(self, side):
        for sh, _ in self.items:
            side.need(("reduced", sh.name), arr=sh.arr, write=True)

    def _copies(self, ctx):
        x, y, c = _coords()
        sends, recvs = [], []
        for t, (sh, win) in enumerate(self.items):
            ref = ctx.ref(("reduced", sh.name))
            sends.append(ctx.remote(t, win.in_shard(ref, c), win.in_shard(ref, c), (x, y, 1 - c)))
            recvs.append(ctx.remote(t, win.in_shard(ref, 1 - c), win.in_shard(ref, 1 - c), (x, y, c)))
        return sends, recvs

    def start(self, ctx):
        for cp in self._copies(ctx)[0]:
            cp.start()

    def finish(self, ctx):
        sends, recvs = self._copies(ctx)
        for cp in recvs:
            cp.wait_recv()
        for cp in sends:
            cp.wait_send()

    def commit(self, res):
        for sh, _ in self.items:
            sh.arr = res[("reduced", sh.name)]


def _gcall(body, name, grid, in_specs, out_specs, out_shape, args, scratch=(), sem=None, side=None, prefetch=()):
    single = not isinstance(out_shape, (list, tuple))
    out_shapes = [out_shape] if single else list(out_shape)
    out_specs = [out_specs] if single else list(out_specs)
    hosted = side is not None and bool(side.ops)
    n_pf, n_in, n_out, n_scr = len(prefetch), len(args), len(out_shapes), len(scratch)
    ns_in, ns_out = (len(side.inputs), len(side.out_shape)) if hosted else (0, 0)

    def wrapped(*refs):
        pf, refs = refs[:n_pf], refs[n_pf:]
        a, si = refs[:n_in], refs[n_in:n_in + ns_in]
        o = refs[n_in + ns_in:n_in + ns_in + n_out]
        so = refs[n_in + ns_in + n_out:n_in + ns_in + n_out + ns_out]
        rest = refs[n_in + ns_in + n_out + ns_out:]
        scr, sems = rest[:n_scr], rest[n_scr:]
        ids = [pl.program_id(k) for k in range(len(grid))]
        first = functools.reduce(jnp.logical_and, [i == 0 for i in ids])
        last = functools.reduce(jnp.logical_and, [i == g - 1 for i, g in zip(ids, grid)])

        @pl.when(first)
        def _():
            side.start(si, so, sems)

        body(*pf, *a, *o, *scr)

        @pl.when(last)
        def _():
            side.finish(si, so, sems)

    res = pl.pallas_call(
        wrapped if hosted else body, name=name,
        grid_spec=pltpu.PrefetchScalarGridSpec(
            num_scalar_prefetch=n_pf, grid=grid,
            in_specs=list(in_specs) + [ANY] * ns_in,
            out_specs=out_specs + [ANY] * ns_out,
            scratch_shapes=list(scratch) + (side.scratch() if hosted else []),
        ),
        out_shape=out_shapes + (side.out_shape if hosted else []),
        input_output_aliases={n_pf + n_in + i: n_out + j for i, j in side.aliases.items()} if hosted else {},
        compiler_params=_cparams(("arbitrary",) * len(grid) if hosted else sem),
    )(*prefetch, *args, *(side.inputs if hosted else []))
    if hosted:
        side.commit(res[n_out:])
    return res[0] if single else res[:n_out]


_DOT_DIMS = {
    "nn": (((1,), (0,)), ((), ())),
    "nt": (((1,), (1,)), ((), ())),
    "tn": (((0,), (0,)), ((), ())),
}


def _mm(a, b, mode, out_dtype, tm, tn, tk, name, j_outer=False, side=None):
    if mode == "nn":
        (m, k), n = a.shape, b.shape[1]
    elif mode == "nt":
        (m, k), n = a.shape, b.shape[0]
    else:
        (k, m), n = a.shape, b.shape[1]
    tm, tn, tk = _pick(m, tm), _pick(n, tn), _pick(k, tk)
    gi, gj, gk = m // tm, n // tn, k // tk
    dims = _DOT_DIMS[mode]

    def ij(g0, g1):
        return (g1, g0) if j_outer else (g0, g1)

    if mode == "tn":
        a_spec = pl.BlockSpec((tk, tm), lambda g0, g1, kk: (kk, ij(g0, g1)[0]))
    else:
        a_spec = pl.BlockSpec((tm, tk), lambda g0, g1, kk: (ij(g0, g1)[0], kk))
    if mode == "nt":
        b_spec = pl.BlockSpec((tn, tk), lambda g0, g1, kk: (ij(g0, g1)[1], kk))
    else:
        b_spec = pl.BlockSpec((tk, tn), lambda g0, g1, kk: (kk, ij(g0, g1)[1]))
    o_spec = pl.BlockSpec((tm, tn), lambda g0, g1, kk: ij(g0, g1))

    def body(a_ref, b_ref, o_ref, *scr):
        p = lax.dot_general(a_ref[...], b_ref[...], dims, preferred_element_type=F32)
        if gk == 1:
            o_ref[...] = p.astype(out_dtype)
        else:
            acc = scr[0]
            kk = pl.program_id(2)

            @pl.when(kk == 0)
            def _():
                acc[...] = p

            @pl.when(kk > 0)
            def _():
                acc[...] += p

            @pl.when(kk == gk - 1)
            def _():
                o_ref[...] = acc[...].astype(out_dtype)

    return _gcall(body, name, (gj, gi, gk) if j_outer else (gi, gj, gk), [a_spec, b_spec], o_spec, _sds((m, n), out_dtype), (a, b),
                  scratch=[pltpu.VMEM((tm, tn), F32)] if gk > 1 else [], sem=("parallel", "parallel", "arbitrary"), side=side)


def _stream_block(i, x_ref, meta_ref):
    d = x_ref.shape[-1]
    first = jnp.concatenate([jnp.zeros((LEAD, d), F32), meta_ref[...]], axis=0)
    return jnp.where(i == 0, first, x_ref[...])


def _norm(xb):
    mu = jnp.mean(xb, axis=-1, keepdims=True)
    xc = xb - mu
    var = jnp.mean(xc * xc, axis=-1, keepdims=True)
    rstd = lax.rsqrt(var + LN_EPS)
    return xc * rstd, rstd


def _ln_bwd_rows(dy, xhat, rstd, g):
    dyg = dy * g
    m1 = jnp.mean(dyg, axis=-1, keepdims=True)
    m2 = jnp.mean(dyg * xhat, axis=-1, keepdims=True)
    return rstd * (dyg - m1 - xhat * m2)


def _ln_in_fwd(x2d, meta, g, b, nb, side=None):
    seq, d = x2d.shape

    def body(x_ref, meta_ref, g_ref, b_ref, h_ref, hb_ref):
        xb = _stream_block(pl.program_id(0), x_ref, meta_ref)
        xhat, _ = _norm(xb)
        y = xhat * g_ref[...] + b_ref[...]
        h_ref[...] = y
        hb_ref[...] = y.astype(BF16)

    row = pl.BlockSpec((BLK, d), lambda i: (i, 0))
    vec = pl.BlockSpec((1, d), lambda i: (0, 0))
    return _gcall(body, "ln_in_fwd", (nb,),
                  [pl.BlockSpec((BLK, d), lambda i: (jnp.maximum(i - 1, 0), 0)), pl.BlockSpec((N_META, d), lambda i: (0, 0)), vec, vec],
                  [row, row], [_sds((nb * BLK, d), F32), _sds((nb * BLK, d), BF16)], (x2d, meta, g, b), sem=("parallel",), side=side)


def _res_ln_fwd(h, z, g, b, side=None):
    tp, d = h.shape

    def body(h_ref, z_ref, g_ref, b_ref, r_ref, y_ref, yb_ref):
        r = DN_ALPHA * h_ref[...] + z_ref[...]
        xhat, _ = _norm(r)
        y = xhat * g_ref[...] + b_ref[...]
        r_ref[...] = r
        y_ref[...] = y
        yb_ref[...] = y.astype(BF16)

    row = pl.BlockSpec((BLK, d), lambda i: (i, 0))
    vec = pl.BlockSpec((1, d), lambda i: (0, 0))
    return _gcall(body, "res_ln1_fwd", (tp // BLK,), [row, row, vec, vec], [row, row, row],
                  [_sds((tp, d), F32), _sds((tp, d), F32), _sds((tp, d), BF16)], (h, z, g, b), sem=("parallel",), side=side)


def _final_ln_loss(h1, z2, g, b, tgt):
    tp, d = h1.shape

    def body(h_ref, z_ref, g_ref, b_ref, t_ref, dr_ref, drb_ref, loss_ref, dg_ref, db_ref):
        i = pl.program_id(0)
        r = DN_ALPHA * h_ref[...] + z_ref[...]
        xhat, rstd = _norm(r)
        y = xhat * g_ref[...] + b_ref[...]
        err = jnp.where(i >= 1, y - t_ref[...], 0.0)
        dy = err * (1.0 / d)
        dr = _ln_bwd_rows(dy, xhat, rstd, g_ref[...])
        dr_ref[...] = dr
        drb_ref[...] = dr.astype(BF16)

        @pl.when(i == 0)
        def _():
            loss_ref[...] = jnp.zeros_like(loss_ref)
            dg_ref[...] = jnp.zeros_like(dg_ref)
            db_ref[...] = jnp.zeros_like(db_ref)

        loss_ref[...] += 0.5 * jnp.sum(jnp.sum(err * err, axis=-1, keepdims=True) * (1.0 / d), axis=0, keepdims=True)
        dg_ref[...] += jnp.sum(dy * xhat, axis=0, keepdims=True)
        db_ref[...] += jnp.sum(dy, axis=0, keepdims=True)

    row = pl.BlockSpec((BLK, d), lambda i: (i, 0))
    vec = pl.BlockSpec((1, d), lambda i: (0, 0))
    return _gcall(body, "final_ln_loss", (tp // BLK,),
                  [row, row, vec, vec, pl.BlockSpec((BLK, d), lambda i: (jnp.maximum(i - 1, 0), 0))],
                  [row, row, pl.BlockSpec((8, LANES), lambda i: (0, 0)), vec, vec],
                  [_sds((tp, d), F32), _sds((tp, d), BF16), _sds((8, LANES), F32), _sds((1, d), F32), _sds((1, d), F32)],
                  (h1, z2, g, b, tgt), sem=("arbitrary",))


def _ln1_bwd(d_res, d_mm, r, g, side=None):
    tp, d = r.shape

    def body(a_ref, m_ref, r_ref, g_ref, dr_ref, drb_ref, dg_ref, db_ref):
        dy = DN_ALPHA * a_ref[...] + m_ref[...]
        xhat, rstd = _norm(r_ref[...])
        dr = _ln_bwd_rows(dy, xhat, rstd, g_ref[...])
        dr_ref[...] = dr
        drb_ref[...] = dr.astype(BF16)

        @pl.when(pl.program_id(0) == 0)
        def _():
            dg_ref[...] = jnp.zeros_like(dg_ref)
            db_ref[...] = jnp.zeros_like(db_ref)

        dg_ref[...] += jnp.sum(dy * xhat, axis=0, keepdims=True)
        db_ref[...] += jnp.sum(dy, axis=0, keepdims=True)

    row = pl.BlockSpec((BLK, d), lambda i: (i, 0))
    vec = pl.BlockSpec((1, d), lambda i: (0, 0))
    return _gcall(body, "ln1_bwd", (tp // BLK,), [row, row, row, vec], [row, row, vec, vec],
                  [_sds((tp, d), F32), _sds((tp, d), BF16), _sds((1, d), F32), _sds((1, d), F32)], (d_res, d_mm, r, g),
                  sem=("arbitrary",), side=side)


def _ln_in_bwd(d_res, d_mm, x2d, meta, g, side=None):
    seq, d = x2d.shape
    nb = d_res.shape[0] // BLK

    def body(a_ref, m_ref, x_ref, meta_ref, g_ref, gx_ref, gm_ref, dg_ref, db_ref):
        i = pl.program_id(0)
        dy = DN_ALPHA * a_ref[...] + m_ref[...]
        xhat, rstd = _norm(_stream_block(i, x_ref, meta_ref))
        dx = _ln_bwd_rows(dy, xhat, rstd, g_ref[...])
        gx_ref[...] = dx

        @pl.when(i == 0)
        def _():
            gm_ref[...] = dx[LEAD:, :]
            dg_ref[...] = jnp.zeros_like(dg_ref)
            db_ref[...] = jnp.zeros_like(db_ref)

        dg_ref[...] += jnp.sum(dy * xhat, axis=0, keepdims=True)
        db_ref[...] += jnp.sum(dy, axis=0, keepdims=True)

    row = pl.BlockSpec((BLK, d), lambda i: (i, 0))
    xrow = pl.BlockSpec((BLK, d), lambda i: (jnp.maximum(i - 1, 0), 0))
    vec = pl.BlockSpec((1, d), lambda i: (0, 0))
    met = pl.BlockSpec((N_META, d), lambda i: (0, 0))
    return _gcall(body, "ln_in_bwd", (nb,), [row, row, xrow, met, vec], [xrow, met, vec, vec],
                  [_sds((seq, d), F32), _sds((N_META, d), F32), _sds((1, d), F32), _sds((1, d), F32)], (d_res, d_mm, x2d, meta, g),
                  sem=("arbitrary",), side=side)


def _rope_tables(tp):
    half = ROPE_DIM // 2
    inv_freq = ROPE_THETA ** (-jnp.arange(half, dtype=F32) * 2.0 / ROPE_DIM)
    pos = (jnp.arange(tp) - LEAD).astype(F32)
    ang = pos[:, None] * inv_freq[None, :]
    cos, sin = jnp.cos(ang), jnp.sin(ang)
    ones = jnp.ones((tp, HEAD_DIM - ROPE_DIM), F32)
    cos_h = jnp.concatenate([cos, cos, ones], axis=1)
    sin_h = jnp.concatenate([-sin, sin, 0.0 * ones], axis=1)
    reps = LANES // HEAD_DIM
    return jnp.tile(cos_h, (1, reps)), jnp.tile(sin_h, (1, reps))


def _rope_partner(x):
    half = ROPE_DIM // 2
    lane = lax.broadcasted_iota(jnp.int32, x.shape, 1) % HEAD_DIM
    upper = jnp.where(lane < ROPE_DIM, pltpu.roll(x, half, 1), 0.0)
    return jnp.where(lane < half, pltpu.roll(x, LANES - half, 1), upper)


def _rope_fwd(proj, cos, sin, n_rot, width, side=None):
    tp = proj.shape[0]

    def body(p_ref, c_ref, s_ref, o_ref):
        c, s = c_ref[...], s_ref[...]
        for j in range(width // LANES):
            sl = slice(j * LANES, (j + 1) * LANES)
            xj = p_ref[:, sl]
            if j < n_rot:
                xj = xj * c + _rope_partner(xj) * s
            o_ref[:, sl] = xj.astype(BF16)

    tab = pl.BlockSpec((BLK, LANES), lambda i: (i, 0))
    blk = pl.BlockSpec((BLK, width), lambda i: (i, 0))
    return _gcall(body, "rope_fwd", (tp // BLK,), [blk, tab, tab], blk, _sds((tp, width), BF16), (proj, cos, sin), sem=("parallel",), side=side)


def _rope_bwd(dq, dk_cur, dk_prev, dk_meta, dv_cur, dv_prev, dv_meta, cos, sin, side=None):
    tp, aw = dq.shape
    kw = dk_cur.shape[1]
    nb = tp // BLK

    def body(dq_ref, kc_ref, kp_ref, km_ref, vc_ref, vp_ref, vm_ref, c_ref, s_ref, o_ref):
        i = pl.program_id(0)
        c, s = c_ref[...], s_ref[...]
        has_next = i + 1 < nb

        def unrot(g):
            return g * c + _rope_partner(g * s)

        def kv_sum(cur, prv, met):
            return cur[...] + jnp.where(has_next, prv[...], 0.0) + jnp.where(i == 0, met[...], 0.0)

        for j in range(aw // LANES):
            sl = slice(j * LANES, (j + 1) * LANES)
            o_ref[:, sl] = unrot(dq_ref[:, sl]).astype(BF16)
        dk = kv_sum(kc_ref, kp_ref, km_ref)
        dv = kv_sum(vc_ref, vp_ref, vm_ref)
        for j in range(kw // LANES):
            sl = slice(j * LANES, (j + 1) * LANES)
            o_ref[:, aw + j * LANES:aw + (j + 1) * LANES] = unrot(dk[:, sl]).astype(BF16)
            o_ref[:, aw + kw + j * LANES:aw + kw + (j + 1) * LANES] = dv[:, sl].astype(BF16)

    cur = pl.BlockSpec((BLK, kw), lambda i: (i, 0))
    nxt = pl.BlockSpec((BLK, kw), lambda i: (jnp.minimum(i + 1, nb - 1), 0))
    met = pl.BlockSpec((BLK, kw), lambda i: (0, 0))
    tab = pl.BlockSpec((BLK, LANES), lambda i: (i, 0))
    return _gcall(body, "rope_bwd", (nb,), [pl.BlockSpec((BLK, aw), lambda i: (i, 0)), cur, nxt, met, cur, nxt, met, tab, tab],
                  pl.BlockSpec((BLK, aw + 2 * kw), lambda i: (i, 0)), _sds((tp, aw + 2 * kw), BF16),
                  (dq, dk_cur, dk_prev, dk_meta, dv_cur, dv_prev, dv_meta, cos, sin), sem=("parallel",), side=side)


def _attn_probs(n, q_ref, km_ref, kp_ref, kc_ref, sink_ref, grp):
    scale = HEAD_DIM ** -0.5
    qs = q_ref[...].reshape(grp * BLK, HEAD_DIM)
    kcat = jnp.concatenate([km_ref[...], kp_ref[...], kc_ref[...]], axis=0)
    s = lax.dot_general(qs, kcat, _DOT_DIMS["nt"], preferred_element_type=F32) * scale
    s = s.reshape(grp, BLK, 3 * BLK)
    r = lax.broadcasted_iota(jnp.int32, (1, BLK, 3 * BLK), 1)
    j = lax.broadcasted_iota(jnp.int32, (1, BLK, 3 * BLK), 2)
    q_idx = n * BLK + r
    meta_ok = (j >= LEAD) & (j < BLK) & (q_idx >= j)
    k_idx = (n - 1) * BLK + (j - BLK)
    diff = q_idx - k_idx
    band_ok = (j >= BLK) & (diff >= 0) & (diff < BLK) & (k_idx >= LEAD + N_META)
    s = jnp.where(meta_ok | band_ok, s, NEG_INF)
    sink = sink_ref[...]
    m = jnp.maximum(jnp.max(s, axis=-1, keepdims=True), sink)
    p = jnp.exp(s - m)
    e_sink = jnp.exp(sink - m)
    inv = 1.0 / (jnp.sum(p, axis=-1, keepdims=True) + e_sink)
    return qs, kcat, p * inv, e_sink * inv


def _attn_specs(grp):
    qspec = pl.BlockSpec((grp, BLK, HEAD_DIM), lambda kk, n: (kk, n, 0))
    kmeta = pl.BlockSpec((None, BLK, HEAD_DIM), lambda kk, n: (kk, 0, 0))
    kprev = pl.BlockSpec((None, BLK, HEAD_DIM), lambda kk, n: (kk, jnp.maximum(n - 1, 0), 0))
    kcur = pl.BlockSpec((None, BLK, HEAD_DIM), lambda kk, n: (kk, n, 0))
    sink = pl.BlockSpec((None, grp, BLK, 1), lambda kk, n: (kk, 0, 0, 0))
    return qspec, kmeta, kprev, kcur, sink


def _attn_fwd(q_hm, k_hm, v_hm, sink4, side=None):
    nq, tp, _ = q_hm.shape
    nkv = k_hm.shape[0]
    grp = nq // nkv

    def body(q_ref, km_ref, kp_ref, kc_ref, vm_ref, vp_ref, vc_ref, sink_ref, o_ref):
        n = pl.program_id(1)
        _, _, pn, _ = _attn_probs(n, q_ref, km_ref, kp_ref, kc_ref, sink_ref, grp)
        vcat = jnp.concatenate([vm_ref[...], vp_ref[...], vc_ref[...]], axis=0)
        o = jnp.dot(pn.reshape(grp * BLK, 3 * BLK).astype(BF16), vcat, preferred_element_type=F32)
        o_ref[...] = o.reshape(grp, BLK, HEAD_DIM).astype(BF16)

    qspec, kmeta, kprev, kcur, sink = _attn_specs(grp)
    return _gcall(body, "attn_fwd", (nkv, tp // BLK), [qspec, kmeta, kprev, kcur, kmeta, kprev, kcur, sink], qspec,
                  _sds((nq, tp, HEAD_DIM), BF16), (q_hm, k_hm, k_hm, k_hm, v_hm, v_hm, v_hm, sink4), sem=("parallel", "parallel"), side=side)


def _attn_bwd(q_hm, k_hm, v_hm, sink4, do_hm, side=None):
    nq, tp, _ = q_hm.shape
    nkv = k_hm.shape[0]
    grp = nq // nkv
    scale = HEAD_DIM ** -0.5

    def body(q_ref, km_ref, kp_ref, kc_ref, vm_ref, vp_ref, vc_ref, sink_ref, do_ref,
             dq_ref, dkc_ref, dkp_ref, dkm_ref, dvc_ref, dvp_ref, dvm_ref, dsk_ref):
        n = pl.program_id(1)
        qs, kcat, pn, p_sink = _attn_probs(n, q_ref, km_ref, kp_ref, kc_ref, sink_ref, grp)
        vcat = jnp.concatenate([vm_ref[...], vp_ref[...], vc_ref[...]], axis=0)
        pn2 = pn.reshape(grp * BLK, 3 * BLK)
        pnb = pn2.astype(BF16)
        dob = do_ref[...].reshape(grp * BLK, HEAD_DIM).astype(BF16)
        dp = lax.dot_general(dob, vcat, _DOT_DIMS["nt"], preferred_element_type=F32)
        delta = jnp.sum(pn2 * dp, axis=-1, keepdims=True)
        ds = (pn2 * (dp - delta) * scale).astype(BF16)
        dq_ref[...] = jnp.dot(ds, kcat, preferred_element_type=F32).reshape(grp, BLK, HEAD_DIM)
        dk = lax.dot_general(ds, qs, _DOT_DIMS["tn"], preferred_element_type=F32)
        dv = lax.dot_general(pnb, dob, _DOT_DIMS["tn"], preferred_element_type=F32)
        dkp_ref[...] = dk[BLK:2 * BLK]
        dkc_ref[...] = dk[2 * BLK:]
        dvp_ref[...] = dv[BLK:2 * BLK]
        dvc_ref[...] = dv[2 * BLK:]
        dsk = -jnp.sum(p_sink * delta.reshape(grp, BLK, 1), axis=1, keepdims=True)

        @pl.when(n == 0)
        def _():
            dkm_ref[...] = jnp.zeros_like(dkm_ref)
            dvm_ref[...] = jnp.zeros_like(dvm_ref)
            dsk_ref[...] = jnp.zeros_like(dsk_ref)

        dkm_ref[...] += dk[:BLK]
        dvm_ref[...] += dv[:BLK]
        dsk_ref[...] += jnp.broadcast_to(dsk, (grp, BLK, 1))

    qspec, kmeta, kprev, kcur, sink = _attn_specs(grp)
    kv_shape = _sds((nkv, tp, HEAD_DIM), F32)
    meta_shape = _sds((nkv, BLK, HEAD_DIM), F32)
    return _gcall(body, "attn_bwd", (nkv, tp // BLK), [qspec, kmeta, kprev, kcur, kmeta, kprev, kcur, sink, qspec],
                  [qspec, kcur, kcur, kmeta, kcur, kcur, kmeta, sink],
                  [_sds((nq, tp, HEAD_DIM), F32), kv_shape, kv_shape, meta_shape, kv_shape, kv_shape, meta_shape,
                   _sds((nkv, grp, BLK, 1), F32)],
                  (q_hm, k_hm, k_hm, k_hm, v_hm, v_hm, v_hm, sink4, do_hm), sem=("parallel", "arbitrary"), side=side)


def _pool_coef(row_blk, col_blk, w):
    r = lax.broadcasted_iota(jnp.int32, (BLK, BLK), 0)
    j = lax.broadcasted_iota(jnp.int32, (BLK, BLK), 1)
    t = row_blk * BLK + r - LEAD
    tj = col_blk * BLK + j - LEAD
    dist = t - tj
    inwin = (dist >= 0) & (dist < w) & (tj >= 0)
    count = jnp.maximum(jnp.minimum(t + 1, w), 1).astype(F32)
    return jnp.where(inwin, 1.0 / count, 0.0) - jnp.where((dist == 0) & (tj >= 0), 1.0, 0.0)


def _pool_fwd(proj, wg, scale, u_off, pool_w, side=None):
    tp = proj.shape[0]
    gw = pool_w // N_GRP
    nb = tp // BLK
    cb = u_off // gw

    def body(up_ref, uc_ref, wg_ref, sc_ref, pooled_ref, mx_ref, pm_ref):
        n, g = pl.program_id(0), pl.program_id(1)
        w = jnp.left_shift(2, g)
        pooled = (jnp.dot(_pool_coef(n, n - 1, w), up_ref[...], precision=HI, preferred_element_type=F32)
                  + jnp.dot(_pool_coef(n, n, w), uc_ref[...], precision=HI, preferred_element_type=F32))
        pb = pooled.astype(BF16)
        mx = jnp.dot(pb, wg_ref[...], preferred_element_type=F32)
        pooled_ref[...] = pb
        mx_ref[...] = mx
        pm_ref[...] = (mx * sc_ref[...]).astype(BF16)

    blk = pl.BlockSpec((BLK, gw), lambda n, g: (n, g))
    return _gcall(body, "pool_fwd", (nb, N_GRP),
                  [pl.BlockSpec((BLK, gw), lambda n, g: (jnp.maximum(n - 1, 0), cb + g)),
                   pl.BlockSpec((BLK, gw), lambda n, g: (n, cb + g)),
                   pl.BlockSpec((None, gw, gw), lambda n, g: (g, 0, 0)),
                   pl.BlockSpec((1, gw), lambda n, g: (0, g))],
                  [blk, blk, blk], [_sds((tp, pool_w), BF16), _sds((tp, pool_w), F32), _sds((tp, pool_w), BF16)],
                  (proj, proj, wg, scale), sem=("parallel", "parallel"), side=side)


def _pool_bwd_mix(d_pm, mx, pooled, wg, scale, side=None):
    tp, pool_w = d_pm.shape
    gw = pool_w // N_GRP

    def body(d_ref, mx_ref, pl_ref, wg_ref, sc_ref, dp_ref, dwg_ref, dsc_ref):
        n = pl.program_id(1)
        d = d_ref[...]
        dmx = (d * sc_ref[...]).astype(BF16)
        dp_ref[...] = lax.dot_general(dmx, wg_ref[...], _DOT_DIMS["nt"], preferred_element_type=F32)

        @pl.when(n == 0)
        def _():
            dwg_ref[...] = jnp.zeros_like(dwg_ref)
            dsc_ref[...] = jnp.zeros_like(dsc_ref)

        dwg_ref[...] += lax.dot_general(pl_ref[...], dmx, _DOT_DIMS["tn"], preferred_element_type=F32)
        dsc_ref[...] += jnp.sum(d * mx_ref[...], axis=0, keepdims=True)

    blk = pl.BlockSpec((BLK, gw), lambda g, n: (n, g))
    wspec = pl.BlockSpec((None, gw, gw), lambda g, n: (g, 0, 0))
    sspec = pl.BlockSpec((1, gw), lambda g, n: (0, g))
    return _gcall(body, "pool_bwd_mix", (N_GRP, tp // BLK), [blk, blk, blk, wspec, sspec], [blk, wspec, sspec],
                  [_sds((tp, pool_w), F32), _sds((N_GRP, gw, gw), F32), _sds((1, pool_w), F32)], (d_pm, mx, pooled, wg, scale),
                  sem=("parallel", "arbitrary"), side=side)


def _pool_bwd_band(dp, side=None):
    tp, pool_w = dp.shape
    gw = pool_w // N_GRP
    nb = tp // BLK

    def body(dc_ref, dn_ref, du_ref):
        n, g = pl.program_id(0), pl.program_id(1)
        w = jnp.left_shift(2, g)
        dnext = jnp.where(n + 1 < nb, dn_ref[...], 0.0)
        du = (lax.dot_general(_pool_coef(n, n, w), dc_ref[...], _DOT_DIMS["tn"], precision=HI, preferred_element_type=F32)
              + lax.dot_general(_pool_coef(n + 1, n, w), dnext, _DOT_DIMS["tn"], precision=HI, preferred_element_type=F32))
        du_ref[...] = du.astype(BF16)

    blk = pl.BlockSpec((BLK, gw), lambda n, g: (n, g))
    return _gcall(body, "pool_bwd_band", (nb, N_GRP), [blk, pl.BlockSpec((BLK, gw), lambda n, g: (jnp.minimum(n + 1, nb - 1), g))],
                  blk, _sds((tp, pool_w), BF16), (dp, dp), sem=("parallel", "parallel"), side=side)


def _gate_tiles(tp, d, g_off):
    tc = _pick(math.gcd(g_off, d), 512, 256, 128)
    tr = _pick(tp, 384, 128)
    return tr, tc


def _mix_fwd(proj, b_gate, a_out, p_out, g_off, side=None):
    tp, d = a_out.shape
    tr, tc = _gate_tiles(tp, d, g_off)
    c0, c1 = g_off // tc, (g_off + d) // tc

    def body(g0_ref, g1_ref, b_ref, a_ref, p_ref, o_ref):
        g0 = jax.nn.sigmoid(g0_ref[...] + b_ref[0:1, :])
        g1 = jax.nn.sigmoid(g1_ref[...] + b_ref[1:2, :])
        o_ref[...] = (g0 * a_ref[...] + g1 * p_ref[...]).astype(BF16)

    blk = pl.BlockSpec((tr, tc), lambda i, j: (i, j))
    return _gcall(body, "mix_fwd", (tp // tr, d // tc),
                  [pl.BlockSpec((tr, tc), lambda i, j: (i, c0 + j)), pl.BlockSpec((tr, tc), lambda i, j: (i, c1 + j)),
                   pl.BlockSpec((2, tc), lambda i, j: (0, j)), blk, blk],
                  blk, _sds((tp, d), BF16), (proj, proj, b_gate, a_out, p_out), sem=("parallel", "parallel"), side=side)


def _mix_bwd(proj, b_gate, a_out, p_out, d_mixed, g_off, side=None):
    tp, d = a_out.shape
    tr, tc = _gate_tiles(tp, d, g_off)
    c0, c1 = g_off // tc, (g_off + d) // tc

    def body(g0_ref, g1_ref, b_ref, a_ref, p_ref, d_ref, da_ref, dp_ref, dl0_ref, dl1_ref, db_ref):
        g0 = jax.nn.sigmoid(g0_ref[...] + b_ref[0:1, :])
        g1 = jax.nn.sigmoid(g1_ref[...] + b_ref[1:2, :])
        dm = d_ref[...]
        da_ref[...] = (dm * g0).astype(BF16)
        dp_ref[...] = (dm * g1).astype(BF16)
        dl0 = dm * a_ref[...] * g0 * (1.0 - g0)
        dl1 = dm * p_ref[...] * g1 * (1.0 - g1)
        dl0_ref[...] = dl0.astype(BF16)
        dl1_ref[...] = dl1.astype(BF16)

        @pl.when(pl.program_id(1) == 0)
        def _():
            db_ref[...] = jnp.zeros_like(db_ref)

        db_ref[...] += jnp.concatenate([jnp.sum(dl0, axis=0, keepdims=True), jnp.sum(dl1, axis=0, keepdims=True)], axis=0)

    blk = pl.BlockSpec((tr, tc), lambda j, i: (i, j))
    big = _sds((tp, d), BF16)
    return _gcall(body, "mix_bwd", (d // tc, tp // tr),
                  [pl.BlockSpec((tr, tc), lambda j, i: (i, c0 + j)), pl.BlockSpec((tr, tc), lambda j, i: (i, c1 + j)),
                   pl.BlockSpec((2, tc), lambda j, i: (0, j)), blk, blk, blk],
                  [blk, blk, blk, blk, pl.BlockSpec((2, tc), lambda j, i: (0, j))], [big, big, big, big, _sds((2, d), F32)],
                  (proj, proj, b_gate, a_out, p_out, d_mixed), sem=("parallel", "arbitrary"), side=side)


SWIGLU_ROWS = 64


def _swiglu_fwd(ff, side=None):
    tp, f2 = ff.shape
    f = f2 // 2
    tr = _pick(tp, SWIGLU_ROWS)

    def body(x_ref, o_ref):
        gate, up = x_ref[:, :f], x_ref[:, f:]
        o_ref[...] = (gate * jax.nn.sigmoid(gate) * up).astype(BF16)

    return _gcall(body, "swiglu_fwd", (tp // tr,), [pl.BlockSpec((tr, f2), lambda i: (i, 0))], pl.BlockSpec((tr, f), lambda i: (i, 0)),
                  _sds((tp, f), BF16), (ff,), sem=("parallel",), side=side)


def _swiglu_bwd(ff, d_act, side=None):
    tp, f2 = ff.shape
    f = f2 // 2
    tr = _pick(tp, SWIGLU_ROWS)

    def body(x_ref, d_ref, o_ref):
        gate, up = x_ref[:, :f], x_ref[:, f:]
        d = d_ref[...]
        sg = jax.nn.sigmoid(gate)
        silu = gate * sg
        o_ref[:, :f] = (d * up * (sg + silu * (1.0 - sg))).astype(BF16)
        o_ref[:, f:] = (d * silu).astype(BF16)

    return _gcall(body, "swiglu_bwd", (tp // tr,), [pl.BlockSpec((tr, f2), lambda i: (i, 0)), pl.BlockSpec((tr, f), lambda i: (i, 0))],
                  pl.BlockSpec((tr, f2), lambda i: (i, 0)), _sds((tp, f2), BF16), (ff, d_act), sem=("parallel",), side=side)


def _tile2(rows, cols, max_bytes=3 << 20):
    tc = _pick(cols, 1024, 640, 512)
    for tr in (512, 344, 256, 128, 64, 32, 16, 8):
        if rows % tr == 0 and tr * tc * 4 <= max_bytes:
            return tr, tc
    return rows, tc


def _cast_into_full(w, win, who, name, side=None):
    r, c = w.shape
    tr, tc = _tile2(r, c)

    def body(who_ref, x_ref, o_ref):
        o_ref[...] = x_ref[...].astype(BF16)

    if win.kind == "col":
        own = pl.BlockSpec((tr, tc), lambda i, j, who_ref: (i, who_ref[1] * (c // tc) + j))
    else:
        own = pl.BlockSpec((tr, tc), lambda i, j, who_ref: (who_ref[1] * (r // tr) + i, j))
    return _gcall(body, name, (r // tr, c // tc), [pl.BlockSpec((tr, tc), lambda i, j, who_ref: (i, j))], own,
                  _sds(win.full_shape, BF16), (w,), sem=("parallel", "parallel"), side=side, prefetch=(who,))


def _adamw(w, g, m, v, name, side=None):
    r, c = w.shape
    tr, tc = _tile2(r, c, 1 << 20)

    def body(w_ref, g_ref, m_ref, v_ref, d_ref, nm_ref, nv_ref):
        gg = g_ref[...]
        nm = ADAM_B1 * m_ref[...] + (1.0 - ADAM_B1) * gg
        nv = ADAM_B2 * v_ref[...] + (1.0 - ADAM_B2) * jnp.square(gg)
        m_hat = nm / (1.0 - ADAM_B1 ** ADAM_STEP)
        v_hat = nv / (1.0 - ADAM_B2 ** ADAM_STEP)
        d_ref[...] = -ADAM_LR * (m_hat / (jnp.sqrt(v_hat) + ADAM_EPS) + ADAM_WD * w_ref[...])
        nm_ref[...] = nm
        nv_ref[...] = nv

    blk = pl.BlockSpec((tr, tc), lambda i, j: (i, j))
    shp = _sds((r, c), F32)
    return _gcall(body, name, (r // tr, c // tc), [blk] * 4, [blk] * 3, [shp] * 3, (w, g, m, v), sem=("parallel", "parallel"), side=side)


def _piece_block_index(win, tr, tc):
    r, c = win.shard_shape
    if win.kind == "col":
        return lambda s, h, i, j: (win.row0 // tr + h * (win.half // tr) + i, s * (c // tc) + j)
    return lambda s, h, i, j: ((s * r + win.row0) // tr + h * (win.half // tr) + i, j)


def _chip_sum(g, who):
    pr, pc = g.win.piece_shape
    tr, tc = _tile2(pr, pc)
    full_idx = _piece_block_index(g.src, tr, tc)

    def body(who_ref, g_ref, o_ref, out_ref):
        out_ref[...] = (g_ref[...] + o_ref[...]).astype(BF16)

    slot = pl.BlockSpec((None, tr, tc), lambda s, i, j, who_ref: (s, i, j))
    g.chip = pl.pallas_call(
        body,
        name="chip_sum_" + g.key.replace("@", "_"),
        grid_spec=pltpu.PrefetchScalarGridSpec(
            num_scalar_prefetch=1,
            grid=(N_CHIPS, pr // tr, pc // tc),
            in_specs=[pl.BlockSpec((tr, tc), lambda s, i, j, who_ref: full_idx(s, who_ref[0], i, j)), slot],
            out_specs=slot,
        ),
        out_shape=_sds((N_CHIPS, pr, pc), BF16),
        compiler_params=_cparams(("parallel", "parallel", "parallel")),
    )(who, g.grad, g.other)


def _final_sum(g, shard, who):
    win = g.win
    pr, pc = win.piece_shape
    tr, tc = _tile2(pr, pc)
    full_idx = _piece_block_index(g.src, tr, tc)
    has_prev = shard.arr is not None

    def body(who_ref, g_ref, o_ref, l1_ref, l2_ref, l3_ref, *rest):
        out_ref = rest[-1]
        acc = g_ref[...] + o_ref[...]
        for l_ref in (l1_ref, l2_ref, l3_ref):
            acc = acc + l_ref[...].astype(F32)
        out_ref[...] = acc

    def landed_spec(k):
        return pl.BlockSpec((None, tr, tc), lambda i, j, who_ref: (who_ref[1 + k], i, j))

    in_specs = [pl.BlockSpec((tr, tc), lambda i, j, who_ref: full_idx(who_ref[1], who_ref[0], i, j)),
                pl.BlockSpec((None, tr, tc), lambda i, j, who_ref: (who_ref[1], i, j)),
                landed_spec(1), landed_spec(2), landed_spec(3)]
    args = [who, g.grad, g.other, g.landed, g.landed, g.landed]
    if has_prev:
        in_specs.append(ANY)
        args.append(shard.arr)
    shard.arr = pl.pallas_call(
        body,
        name="final_sum_" + g.key.replace("@", "_"),
        grid_spec=pltpu.PrefetchScalarGridSpec(
            num_scalar_prefetch=1,
            grid=(pr // tr, pc // tc),
            in_specs=in_specs,
            out_specs=pl.BlockSpec((tr, tc), lambda i, j, who_ref: (win.row0 // tr + who_ref[0] * (pr // tr) + i, j)),
        ),
        out_shape=_sds(win.shard_shape, F32),
        input_output_aliases={6: 0} if has_prev else {},
        compiler_params=_cparams(("parallel", "parallel")),
    )(*args)


def _gather_small(packed):
    r, c = packed.shape

    def body(in_ref, out_ref, send_sems, recv_sems):
        x, y, c_ = _coords()
        s_me = 2 * x + y
        out_ref[s_me] = in_ref[...]
        copies = []
        for j, (ox, oy) in enumerate(_other_chips(x, y)):
            cp = pltpu.make_async_remote_copy(src_ref=in_ref, dst_ref=out_ref.at[s_me], send_sem=send_sems.at[j],
                                              recv_sem=recv_sems.at[j], device_id=(ox, oy, c_), device_id_type=MESH)
            cp.start()
            copies.append(cp)
        for j, (ox, oy) in enumerate(_other_chips(x, y)):
            copies[j].wait_send()
            pltpu.make_async_remote_copy(src_ref=in_ref, dst_ref=out_ref.at[2 * ox + oy], send_sem=send_sems.at[j],
                                         recv_sem=recv_sems.at[j], device_id=(x, y, c_), device_id_type=MESH).wait_recv()

    return pl.pallas_call(
        body,
        name="gather_small",
        in_specs=[VMEM_FULL],
        out_specs=VMEM_FULL,
        out_shape=_sds((N_CHIPS, r, c), F32),
        scratch_shapes=[pltpu.SemaphoreType.DMA((3,)), pltpu.SemaphoreType.DMA((3,))],
    )(packed)


def _all_reduce_small(packed):
    r, c = packed.shape

    def body(in_ref, out_ref, slots, send_sems, recv_sems):
        x, y, c_ = _coords()
        me = 4 * x + 2 * y + c_
        slots[me] = in_ref[...]
        copies = []
        for k in range(1, N_DEV):
            peer = me ^ k
            cp = pltpu.make_async_remote_copy(src_ref=in_ref, dst_ref=slots.at[me], send_sem=send_sems.at[k - 1],
                                              recv_sem=recv_sems.at[k - 1],
                                              device_id=(peer // 4, (peer // 2) % 2, peer % 2), device_id_type=MESH)
            cp.start()
            copies.append(cp)
        for k in range(1, N_DEV):
            copies[k - 1].wait_send()
            pltpu.make_async_remote_copy(src_ref=in_ref, dst_ref=slots.at[me ^ k], send_sem=send_sems.at[k - 1],
                                         recv_sem=recv_sems.at[k - 1], device_id=(x, y, c_), device_id_type=MESH).wait_recv()
        acc = slots[0]
        for d in range(1, N_DEV):
            acc = acc + slots[d]
        out_ref[...] = acc

    return pl.pallas_call(
        body,
        name="all_reduce_small",
        in_specs=[VMEM_FULL],
        out_specs=VMEM_FULL,
        out_shape=_sds((r, c), F32),
        scratch_shapes=[pltpu.VMEM((N_DEV, r, c), F32), pltpu.SemaphoreType.DMA((N_DEV - 1,)), pltpu.SemaphoreType.DMA((N_DEV - 1,))],
    )(packed)


def _rows_of(a, width):
    flat = a.reshape(-1)
    n = -(-flat.shape[0] // width) * width
    return jnp.pad(flat, (0, n - flat.shape[0])).reshape(-1, width)


def _pad_rows(a, mult=8):
    n = -(-a.shape[0] // mult) * mult
    return jnp.pad(a, ((0, n - a.shape[0]), (0, 0)))


def _heads_major(a, nh):
    tp = a.shape[0]
    return a.reshape(tp, nh, HEAD_DIM).transpose(1, 0, 2)


def _heads_minor(a):
    nh, tp, hd = a.shape
    return a.transpose(1, 0, 2).reshape(tp, nh * hd)


def kernel(x, meta_tokens, ln_in_g, ln_in_b, w_in, b_gate, attn_sinks, w_attn_up, w_pool_grp, pool_scale, w_pool_up, w_out, ln1_g, ln1_b, w_ffn_in, w_ffn_down, ln2_g, ln2_b, loss_target, m_meta_tokens, m_ln_in_g, m_ln_in_b, m_w_in, m_b_gate, m_attn_sinks, m_w_attn_up, m_w_pool_grp, m_pool_scale, m_w_pool_up, m_w_out, m_ln1_g, m_ln1_b, m_w_ffn_in, m_w_ffn_down, m_ln2_g, m_ln2_b, v_meta_tokens, v_ln_in_g, v_ln_in_b, v_w_in, v_b_gate, v_attn_sinks, v_w_attn_up, v_w_pool_grp, v_pool_scale, v_w_pool_up, v_w_out, v_ln1_g, v_ln1_b, v_w_ffn_in, v_w_ffn_down, v_ln2_g, v_ln2_b):
    seq, d = x.shape[1], x.shape[2]
    tp = LEAD + N_META + seq
    nb = tp // BLK
    nq = attn_sinks.shape[1]
    grp = nq // N_KV
    attn_w = nq * HEAD_DIM
    kv_w = N_KV * HEAD_DIM
    qkv_w = attn_w + 2 * kv_w
    pool_w = pool_scale.shape[1]
    gw = pool_w // N_GRP
    g_off = qkv_w + pool_w
    dc = d // N_CHIPS
    cx, cy, cc = _coords()
    s_me = 2 * cx + cy
    who = jnp.stack([cc, s_me, (s_me + 1) % N_CHIPS, (s_me + 2) % N_CHIPS, (s_me + 3) % N_CHIPS]).astype(jnp.int32)

    names = ["w_in", "w_attn_up", "w_pool_grp", "w_pool_up", "w_out", "w_ffn_in", "w_ffn_down"]
    kinds = dict(w_in="col", w_attn_up="col", w_pool_grp="row", w_pool_up="col", w_out="row", w_ffn_in="col", w_ffn_down="row")
    grp_shard = (N_GRP * (gw // N_CHIPS), gw)
    big_w = dict(w_in=w_in[0], w_attn_up=w_attn_up[0], w_pool_grp=w_pool_grp[0].reshape(grp_shard), w_pool_up=w_pool_up[0],
                 w_out=w_out[0], w_ffn_in=w_ffn_in[0], w_ffn_down=w_ffn_down[0])
    big_m = dict(w_in=m_w_in[0], w_attn_up=m_w_attn_up[0], w_pool_grp=m_w_pool_grp[0].reshape(grp_shard), w_pool_up=m_w_pool_up[0],
                 w_out=m_w_out[0], w_ffn_in=m_w_ffn_in[0], w_ffn_down=m_w_ffn_down[0])
    big_v = dict(w_in=v_w_in[0], w_attn_up=v_w_attn_up[0], w_pool_grp=v_w_pool_grp[0].reshape(grp_shard), w_pool_up=v_w_pool_up[0],
                 w_out=v_w_out[0], w_ffn_in=v_w_ffn_in[0], w_ffn_down=v_w_ffn_down[0])
    small_rows = _pad_rows(jnp.concatenate([meta_tokens, b_gate[0]], axis=0))
    gathered = _gather_small(small_rows)
    gathered = gathered.transpose(1, 0, 2).reshape(small_rows.shape[0], d)
    meta_full, b_gate_full = gathered[:N_META], gathered[N_META:N_META + 2]

    W = {}

    def cast(n, side=None):
        win = _Win(kinds[n], big_w[n].shape)
        W[n] = _Weight(n, win, _cast_into_full(big_w[n], win, who, "cast_" + n, side=side))

    def whole(*ns):
        return [(W[n], W[n].win) for n in ns]

    def legs(first=(), second=(), third=()):
        ops = [_GatherD2d(third, DIAGONAL)] if third else []
        ops += [_GatherRing(second), _GatherD2d(second, NEIGHBOURS)] if second else []
        ops += [_GatherIci(first)] if first else []
        return _Side(ops)

    mid = ("w_attn_up", "w_pool_grp", "w_pool_up", "w_out")
    for n in ("w_in",) + mid:
        cast(n)
    cast("w_ffn_in", side=legs(first=whole("w_in")))
    cast("w_ffn_down", side=legs(second=whole("w_in")))
    wins = {n: W[n].win for n in names}
    ffn_in_parts = [(W["w_ffn_in"], win) for win in W["w_ffn_in"].win.split(4)]
    x2d, tgt2d = x[0], loss_target[0]
    g_in, b_in = ln_in_g.reshape(1, d), ln_in_b.reshape(1, d)
    h0, h0b = _ln_in_fwd(x2d, meta_full, g_in, b_in, nb, side=legs(first=whole(*mid), third=whole("w_in")))
    proj = _mm(h0b, W["w_in"].full, "nn", F32, 1408, 512, 4096, "mm_proj", side=legs(first=ffn_in_parts[0:3], second=whole(*mid)))
    cos, sin = _rope_tables(tp)
    n_rot = (attn_w + kv_w) // LANES
    qkv = _rope_fwd(proj, cos, sin, n_rot, qkv_w, side=legs(third=whole(*mid)))
    q_hm = _heads_major(qkv[:, :attn_w], nq)
    k_hm = _heads_major(qkv[:, attn_w:attn_w + kv_w], N_KV)
    v_hm = _heads_major(qkv[:, attn_w + kv_w:], N_KV)
    sink4 = jnp.broadcast_to(attn_sinks.reshape(N_KV, grp, 1, 1), (N_KV, grp, BLK, 1))
    o_hm = _attn_fwd(q_hm, k_hm, v_hm, sink4, side=legs(first=ffn_in_parts[3:4], second=ffn_in_parts[0:3]))
    o = _heads_minor(o_hm)
    wf_grp = W["w_pool_grp"].full.reshape(N_CHIPS, N_GRP, gw // N_CHIPS, gw).transpose(1, 0, 2, 3).reshape(N_GRP, gw, gw)
    pooled, mx, pm = _pool_fwd(proj, wf_grp, pool_scale, qkv_w, pool_w, side=legs(second=ffn_in_parts[3:4], third=ffn_in_parts[0:3]))
    a_out = _mm(o, W["w_attn_up"].full, "nn", F32, 1408, 1024, 2048, "mm_attn_up", side=legs(third=ffn_in_parts[3:4]))
    p_out = _mm(pm, W["w_pool_up"].full, "nn", F32, 1408, 1024, 2048, "mm_pool_up")
    mixed = _mix_fwd(proj, b_gate_full, a_out, p_out, g_off)
    z1 = _mm(mixed, W["w_out"].full, "nn", F32, 1408, 512, 4096, "mm_out", side=legs(first=whole("w_ffn_down")))
    r1, h1, h1b = _res_ln_fwd(h0, z1, ln1_g, ln1_b, side=legs(second=whole("w_ffn_down")))
    ff = _mm(h1b, W["w_ffn_in"].full, "nn", F32, 1408, 512, 4096, "mm_ffn_in", side=legs(third=whole("w_ffn_down")))
    act = _swiglu_fwd(ff)
    wf_down = W["w_ffn_down"].full
    z2 = _mm(act, wf_down, "nn", F32, 704, 1024, 5504, "mm_ffn_down")
    d_r2, d_r2b, loss_tile, dg2, db2 = _final_ln_loss(h1, z2, ln2_g, ln2_b, tgt2d)

    S = {n: _Shard(n) for n in names}

    def grads_of(name, grad, parts=1):
        return [_Grad(name, win, grad) for win in wins[name].split(parts)]

    def sibling(gs):
        return _ReduceSibling(gs)

    def chips(gs):
        for g in gs:
            _chip_sum(g, who)
        return _ReduceChips(gs)

    def share(gs):
        for g in gs:
            _final_sum(g, S[g.name], who)
        return _ShareReduced([(S[g.name], g.win) for g in gs])

    g6 = grads_of("w_ffn_down", _mm(act, d_r2b, "tn", F32, 256, 1024, tp, "mm_gw_ffn_down", j_outer=True))
    d_act = _mm(d_r2b, wf_down, "nt", F32, 1408, 256, 4096, "mm_d_act", side=_Side([sibling(g6)]))
    d_ff = _swiglu_bwd(ff, d_act)
    g5 = grads_of("w_ffn_in", _mm(h1b, d_ff, "tn", F32, 1024, 512, tp, "mm_gw_ffn_in", side=_Side([chips(g6)])), parts=2)
    d_h1_mm = _mm(d_ff, W["w_ffn_in"].full, "nt", F32, 704, 1024, 5504, "mm_d_h1", side=_Side([share(g6), sibling(g5)]))
    d_r1, d_r1b, dg1, db1 = _ln1_bwd(d_r2, d_h1_mm, r1, ln1_g)
    g4 = grads_of("w_out", _mm(mixed, d_r1b, "tn", F32, 1024, 512, tp, "mm_gw_out"))
    d_mixed = _mm(d_r1b, W["w_out"].full, "nt", F32, 1408, 512, 4096, "mm_d_mixed", side=_Side([sibling(g4)]))
    d_a, d_p, d_gl0, d_gl1, d_bgate = _mix_bwd(proj, b_gate_full, a_out, p_out, d_mixed, g_off, side=_Side([chips(g4)]))
    g1 = grads_of("w_attn_up", _mm(o, d_a, "tn", F32, 1024, 512, tp, "mm_gw_attn_up", side=_Side([share(g4)])))
    d_o = _mm(d_a, W["w_attn_up"].full, "nt", F32, 1408, 512, 4096, "mm_d_o", side=_Side([sibling(g1)]))
    g3 = grads_of("w_pool_up", _mm(pm, d_p, "tn", F32, 1024, 512, tp, "mm_gw_pool_up", side=_Side([chips(g1)])))
    d_pm = _mm(d_p, W["w_pool_up"].full, "nt", F32, 1408, 512, 4096, "mm_d_pm", side=_Side([sibling(g3)]))
    d_pooled, gw_grp, d_scale = _pool_bwd_mix(d_pm, mx, pooled, wf_grp, pool_scale, side=_Side([chips(g3), share(g1)]))
    gw_grp_sm = gw_grp.reshape(N_GRP, N_CHIPS, gw // N_CHIPS, gw).transpose(1, 0, 2, 3).reshape(N_CHIPS * grp_shard[0], gw)
    g2 = grads_of("w_pool_grp", gw_grp_sm)
    d_u = _pool_bwd_band(d_pooled, side=_Side([sibling(g2), share(g3)]))
    dq_hm, dk_cur, dk_prev, dk_meta, dv_cur, dv_prev, dv_meta, d_sink = _attn_bwd(
        q_hm, k_hm, v_hm, sink4, _heads_major(d_o, nq), side=_Side([chips(g5[0:1] + g2)]))
    d_qkv = _rope_bwd(_heads_minor(dq_hm), _heads_minor(dk_cur), _heads_minor(dk_prev), _heads_minor(dk_meta),
                      _heads_minor(dv_cur), _heads_minor(dv_prev), _heads_minor(dv_meta), cos, sin, side=_Side([share(g2)]))
    d_proj = jnp.concatenate([d_qkv, d_u, d_gl0, d_gl1], axis=1)
    lo, hi = wins["w_in"].split(2)
    g0a = [_Grad("w_in", lo, _mm(h0b[:, :d // 2], d_proj, "tn", F32, 1024, 512, tp, "mm_gw_in_lo",
                                 side=_Side([chips(g5[1:2]), share(g5[0:1])])), lo.row0)]
    g0b = [_Grad("w_in", hi, _mm(h0b[:, d // 2:], d_proj, "tn", F32, 1024, 512, tp, "mm_gw_in_hi",
                                 side=_Side([sibling(g0a), share(g5[1:2])])), hi.row0)]
    d_h0_mm = _mm(d_proj, W["w_in"].full, "nt", F32, 1408, 1024, 2560, "mm_d_h0", side=_Side([chips(g0a), sibling(g0b)]))
    grad_x2d, d_meta, dg_in, db_in = _ln_in_bwd(d_r1, d_h0_mm, x2d, meta_full, g_in, side=_Side([chips(g0b), share(g0a)]))

    small_parts = [d_meta, d_bgate, dg_in, db_in, dg1, db1, dg2, db2, _rows_of(d_scale, d), _rows_of(d_sink[:, :, 0, 0], d)]
    offs = [0]
    for p in small_parts:
        offs.append(offs[-1] + p.shape[0])
    red = _all_reduce_small(_pad_rows(jnp.concatenate(small_parts, axis=0)))
    r_meta, r_bgate, r_g_in, r_b_in, r_g1, r_b1, r_g2, r_b2, r_scale, r_sink = [red[offs[k]:offs[k + 1]] for k in range(len(small_parts))]
    col0 = s_me * dc
    g_meta = lax.dynamic_slice(r_meta, (0, col0), (N_META, dc))
    g_bgate = lax.dynamic_slice(r_bgate, (0, col0), (2, dc))
    g_scale = r_scale.reshape(-1)[:pool_w]
    g_sink = r_sink.reshape(-1)[:nq]

    upd = {}

    def adamw(n, side=None):
        upd[n] = _adamw(big_w[n], S[n].arr, big_m[n], big_v[n], "adamw_" + n, side=side)

    adamw("w_out", side=_Side([share(g0b)]))
    for n in ("w_ffn_in", "w_ffn_down", "w_attn_up", "w_pool_grp", "w_pool_up", "w_in"):
        adamw(n)

    small_w = [meta_tokens, b_gate[0], ln_in_g, ln_in_b, attn_sinks, pool_scale, ln1_g, ln1_b, ln2_g, ln2_b]
    small_m = [m_meta_tokens, m_b_gate[0], m_ln_in_g, m_ln_in_b, m_attn_sinks, m_pool_scale, m_ln1_g, m_ln1_b, m_ln2_g, m_ln2_b]
    small_v = [v_meta_tokens, v_b_gate[0], v_ln_in_g, v_ln_in_b, v_attn_sinks, v_pool_scale, v_ln1_g, v_ln1_b, v_ln2_g, v_ln2_b]
    small_g = [g_meta, g_bgate, r_g_in, r_b_in, g_sink, g_scale, r_g1, r_b1, r_g2, r_b2]
    small_g = [g.reshape(w.shape) for g, w in zip(small_g, small_w)]

    def pack(parts):
        return _pad_rows(jnp.concatenate([_rows_of(p, dc) for p in parts], axis=0))

    s_delta, s_m, s_v = _adamw(pack(small_w), pack(small_g), pack(small_m), pack(small_v), "adamw_small")

    def unpack(packed):
        out, row = [], 0
        for w in small_w:
            nrow = -(-w.size // dc)
            out.append(packed[row:row + nrow].reshape(-1)[:w.size].reshape(w.shape))
            row += nrow
        return out

    s_delta, s_m, s_v = unpack(s_delta), unpack(s_m), unpack(s_v)

    order = ["meta_tokens", "ln_in_g", "ln_in_b", "w_in", "b_gate", "attn_sinks", "w_attn_up", "w_pool_grp", "pool_scale",
             "w_pool_up", "w_out", "ln1_g", "ln1_b", "w_ffn_in", "w_ffn_down", "ln2_g", "ln2_b"]
    small_names = ["meta_tokens", "b_gate", "ln_in_g", "ln_in_b", "attn_sinks", "pool_scale", "ln1_g", "ln1_b", "ln2_g", "ln2_b"]
    out_shapes = dict(meta_tokens=meta_tokens.shape, ln_in_g=ln_in_g.shape, ln_in_b=ln_in_b.shape, w_in=w_in.shape, b_gate=b_gate.shape,
                      attn_sinks=attn_sinks.shape, w_attn_up=w_attn_up.shape, w_pool_grp=w_pool_grp.shape, pool_scale=pool_scale.shape,
                      w_pool_up=w_pool_up.shape, w_out=w_out.shape, ln1_g=ln1_g.shape, ln1_b=ln1_b.shape, w_ffn_in=w_ffn_in.shape,
                      w_ffn_down=w_ffn_down.shape, ln2_g=ln2_g.shape, ln2_b=ln2_b.shape)
    grads, deltas, new_m, new_v = {}, {}, {}, {}
    for n in names:
        grads[n], (deltas[n], new_m[n], new_v[n]) = S[n].arr, upd[n]
    for k, n in enumerate(small_names):
        grads[n], deltas[n], new_m[n], new_v[n] = small_g[k], s_delta[k], s_m[k], s_v[k]

    loss = lax.psum(loss_tile[0, 0], ("x", "y", "c"))
    outs = [loss, grad_x2d.reshape(x.shape)]
    for group in (grads, deltas, new_m, new_v):
        outs += [group[n].reshape(out_shapes[n]) for n in order]
    return tuple(outs)
```

```python
import functools
import math

import jax
import jax.numpy as jnp
from jax import lax
from jax.experimental import pallas as pl
from jax.experimental.pallas import tpu as pltpu

F32 = jnp.float32
BF16 = jnp.bfloat16
MESH = pl.DeviceIdType.MESH
ANY = pl.BlockSpec(memory_space=pl.ANY)
VMEM_FULL = pl.BlockSpec(memory_space=pltpu.VMEM)

N_META = 16
HEAD_DIM = 64
N_KV = 4
BLK = 128
LEAD = (-N_META) % BLK
ROPE_DIM = HEAD_DIM // 4
ROPE_THETA = 500000.0
NEG_INF = -1e30
POOL_WINDOWS = (2, 4, 8, 16)
N_GRP = len(POOL_WINDOWS)
LN_EPS = 1e-5
DN_ALPHA = 2.0 ** 0.25
ADAM_LR = 0.001
ADAM_B1 = 0.9
ADAM_B2 = 0.999
ADAM_EPS = 1e-08
ADAM_WD = 0.01
ADAM_STEP = 10
N_CHIPS = 4
N_DEV = 8
LANES = 128
VMEM_LIMIT_MB = 56
HI = lax.Precision.HIGHEST


def _cparams(sem=None, vmem_mb=VMEM_LIMIT_MB):
    kw = dict(vmem_limit_bytes=vmem_mb << 20)
    if sem is not None:
        kw["dimension_semantics"] = sem
    return pltpu.CompilerParams(**kw)


def _pick(dim, *cands):
    for c in cands:
        if c <= dim and dim % c == 0:
            return c
    return dim


def _sds(shape, dtype):
    return jax.ShapeDtypeStruct(tuple(shape), dtype)


def _coords():
    return lax.axis_index("x"), lax.axis_index("y"), lax.axis_index("c")


def _other_chips(x, y):
    return [(1 - x, y), (x, 1 - y), (1 - x, 1 - y)]


class _Win:
    def __init__(self, kind, shard_shape, row0=0, nrows=None):
        self.kind, self.shard_shape, self.row0 = kind, tuple(shard_shape), row0
        self.nrows = shard_shape[0] if nrows is None else nrows
        self.half = self.nrows // 2

    @property
    def piece_shape(self):
        return (self.half, self.shard_shape[1])

    @property
    def full_shape(self):
        r, c = self.shard_shape
        return (r, N_CHIPS * c) if self.kind == "col" else (N_CHIPS * r, c)

    def in_full(self, ref, s, h, q=None):
        r, c = self.shard_shape
        start, size = self.row0 + h * self.half, self.half
        if q is not None:
            start, size = start + q * (self.half // 2), self.half // 2
        if self.kind == "col":
            return ref.at[pl.ds(start, size), pl.ds(s * c, c)]
        return ref.at[pl.ds(s * r + start, size), :]

    def in_shard(self, ref, h):
        return ref.at[pl.ds(self.row0 + h * self.half, self.half), :]

    def split(self, n):
        return [_Win(self.kind, self.shard_shape, self.row0 + q * (self.nrows // n), self.nrows // n) for q in range(n)]


class _Weight:
    def __init__(self, name, win, full):
        self.name, self.win, self.full = name, win, full


class _Grad:
    def __init__(self, name, win, grad, grad_row0=0):
        self.name, self.win, self.grad = name, win, grad
        self.key = "%s@%d" % (name, win.row0)
        self.grad_key = "%s@%d" % (name, grad_row0)
        self.src = _Win(win.kind, win.shard_shape, win.row0 - grad_row0, win.nrows)
        self.other = self.chip = self.landed = None


class _Shard:
    def __init__(self, name):
        self.name, self.arr = name, None


COPY_STREAMS = 8
BF16_ROWS = 16


def _stream_views(ref):
    rows, cols = ref.shape
    n = COPY_STREAMS
    if rows % (n * BF16_ROWS) == 0:
        return [ref.at[pl.ds(i * (rows // n), rows // n), :] for i in range(n)]
    if cols % (n * LANES) == 0:
        return [ref.at[:, pl.ds(i * (cols // n), cols // n)] for i in range(n)]
    return [ref]


class _StreamedCopy:
    def __init__(self, make, src, dst):
        self.whole = make(src, dst)
        self.parts = [make(s, d) for s, d in zip(_stream_views(src), _stream_views(dst))]

    def start(self):
        for cp in self.parts:
            cp.start()

    def wait(self):
        self.whole.wait()

    def wait_send(self):
        self.whole.wait_send()

    def wait_recv(self):
        self.whole.wait_recv()


class _Ctx:
    def __init__(self, side, in_refs, out_refs, send_sems, recv_sems, local_sems, base, lbase):
        self.side, self.in_refs, self.out_refs = side, in_refs, out_refs
        self.send_sems, self.recv_sems, self.local_sems, self.base, self.lbase = send_sems, recv_sems, local_sems, base, lbase

    def ref(self, key):
        info = self.side.info[key]
        return self.in_refs[info["in"]] if info["in"] is not None else self.out_refs[info["out"]]

    def remote(self, k, src, dst, to):
        def make(s, d):
            return pltpu.make_async_remote_copy(src_ref=s, dst_ref=d, send_sem=self.send_sems.at[self.base + k],
                                                recv_sem=self.recv_sems.at[self.base + k], device_id=to, device_id_type=MESH)

        return _StreamedCopy(make, src, dst)


class _Side:
    def __init__(self, ops):
        self.ops, self.info, self.keys = ops, {}, []
        self.nsem = self.nlocal = 0
        self.bases = []
        for op in ops:
            op.register(self)
            self.bases.append((self.nsem, self.nlocal))
            self.nsem += op.nsem
            self.nlocal += op.nlocal
        self.inputs, self.out_shape, self.aliases = [], [], {}
        for key in self.keys:
            info = self.info[key]
            info["in"] = info["out"] = None
            if info["arr"] is not None:
                info["in"] = len(self.inputs)
                self.inputs.append(info["arr"])
            if info["write"]:
                info["out"] = len(self.out_shape)
                self.out_shape.append(info["sds"])
                if info["in"] is not None:
                    self.aliases[info["in"]] = info["out"]

    def need(self, key, arr=None, sds=None, write=False):
        if key not in self.info:
            self.keys.append(key)
            self.info[key] = dict(arr=arr, sds=sds if arr is None else _sds(arr.shape, arr.dtype), write=write)
        else:
            self.info[key]["write"] = self.info[key]["write"] or write

    def _ctx(self, k, in_refs, out_refs, sems):
        return _Ctx(self, in_refs, out_refs, sems[0], sems[1], sems[2], *self.bases[k])

    def start(self, in_refs, out_refs, sems):
        for k, op in enumerate(self.ops):
            op.start(self._ctx(k, in_refs, out_refs, sems))

    def finish(self, in_refs, out_refs, sems):
        for k, op in enumerate(self.ops):
            op.finish(self._ctx(k, in_refs, out_refs, sems))

    def scratch(self):
        return [pltpu.SemaphoreType.DMA((max(self.nsem, 1),)), pltpu.SemaphoreType.DMA((max(self.nsem, 1),)),
                pltpu.SemaphoreType.DMA((max(self.nlocal, 1),))]

    def commit(self, outs):
        res = {key: outs[self.info[key]["out"]] for key in self.keys if self.info[key]["write"]}
        for op in self.ops:
            op.commit(res)


NEIGHBOURS, DIAGONAL = (0, 1), (2,)


class _GatherIci:
    def __init__(self, pairs):
        self.pairs = pairs
        self.nsem, self.nlocal = 2 * len(pairs), 0

    def register(self, side):
        for w, win in self.pairs:
            side.need(("full", w.name), arr=w.full, write=True)

    def _copies(self, ctx):
        x, y, c = _coords()
        s_me = 2 * x + y
        sends, recvs = [], []
        for t, (w, win) in enumerate(self.pairs):
            full = ctx.ref(("full", w.name))
            for j in NEIGHBOURS:
                ox, oy = _other_chips(x, y)[j]
                mine, landing = win.in_full(full, s_me, c), win.in_full(full, 2 * ox + oy, c)
                sends.append(ctx.remote(2 * t + j, mine, mine, (ox, oy, c)))
                recvs.append(ctx.remote(2 * t + j, landing, landing, (x, y, c)))
        return sends, recvs

    def start(self, ctx):
        for cp in self._copies(ctx)[0]:
            cp.start()

    def finish(self, ctx):
        sends, recvs = self._copies(ctx)
        for cp in recvs:
            cp.wait_recv()
        for cp in sends:
            cp.wait_send()

    def commit(self, res):
        for w, _ in self.pairs:
            w.full = res[("full", w.name)]


class _GatherRing:
    def __init__(self, pairs):
        self.pairs = pairs
        self.nsem, self.nlocal = 2 * len(pairs), 0

    def register(self, side):
        for w, win in self.pairs:
            side.need(("full", w.name), arr=w.full, write=True)

    def _copies(self, ctx):
        x, y, c = _coords()
        s_x, s_y, s_d = 2 * (1 - x) + y, 2 * x + (1 - y), 2 * (1 - x) + (1 - y)
        sends, recvs = [], []
        for t, (w, win) in enumerate(self.pairs):
            full = ctx.ref(("full", w.name))
            for q, (s_from, to) in enumerate([(s_x, (x, 1 - y, c)), (s_y, (1 - x, y, c))]):
                passed, landing = win.in_full(full, s_from, c, q), win.in_full(full, s_d, c, q)
                sends.append(ctx.remote(2 * t + q, passed, passed, to))
                recvs.append(ctx.remote(2 * t + q, landing, landing, (x, y, c)))
        return sends, recvs

    def start(self, ctx):
        for cp in self._copies(ctx)[0]:
            cp.start()

    def finish(self, ctx):
        sends, recvs = self._copies(ctx)
        for cp in recvs:
            cp.wait_recv()
        for cp in sends:
            cp.wait_send()

    def commit(self, res):
        for w, _ in self.pairs:
            w.full = res[("full", w.name)]


class _GatherD2d:
    def __init__(self, pairs, which):
        self.pairs, self.which = pairs, which
        self.nsem, self.nlocal = 3 * len(pairs), 0

    def register(self, side):
        for w, win in self.pairs:
            side.need(("full", w.name), arr=w.full, write=True)

    def _copies(self, ctx):
        x, y, c = _coords()
        sends, recvs = [], []
        for t, (w, win) in enumerate(self.pairs):
            full = ctx.ref(("full", w.name))
            for j in self.which:
                ox, oy = _other_chips(x, y)[j]
                mine, theirs = win.in_full(full, 2 * ox + oy, c), win.in_full(full, 2 * ox + oy, 1 - c)
                sends.append(ctx.remote(3 * t + j, mine, mine, (x, y, 1 - c)))
                recvs.append(ctx.remote(3 * t + j, theirs, theirs, (x, y, c)))
        return sends, recvs

    def start(self, ctx):
        for cp in self._copies(ctx)[0]:
            cp.start()

    def finish(self, ctx):
        sends, recvs = self._copies(ctx)
        for cp in recvs:
            cp.wait_recv()
        for cp in sends:
            cp.wait_send()

    def commit(self, res):
        for w, _ in self.pairs:
            w.full = res[("full", w.name)]


class _ReduceSibling:
    def __init__(self, grads):
        self.grads = grads
        self.nsem, self.nlocal = N_CHIPS * len(grads), 0

    def register(self, side):
        for g in self.grads:
            side.need(("grad", g.grad_key), arr=g.grad)
            side.need(("other", g.key), sds=_sds((N_CHIPS,) + g.win.piece_shape, F32), write=True)

    def _copies(self, ctx):
        x, y, c = _coords()
        out = []
        for t, g in enumerate(self.grads):
            grad, other = ctx.ref(("grad", g.grad_key)), ctx.ref(("other", g.key))
            for s in range(N_CHIPS):
                out.append(ctx.remote(N_CHIPS * t + s, g.src.in_full(grad, s, 1 - c), other.at[s], (x, y, 1 - c)))
        return out

    def start(self, ctx):
        for cp in self._copies(ctx):
            cp.start()

    def finish(self, ctx):
        for cp in self._copies(ctx):
            cp.wait()

    def commit(self, res):
        for g in self.grads:
            g.other = res[("other", g.key)]


class _ReduceChips:
    def __init__(self, grads):
        self.grads = grads
        self.nsem, self.nlocal = 3 * len(grads), 0

    def register(self, side):
        for g in self.grads:
            side.need(("chip", g.key), arr=g.chip)
            side.need(("landed", g.key), sds=_sds(g.chip.shape, g.chip.dtype), write=True)

    def _copies(self, ctx):
        x, y, c = _coords()
        s_me = 2 * x + y
        out = []
        for t, g in enumerate(self.grads):
            chip, landed = ctx.ref(("chip", g.key)), ctx.ref(("landed", g.key))
            for j, (ox, oy) in enumerate(_other_chips(x, y)):
                out.append(ctx.remote(3 * t + j, chip.at[2 * ox + oy], landed.at[s_me], (ox, oy, c)))
        return out

    def start(self, ctx):
        for cp in self._copies(ctx):
            cp.start()

    def finish(self, ctx):
        for cp in self._copies(ctx):
            cp.wait()

    def commit(self, res):
        for g in self.grads:
            g.landed = res[("landed", g.key)]


class _ShareReduced:
    def __init__(self, items):
        self.items = items
        self.nsem, self.nlocal = len(items), 0

    def register(self, side):
        for sh, _ in self.items:
            side.need(("reduced", sh.name), arr=sh.arr, write=True)

    def _copies(self, ctx):
        x, y, c = _coords()
        sends, recvs = [], []
        for t, (sh, win) in enumerate(self.items):
            ref = ctx.ref(("reduced", sh.name))
            sends.append(ctx.remote(t, win.in_shard(ref, c), win.in_shard(ref, c), (x, y, 1 - c)))
            recvs.append(ctx.remote(t, win.in_shard(ref, 1 - c), win.in_shard(ref, 1 - c), (x, y, c)))
        return sends, recvs

    def start(self, ctx):
        for cp in self._copies(ctx)[0]:
            cp.start()

    def finish(self, ctx):
        sends, recvs = self._copies(ctx)
        for cp in recvs:
            cp.wait_recv()
        for cp in sends:
            cp.wait_send()

    def commit(self, res):
        for sh, _ in self.items:
            sh.arr = res[("reduced", sh.name)]


def _gcall(body, name, grid, in_specs, out_specs, out_shape, args, scratch=(), sem=None, side=None, prefetch=()):
    single = not isinstance(out_shape, (list, tuple))
    out_shapes = [out_shape] if single else list(out_shape)
    out_specs = [out_specs] if single else list(out_specs)
    hosted = side is not None and bool(side.ops)
    n_pf, n_in, n_out, n_scr = len(prefetch), len(args), len(out_shapes), len(scratch)
    ns_in, ns_out = (len(side.inputs), len(side.out_shape)) if hosted else (0, 0)

    def wrapped(*refs):
        pf, refs = refs[:n_pf], refs[n_pf:]
        a, si = refs[:n_in], refs[n_in:n_in + ns_in]
        o = refs[n_in + ns_in:n_in + ns_in + n_out]
        so = refs[n_in + ns_in + n_out:n_in + ns_in + n_out + ns_out]
        rest = refs[n_in + ns_in + n_out + ns_out:]
        scr, sems = rest[:n_scr], rest[n_scr:]
        ids = [pl.program_id(k) for k in range(len(grid))]
        first = functools.reduce(jnp.logical_and, [i == 0 for i in ids])
        last = functools.reduce(jnp.logical_and, [i == g - 1 for i, g in zip(ids, grid)])

        @pl.when(first)
        def _():
            side.start(si, so, sems)

        body(*pf, *a, *o, *scr)

        @pl.when(last)
        def _():
            side.finish(si, so, sems)

    res = pl.pallas_call(
        wrapped if hosted else body, name=name,
        grid_spec=pltpu.PrefetchScalarGridSpec(
            num_scalar_prefetch=n_pf, grid=grid,
            in_specs=list(in_specs) + [ANY] * ns_in,
            out_specs=out_specs + [ANY] * ns_out,
            scratch_shapes=list(scratch) + (side.scratch() if hosted else []),
        ),
        out_shape=out_shapes + (side.out_shape if hosted else []),
        input_output_aliases={n_pf + n_in + i: n_out + j for i, j in side.aliases.items()} if hosted else {},
        compiler_params=_cparams(("arbitrary",) * len(grid) if hosted else sem),
    )(*prefetch, *args, *(side.inputs if hosted else []))
    if hosted:
        side.commit(res[n_out:])
    return res[0] if single else res[:n_out]


_DOT_DIMS = {
    "nn": (((1,), (0,)), ((), ())),
    "nt": (((1,), (1,)), ((), ())),
    "tn": (((0,), (0,)), ((), ())),
}


def _mm(a, b, mode, out_dtype, tm, tn, tk, name, j_outer=False, side=None):
    if mode == "nn":
        (m, k), n = a.shape, b.shape[1]
    elif mode == "nt":
        (m, k), n = a.shape, b.shape[0]
    else:
        (k, m), n = a.shape, b.shape[1]
    tm, tn, tk = _pick(m, tm), _pick(n, tn), _pick(k, tk)
    gi, gj, gk = m // tm, n // tn, k // tk
    dims = _DOT_DIMS[mode]

    def ij(g0, g1):
        return (g1, g0) if j_outer else (g0, g1)

    if mode == "tn":
        a_spec = pl.BlockSpec((tk, tm), lambda g0, g1, kk: (kk, ij(g0, g1)[0]))
    else:
        a_spec = pl.BlockSpec((tm, tk), lambda g0, g1, kk: (ij(g0, g1)[0], kk))
    if mode == "nt":
        b_spec = pl.BlockSpec((tn, tk), lambda g0, g1, kk: (ij(g0, g1)[1], kk))
    else:
        b_spec = pl.BlockSpec((tk, tn), lambda g0, g1, kk: (kk, ij(g0, g1)[1]))
    o_spec = pl.BlockSpec((tm, tn), lambda g0, g1, kk: ij(g0, g1))

    def body(a_ref, b_ref, o_ref, *scr):
        p = lax.dot_general(a_ref[...], b_ref[...], dims, preferred_element_type=F32)
        if gk == 1:
            o_ref[...] = p.astype(out_dtype)
        else:
            acc = scr[0]
            kk = pl.program_id(2)

            @pl.when(kk == 0)
            def _():
                acc[...] = p

            @pl.when(kk > 0)
            def _():
                acc[...] += p

            @pl.when(kk == gk - 1)
            def _():
                o_ref[...] = acc[...].astype(out_dtype)

    return _gcall(body, name, (gj, gi, gk) if j_outer else (gi, gj, gk), [a_spec, b_spec], o_spec, _sds((m, n), out_dtype), (a, b),
                  scratch=[pltpu.VMEM((tm, tn), F32)] if gk > 1 else [], sem=("parallel", "parallel", "arbitrary"), side=side)


def _stream_block(i, x_ref, meta_ref):
    d = x_ref.shape[-1]
    first = jnp.concatenate([jnp.zeros((LEAD, d), F32), meta_ref[...]], axis=0)
    return jnp.where(i == 0, first, x_ref[...])


def _norm(xb):
    mu = jnp.mean(xb, axis=-1, keepdims=True)
    xc = xb - mu
    var = jnp.mean(xc * xc, axis=-1, keepdims=True)
    rstd = lax.rsqrt(var + LN_EPS)
    return xc * rstd, rstd


def _ln_bwd_rows(dy, xhat, rstd, g):
    dyg = dy * g
    m1 = jnp.mean(dyg, axis=-1, keepdims=True)
    m2 = jnp.mean(dyg * xhat, axis=-1, keepdims=True)
    return rstd * (dyg - m1 - xhat * m2)


def _ln_in_fwd(x2d, meta, g, b, nb, side=None):
    seq, d = x2d.shape

    def body(x_ref, meta_ref, g_ref, b_ref, h_ref, hb_ref):
        xb = _stream_block(pl.program_id(0), x_ref, meta_ref)
        xhat, _ = _norm(xb)
        y = xhat * g_ref[...] + b_ref[...]
        h_ref[...] = y
        hb_ref[...] = y.astype(BF16)

    row = pl.BlockSpec((BLK, d), lambda i: (i, 0))
    vec = pl.BlockSpec((1, d), lambda i: (0, 0))
    return _gcall(body, "ln_in_fwd", (nb,),
                  [pl.BlockSpec((BLK, d), lambda i: (jnp.maximum(i - 1, 0), 0)), pl.BlockSpec((N_META, d), lambda i: (0, 0)), vec, vec],
                  [row, row], [_sds((nb * BLK, d), F32), _sds((nb * BLK, d), BF16)], (x2d, meta, g, b), sem=("parallel",), side=side)


def _res_ln_fwd(h, z, g, b, side=None):
    tp, d = h.shape

    def body(h_ref, z_ref, g_ref, b_ref, r_ref, y_ref, yb_ref):
        r = DN_ALPHA * h_ref[...] + z_ref[...]
        xhat, _ = _norm(r)
        y = xhat * g_ref[...] + b_ref[...]
        r_ref[...] = r
        y_ref[...] = y
        yb_ref[...] = y.astype(BF16)

    row = pl.BlockSpec((BLK, d), lambda i: (i, 0))
    vec = pl.BlockSpec((1, d), lambda i: (0, 0))
    return _gcall(body, "res_ln1_fwd", (tp // BLK,), [row, row, vec, vec], [row, row, row],
                  [_sds((tp, d), F32), _sds((tp, d), F32), _sds((tp, d), BF16)], (h, z, g, b), sem=("parallel",), side=side)


def _final_ln_loss(h1, z2, g, b, tgt):
    tp, d = h1.shape

    def body(h_ref, z_ref, g_ref, b_ref, t_ref, dr_ref, drb_ref, loss_ref, dg_ref, db_ref):
        i = pl.program_id(0)
        r = DN_ALPHA * h_ref[...] + z_ref[...]
        xhat, rstd = _norm(r)
        y = xhat * g_ref[...] + b_ref[...]
        err = jnp.where(i >= 1, y - t_ref[...], 0.0)
        dy = err * (1.0 / d)
        dr = _ln_bwd_rows(dy, xhat, rstd, g_ref[...])
        dr_ref[...] = dr
        drb_ref[...] = dr.astype(BF16)

        @pl.when(i == 0)
        def _():
            loss_ref[...] = jnp.zeros_like(loss_ref)
            dg_ref[...] = jnp.zeros_like(dg_ref)
            db_ref[...] = jnp.zeros_like(db_ref)

        loss_ref[...] += 0.5 * jnp.sum(jnp.sum(err * err, axis=-1, keepdims=True) * (1.0 / d), axis=0, keepdims=True)
        dg_ref[...] += jnp.sum(dy * xhat, axis=0, keepdims=True)
        db_ref[...] += jnp.sum(dy, axis=0, keepdims=True)

    row = pl.BlockSpec((BLK, d), lambda i: (i, 0))
    vec = pl.BlockSpec((1, d), lambda i: (0, 0))
    return _gcall(body, "final_ln_loss", (tp // BLK,),
                  [row, row, vec, vec, pl.BlockSpec((BLK, d), lambda i: (jnp.maximum(i - 1, 0), 0))],
                  [row, row, pl.BlockSpec((8, LANES), lambda i: (0, 0)), vec, vec],
                  [_sds((tp, d), F32), _sds((tp, d), BF16), _sds((8, LANES), F32), _sds((1, d), F32), _sds((1, d), F32)],
                  (h1, z2, g, b, tgt), sem=("arbitrary",))


def _ln1_bwd(d_res, d_mm, r, g, side=None):
    tp, d = r.shape

    def body(a_ref, m_ref, r_ref, g_ref, dr_ref, drb_ref, dg_ref, db_ref):
        dy = DN_ALPHA * a_ref[...] + m_ref[...]
        xhat, rstd = _norm(r_ref[...])
        dr = _ln_bwd_rows(dy, xhat, rstd, g_ref[...])
        dr_ref[...] = dr
        drb_ref[...] = dr.astype(BF16)

        @pl.when(pl.program_id(0) == 0)
        def _():
            dg_ref[...] = jnp.zeros_like(dg_ref)
            db_ref[...] = jnp.zeros_like(db_ref)

        dg_ref[...] += jnp.sum(dy * xhat, axis=0, keepdims=True)
        db_ref[...] += jnp.sum(dy, axis=0, keepdims=True)

    row = pl.BlockSpec((BLK, d), lambda i: (i, 0))
    vec = pl.BlockSpec((1, d), lambda i: (0, 0))
    return _gcall(body, "ln1_bwd", (tp // BLK,), [row, row, row, vec], [row, row, vec, vec],
                  [_sds((tp, d), F32), _sds((tp, d), BF16), _sds((1, d), F32), _sds((1, d), F32)], (d_res, d_mm, r, g),
                  sem=("arbitrary",), side=side)


def _ln_in_bwd(d_res, d_mm, x2d, meta, g, side=None):
    seq, d = x2d.shape
    nb = d_res.shape[0] // BLK

    def body(a_ref, m_ref, x_ref, meta_ref, g_ref, gx_ref, gm_ref, dg_ref, db_ref):
        i = pl.program_id(0)
        dy = DN_ALPHA * a_ref[...] + m_ref[...]
        xhat, rstd = _norm(_stream_block(i, x_ref, meta_ref))
        dx = _ln_bwd_rows(dy, xhat, rstd, g_ref[...])
        gx_ref[...] = dx

        @pl.when(i == 0)
        def _():
            gm_ref[...] = dx[LEAD:, :]
            dg_ref[...] = jnp.zeros_like(dg_ref)
            db_ref[...] = jnp.zeros_like(db_ref)

        dg_ref[...] += jnp.sum(dy * xhat, axis=0, keepdims=True)
        db_ref[...] += jnp.sum(dy, axis=0, keepdims=True)

    row = pl.BlockSpec((BLK, d), lambda i: (i, 0))
    xrow = pl.BlockSpec((BLK, d), lambda i: (jnp.maximum(i - 1, 0), 0))
    vec = pl.BlockSpec((1, d), lambda i: (0, 0))
    met = pl.BlockSpec((N_META, d), lambda i: (0, 0))
    return _gcall(body, "ln_in_bwd", (nb,), [row, row, xrow, met, vec], [xrow, met, vec, vec],
                  [_sds((seq, d), F32), _sds((N_META, d), F32), _sds((1, d), F32), _sds((1, d), F32)], (d_res, d_mm, x2d, meta, g),
                  sem=("arbitrary",), side=side)


def _rope_tables(tp):
    half = ROPE_DIM // 2
    inv_freq = ROPE_THETA ** (-jnp.arange(half, dtype=F32) * 2.0 / ROPE_DIM)
    pos = (jnp.arange(tp) - LEAD).astype(F32)
    ang = pos[:, None] * inv_freq[None, :]
    cos, sin = jnp.cos(ang), jnp.sin(ang)
    ones = jnp.ones((tp, HEAD_DIM - ROPE_DIM), F32)
    cos_h = jnp.concatenate([cos, cos, ones], axis=1)
    sin_h = jnp.concatenate([-sin, sin, 0.0 * ones], axis=1)
    reps = LANES // HEAD_DIM
    return jnp.tile(cos_h, (1, reps)), jnp.tile(sin_h, (1, reps))


def _rope_partner(x):
    half = ROPE_DIM // 2
    lane = lax.broadcasted_iota(jnp.int32, x.shape, 1) % HEAD_DIM
    upper = jnp.where(lane < ROPE_DIM, pltpu.roll(x, half, 1), 0.0)
    return jnp.where(lane < half, pltpu.roll(x, LANES - half, 1), upper)


def _rope_fwd(proj, cos, sin, n_rot, width, side=None):
    tp = proj.shape[0]

    def body(p_ref, c_ref, s_ref, o_ref):
        c, s = c_ref[...], s_ref[...]
        for j in range(width // LANES):
            sl = slice(j * LANES, (j + 1) * LANES)
            xj = p_ref[:, sl]
            if j < n_rot:
                xj = xj * c + _rope_partner(xj) * s
            o_ref[:, sl] = xj.astype(BF16)

    tab = pl.BlockSpec((BLK, LANES), lambda i: (i, 0))
    blk = pl.BlockSpec((BLK, width), lambda i: (i, 0))
    return _gcall(body, "rope_fwd", (tp // BLK,), [blk, tab, tab], blk, _sds((tp, width), BF16), (proj, cos, sin), sem=("parallel",), side=side)


def _rope_bwd(dq, dk_cur, dk_prev, dk_meta, dv_cur, dv_prev, dv_meta, cos, sin, side=None):
    tp, aw = dq.shape
    kw = dk_cur.shape[1]
    nb = tp // BLK

    def body(dq_ref, kc_ref, kp_ref, km_ref, vc_ref, vp_ref, vm_ref, c_ref, s_ref, o_ref):
        i = pl.program_id(0)
        c, s = c_ref[...], s_ref[...]
        has_next = i + 1 < nb

        def unrot(g):
            return g * c + _rope_partner(g * s)

        def kv_sum(cur, prv, met):
            return cur[...] + jnp.where(has_next, prv[...], 0.0) + jnp.where(i == 0, met[...], 0.0)

        for j in range(aw // LANES):
            sl = slice(j * LANES, (j + 1) * LANES)
            o_ref[:, sl] = unrot(dq_ref[:, sl]).astype(BF16)
        dk = kv_sum(kc_ref, kp_ref, km_ref)
        dv = kv_sum(vc_ref, vp_ref, vm_ref)
        for j in range(kw // LANES):
            sl = slice(j * LANES, (j + 1) * LANES)
            o_ref[:, aw + j * LANES:aw + (j + 1) * LANES] = unrot(dk[:, sl]).astype(BF16)
            o_ref[:, aw + kw + j * LANES:aw + kw + (j + 1) * LANES] = dv[:, sl].astype(BF16)

    cur = pl.BlockSpec((BLK, kw), lambda i: (i, 0))
    nxt = pl.BlockSpec((BLK, kw), lambda i: (jnp.minimum(i + 1, nb - 1), 0))
    met = pl.BlockSpec((BLK, kw), lambda i: (0, 0))
    tab = pl.BlockSpec((BLK, LANES), lambda i: (i, 0))
    return _gcall(body, "rope_bwd", (nb,), [pl.BlockSpec((BLK, aw), lambda i: (i, 0)), cur, nxt, met, cur, nxt, met, tab, tab],
                  pl.BlockSpec((BLK, aw + 2 * kw), lambda i: (i, 0)), _sds((tp, aw + 2 * kw), BF16),
                  (dq, dk_cur, dk_prev, dk_meta, dv_cur, dv_prev, dv_meta, cos, sin), sem=("parallel",), side=side)


def _attn_probs(n, q_ref, km_ref, kp_ref, kc_ref, sink_ref, grp):
    scale = HEAD_DIM ** -0.5
    qs = q_ref[...].reshape(grp * BLK, HEAD_DIM)
    kcat = jnp.concatenate([km_ref[...], kp_ref[...], kc_ref[...]], axis=0)
    s = lax.dot_general(qs, kcat, _DOT_DIMS["nt"], preferred_element_type=F32) * scale
    s = s.reshape(grp, BLK, 3 * BLK)
    r = lax.broadcasted_iota(jnp.int32, (1, BLK, 3 * BLK), 1)
    j = lax.broadcasted_iota(jnp.int32, (1, BLK, 3 * BLK), 2)
    q_idx = n * BLK + r
    meta_ok = (j >= LEAD) & (j < BLK) & (q_idx >= j)
    k_idx = (n - 1) * BLK + (j - BLK)
    diff = q_idx - k_idx
    band_ok = (j >= BLK) & (diff >= 0) & (diff < BLK) & (k_idx >= LEAD + N_META)
    s = jnp.where(meta_ok | band_ok, s, NEG_INF)
    sink = sink_ref[...]
    m = jnp.maximum(jnp.max(s, axis=-1, keepdims=True), sink)
    p = jnp.exp(s - m)
    e_sink = jnp.exp(sink - m)
    inv = 1.0 / (jnp.sum(p, axis=-1, keepdims=True) + e_sink)
    return qs, kcat, p * inv, e_sink * inv


def _attn_specs(grp):
    qspec = pl.BlockSpec((grp, BLK, HEAD_DIM), lambda kk, n: (kk, n, 0))
    kmeta = pl.BlockSpec((None, BLK, HEAD_DIM), lambda kk, n: (kk, 0, 0))
    kprev = pl.BlockSpec((None, BLK, HEAD_DIM), lambda kk, n: (kk, jnp.maximum(n - 1, 0), 0))
    kcur = pl.BlockSpec((None, BLK, HEAD_DIM), lambda kk, n: (kk, n, 0))
    sink = pl.BlockSpec((None, grp, BLK, 1), lambda kk, n: (kk, 0, 0, 0))
    return qspec, kmeta, kprev, kcur, sink


def _attn_fwd(q_hm, k_hm, v_hm, sink4, side=None):
    nq, tp, _ = q_hm.shape
    nkv = k_hm.shape[0]
    grp = nq // nkv

    def body(q_ref, km_ref, kp_ref, kc_ref, vm_ref, vp_ref, vc_ref, sink_ref, o_ref):
        n = pl.program_id(1)
        _, _, pn, _ = _attn_probs(n, q_ref, km_ref, kp_ref, kc_ref, sink_ref, grp)
        vcat = jnp.concatenate([vm_ref[...], vp_ref[...], vc_ref[...]], axis=0)
        o = jnp.dot(pn.reshape(grp * BLK, 3 * BLK).astype(BF16), vcat, preferred_element_type=F32)
        o_ref[...] = o.reshape(grp, BLK, HEAD_DIM).astype(BF16)

    qspec, kmeta, kprev, kcur, sink = _attn_specs(grp)
    return _gcall(body, "attn_fwd", (nkv, tp // BLK), [qspec, kmeta, kprev, kcur, kmeta, kprev, kcur, sink], qspec,
                  _sds((nq, tp, HEAD_DIM), BF16), (q_hm, k_hm, k_hm, k_hm, v_hm, v_hm, v_hm, sink4), sem=("parallel", "parallel"), side=side)


def _attn_bwd(q_hm, k_hm, v_hm, sink4, do_hm, side=None):
    nq, tp, _ = q_hm.shape
    nkv = k_hm.shape[0]
    grp = nq // nkv
    scale = HEAD_DIM ** -0.5

    def body(q_ref, km_ref, kp_ref, kc_ref, vm_ref, vp_ref, vc_ref, sink_ref, do_ref,
             dq_ref, dkc_ref, dkp_ref, dkm_ref, dvc_ref, dvp_ref, dvm_ref, dsk_ref):
        n = pl.program_id(1)
        qs, kcat, pn, p_sink = _attn_probs(n, q_ref, km_ref, kp_ref, kc_ref, sink_ref, grp)
        vcat = jnp.concatenate([vm_ref[...], vp_ref[...], vc_ref[...]], axis=0)
        pn2 = pn.reshape(grp * BLK, 3 * BLK)
        pnb = pn2.astype(BF16)
        dob = do_ref[...].reshape(grp * BLK, HEAD_DIM).astype(BF16)
        dp = lax.dot_general(dob, vcat, _DOT_DIMS["nt"], preferred_element_type=F32)
        delta = jnp.sum(pn2 * dp, axis=-1, keepdims=True)
        ds = (pn2 * (dp - delta) * scale).astype(BF16)
        dq_ref[...] = jnp.dot(ds, kcat, preferred_element_type=F32).reshape(grp, BLK, HEAD_DIM)
        dk = lax.dot_general(ds, qs, _DOT_DIMS["tn"], preferred_element_type=F32)
        dv = lax.dot_general(pnb, dob, _DOT_DIMS["tn"], preferred_element_type=F32)
        dkp_ref[...] = dk[BLK:2 * BLK]
        dkc_ref[...] = dk[2 * BLK:]
        dvp_ref[...] = dv[BLK:2 * BLK]
        dvc_ref[...] = dv[2 * BLK:]
        dsk = -jnp.sum(p_sink * delta.reshape(grp, BLK, 1), axis=1, keepdims=True)

        @pl.when(n == 0)
        def _():
            dkm_ref[...] = jnp.zeros_like(dkm_ref)
            dvm_ref[...] = jnp.zeros_like(dvm_ref)
            dsk_ref[...] = jnp.zeros_like(dsk_ref)

        dkm_ref[...] += dk[:BLK]
        dvm_ref[...] += dv[:BLK]
        dsk_ref[...] += jnp.broadcast_to(dsk, (grp, BLK, 1))

    qspec, kmeta, kprev, kcur, sink = _attn_specs(grp)
    kv_shape = _sds((nkv, tp, HEAD_DIM), F32)
    meta_shape = _sds((nkv, BLK, HEAD_DIM), F32)
    return _gcall(body, "attn_bwd", (nkv, tp // BLK), [qspec, kmeta, kprev, kcur, kmeta, kprev, kcur, sink, qspec],
                  [qspec, kcur, kcur, kmeta, kcur, kcur, kmeta, sink],
                  [_sds((nq, tp, HEAD_DIM), F32), kv_shape, kv_shape, meta_shape, kv_shape, kv_shape, meta_shape,
                   _sds((nkv, grp, BLK, 1), F32)],
                  (q_hm, k_hm, k_hm, k_hm, v_hm, v_hm, v_hm, sink4, do_hm), sem=("parallel", "arbitrary"), side=side)


def _pool_coef(row_blk, col_blk, w):
    r = lax.broadcasted_iota(jnp.int32, (BLK, BLK), 0)
    j = lax.broadcasted_iota(jnp.int32, (BLK, BLK), 1)
    t = row_blk * BLK + r - LEAD
    tj = col_blk * BLK + j - LEAD
    dist = t - tj
    inwin = (dist >= 0) & (dist < w) & (tj >= 0)
    count = jnp.maximum(jnp.minimum(t + 1, w), 1).astype(F32)
    return jnp.where(inwin, 1.0 / count, 0.0) - jnp.where((dist == 0) & (tj >= 0), 1.0, 0.0)


def _pool_fwd(proj, wg, scale, u_off, pool_w, side=None):
    tp = proj.shape[0]
    gw = pool_w // N_GRP
    nb = tp // BLK
    cb = u_off // gw

    def body(up_ref, uc_ref, wg_ref, sc_ref, pooled_ref, mx_ref, pm_ref):
        n, g = pl.program_id(0), pl.program_id(1)
        w = jnp.left_shift(2, g)
        pooled = (jnp.dot(_pool_coef(n, n - 1, w), up_ref[...], precision=HI, preferred_element_type=F32)
                  + jnp.dot(_pool_coef(n, n, w), uc_ref[...], precision=HI, preferred_element_type=F32))
        pb = pooled.astype(BF16)
        mx = jnp.dot(pb, wg_ref[...], preferred_element_type=F32)
        pooled_ref[...] = pb
        mx_ref[...] = mx
        pm_ref[...] = (mx * sc_ref[...]).astype(BF16)

    blk = pl.BlockSpec((BLK, gw), lambda n, g: (n, g))
    return _gcall(body, "pool_fwd", (nb, N_GRP),
                  [pl.BlockSpec((BLK, gw), lambda n, g: (jnp.maximum(n - 1, 0), cb + g)),
                   pl.BlockSpec((BLK, gw), lambda n, g: (n, cb + g)),
                   pl.BlockSpec((None, gw, gw), lambda n, g: (g, 0, 0)),
                   pl.BlockSpec((1, gw), lambda n, g: (0, g))],
                  [blk, blk, blk], [_sds((tp, pool_w), BF16), _sds((tp, pool_w), F32), _sds((tp, pool_w), BF16)],
                  (proj, proj, wg, scale), sem=("parallel", "parallel"), side=side)


def _pool_bwd_mix(d_pm, mx, pooled, wg, scale, side=None):
    tp, pool_w = d_pm.shape
    gw = pool_w // N_GRP

    def body(d_ref, mx_ref, pl_ref, wg_ref, sc_ref, dp_ref, dwg_ref, dsc_ref):
        n = pl.program_id(1)
        d = d_ref[...]
        dmx = (d * sc_ref[...]).astype(BF16)
        dp_ref[...] = lax.dot_general(dmx, wg_ref[...], _DOT_DIMS["nt"], preferred_element_type=F32)

        @pl.when(n == 0)
        def _():
            dwg_ref[...] = jnp.zeros_like(dwg_ref)
            dsc_ref[...] = jnp.zeros_like(dsc_ref)

        dwg_ref[...] += lax.dot_general(pl_ref[...], dmx, _DOT_DIMS["tn"], preferred_element_type=F32)
        dsc_ref[...] += jnp.sum(d * mx_ref[...], axis=0, keepdims=True)

    blk = pl.BlockSpec((BLK, gw), lambda g, n: (n, g))
    wspec = pl.BlockSpec((None, gw, gw), lambda g, n: (g, 0, 0))
    sspec = pl.BlockSpec((1, gw), lambda g, n: (0, g))
    return _gcall(body, "pool_bwd_mix", (N_GRP, tp // BLK), [blk, blk, blk, wspec, sspec], [blk, wspec, sspec],
                  [_sds((tp, pool_w), F32), _sds((N_GRP, gw, gw), F32), _sds((1, pool_w), F32)], (d_pm, mx, pooled, wg, scale),
                  sem=("parallel", "arbitrary"), side=side)


def _pool_bwd_band(dp, side=None):
    tp, pool_w = dp.shape
    gw = pool_w // N_GRP
    nb = tp // BLK

    def body(dc_ref, dn_ref, du_ref):
        n, g = pl.program_id(0), pl.program_id(1)
        w = jnp.left_shift(2, g)
        dnext = jnp.where(n + 1 < nb, dn_ref[...], 0.0)
        du = (lax.dot_general(_pool_coef(n, n, w), dc_ref[...], _DOT_DIMS["tn"], precision=HI, preferred_element_type=F32)
              + lax.dot_general(_pool_coef(n + 1, n, w), dnext, _DOT_DIMS["tn"], precision=HI, preferred_element_type=F32))
        du_ref[...] = du.astype(BF16)

    blk = pl.BlockSpec((BLK, gw), lambda n, g: (n, g))
    return _gcall(body, "pool_bwd_band", (nb, N_GRP), [blk, pl.BlockSpec((BLK, gw), lambda n, g: (jnp.minimum(n + 1, nb - 1), g))],
                  blk, _sds((tp, pool_w), BF16), (dp, dp), sem=("parallel", "parallel"), side=side)


def _gate_tiles(tp, d, g_off):
    tc = _pick(math.gcd(g_off, d), 512, 256, 128)
    tr = _pick(tp, 384, 128)
    return tr, tc


def _mix_fwd(proj, b_gate, a_out, p_out, g_off, side=None):
    tp, d = a_out.shape
    tr, tc = _gate_tiles(tp, d, g_off)
    c0, c1 = g_off // tc, (g_off + d) // tc

    def body(g0_ref, g1_ref, b_ref, a_ref, p_ref, o_ref):
        g0 = jax.nn.sigmoid(g0_ref[...] + b_ref[0:1, :])
        g1 = jax.nn.sigmoid(g1_ref[...] + b_ref[1:2, :])
        o_ref[...] = (g0 * a_ref[...] + g1 * p_ref[...]).astype(BF16)

    blk = pl.BlockSpec((tr, tc), lambda i, j: (i, j))
    return _gcall(body, "mix_fwd", (tp // tr, d // tc),
                  [pl.BlockSpec((tr, tc), lambda i, j: (i, c0 + j)), pl.BlockSpec((tr, tc), lambda i, j: (i, c1 + j)),
                   pl.BlockSpec((2, tc), lambda i, j: (0, j)), blk, blk],
                  blk, _sds((tp, d), BF16), (proj, proj, b_gate, a_out, p_out), sem=("parallel", "parallel"), side=side)


def _mix_bwd(proj, b_gate, a_out, p_out, d_mixed, g_off, side=None):
    tp, d = a_out.shape
    tr, tc = _gate_tiles(tp, d, g_off)
    c0, c1 = g_off // tc, (g_off + d) // tc

    def body(g0_ref, g1_ref, b_ref, a_ref, p_ref, d_ref, da_ref, dp_ref, dl0_ref, dl1_ref, db_ref):
        g0 = jax.nn.sigmoid(g0_ref[...] + b_ref[0:1, :])
        g1 = jax.nn.sigmoid(g1_ref[...] + b_ref[1:2, :])
        dm = d_ref[...]
        da_ref[...] = (dm * g0).astype(BF16)
        dp_ref[...] = (dm * g1).astype(BF16)
        dl0 = dm * a_ref[...] * g0 * (1.0 - g0)
        dl1 = dm * p_ref[...] * g1 * (1.0 - g1)
        dl0_ref[...] = dl0.astype(BF16)
        dl1_ref[...] = dl1.astype(BF16)

        @pl.when(pl.program_id(1) == 0)
        def _():
            db_ref[...] = jnp.zeros_like(db_ref)

        db_ref[...] += jnp.concatenate([jnp.sum(dl0, axis=0, keepdims=True), jnp.sum(dl1, axis=0, keepdims=True)], axis=0)

    blk = pl.BlockSpec((tr, tc), lambda j, i: (i, j))
    big = _sds((tp, d), BF16)
    return _gcall(body, "mix_bwd", (d // tc, tp // tr),
                  [pl.BlockSpec((tr, tc), lambda j, i: (i, c0 + j)), pl.BlockSpec((tr, tc), lambda j, i: (i, c1 + j)),
                   pl.BlockSpec((2, tc), lambda j, i: (0, j)), blk, blk, blk],
                  [blk, blk, blk, blk, pl.BlockSpec((2, tc), lambda j, i: (0, j))], [big, big, big, big, _sds((2, d), F32)],
                  (proj, proj, b_gate, a_out, p_out, d_mixed), sem=("parallel", "arbitrary"), side=side)


SWIGLU_ROWS = 64


def _swiglu_fwd(ff, side=None):
    tp, f2 = ff.shape
    f = f2 // 2
    tr = _pick(tp, SWIGLU_ROWS)

    def body(x_ref, o_ref):
        gate, up = x_ref[:, :f], x_ref[:, f:]
        o_ref[...] = (gate * jax.nn.sigmoid(gate) * up).astype(BF16)

    return _gcall(body, "swiglu_fwd", (tp // tr,), [pl.BlockSpec((tr, f2), lambda i: (i, 0))], pl.BlockSpec((tr, f), lambda i: (i, 0)),
                  _sds((tp, f), BF16), (ff,), sem=("parallel",), side=side)


def _swiglu_bwd(ff, d_act, side=None):
    tp, f2 = ff.shape
    f = f2 // 2
    tr = _pick(tp, SWIGLU_ROWS)

    def body(x_ref, d_ref, o_ref):
        gate, up = x_ref[:, :f], x_ref[:, f:]
        d = d_ref[...]
        sg = jax.nn.sigmoid(gate)
        silu = gate * sg
        o_ref[:, :f] = (d * up * (sg + silu * (1.0 - sg))).astype(BF16)
        o_ref[:, f:] = (d * silu).astype(BF16)

    return _gcall(body, "swiglu_bwd", (tp // tr,), [pl.BlockSpec((tr, f2), lambda i: (i, 0)), pl.BlockSpec((tr, f), lambda i: (i, 0))],
                  pl.BlockSpec((tr, f2), lambda i: (i, 0)), _sds((tp, f2), BF16), (ff, d_act), sem=("parallel",), side=side)


def _tile2(rows, cols, max_bytes=3 << 20):
    tc = _pick(cols, 1024, 640, 512)
    for tr in (512, 344, 256, 128, 64, 32, 16, 8):
        if rows % tr == 0 and tr * tc * 4 <= max_bytes:
            return tr, tc
    return rows, tc


def _cast_into_full(w, win, who, name, side=None):
    r, c = w.shape
    tr, tc = _tile2(r, c)

    def body(who_ref, x_ref, o_ref):
        o_ref[...] = x_ref[...].astype(BF16)

    if win.kind == "col":
        own = pl.BlockSpec((tr, tc), lambda i, j, who_ref: (i, who_ref[1] * (c // tc) + j))
    else:
        own = pl.BlockSpec((tr, tc), lambda i, j, who_ref: (who_ref[1] * (r // tr) + i, j))
    return _gcall(body, name, (r // tr, c // tc), [pl.BlockSpec((tr, tc), lambda i, j, who_ref: (i, j))], own,
                  _sds(win.full_shape, BF16), (w,), sem=("parallel", "parallel"), side=side, prefetch=(who,))


def _adamw(w, g, m, v, name, side=None):
    r, c = w.shape
    tr, tc = _tile2(r, c, 1 << 20)

    def body(w_ref, g_ref, m_ref, v_ref, d_ref, nm_ref, nv_ref):
        gg = g_ref[...]
        nm = ADAM_B1 * m_ref[...] + (1.0 - ADAM_B1) * gg
        nv = ADAM_B2 * v_ref[...] + (1.0 - ADAM_B2) * jnp.square(gg)
        m_hat = nm / (1.0 - ADAM_B1 ** ADAM_STEP)
        v_hat = nv / (1.0 - ADAM_B2 ** ADAM_STEP)
        d_ref[...] = -ADAM_LR * (m_hat / (jnp.sqrt(v_hat) + ADAM_EPS) + ADAM_WD * w_ref[...])
        nm_ref[...] = nm
        nv_ref[...] = nv

    blk = pl.BlockSpec((tr, tc), lambda i, j: (i, j))
    shp = _sds((r, c), F32)
    return _gcall(body, name, (r // tr, c // tc), [blk] * 4, [blk] * 3, [shp] * 3, (w, g, m, v), sem=("parallel", "parallel"), side=side)


def _piece_block_index(win, tr, tc):
    r, c = win.shard_shape
    if win.kind == "col":
        return lambda s, h, i, j: (win.row0 // tr + h * (win.half // tr) + i, s * (c // tc) + j)
    return lambda s, h, i, j: ((s * r + win.row0) // tr + h * (win.half // tr) + i, j)


def _chip_sum(g, who):
    pr, pc = g.win.piece_shape
    tr, tc = _tile2(pr, pc)
    full_idx = _piece_block_index(g.src, tr, tc)

    def body(who_ref, g_ref, o_ref, out_ref):
        out_ref[...] = (g_ref[...] + o_ref[...]).astype(BF16)

    slot = pl.BlockSpec((None, tr, tc), lambda s, i, j, who_ref: (s, i, j))
    g.chip = pl.pallas_call(
        body,
        name="chip_sum_" + g.key.replace("@", "_"),
        grid_spec=pltpu.PrefetchScalarGridSpec(
            num_scalar_prefetch=1,
            grid=(N_CHIPS, pr // tr, pc // tc),
            in_specs=[pl.BlockSpec((tr, tc), lambda s, i, j, who_ref: full_idx(s, who_ref[0], i, j)), slot],
            out_specs=slot,
        ),
        out_shape=_sds((N_CHIPS, pr, pc), BF16),
        compiler_params=_cparams(("parallel", "parallel", "parallel")),
    )(who, g.grad, g.other)


def _final_sum(g, shard, who):
    win = g.win
    pr, pc = win.piece_shape
    tr, tc = _tile2(pr, pc)
    full_idx = _piece_block_index(g.src, tr, tc)
    has_prev = shard.arr is not None

    def body(who_ref, g_ref, o_ref, l1_ref, l2_ref, l3_ref, *rest):
        out_ref = rest[-1]
        acc = g_ref[...] + o_ref[...]
        for l_ref in (l1_ref, l2_ref, l3_ref):
            acc = acc + l_ref[...].astype(F32)
        out_ref[...] = acc

    def landed_spec(k):
        return pl.BlockSpec((None, tr, tc), lambda i, j, who_ref: (who_ref[1 + k], i, j))

    in_specs = [pl.BlockSpec((tr, tc), lambda i, j, who_ref: full_idx(who_ref[1], who_ref[0], i, j)),
                pl.BlockSpec((None, tr, tc), lambda i, j, who_ref: (who_ref[1], i, j)),
                landed_spec(1), landed_spec(2), landed_spec(3)]
    args = [who, g.grad, g.other, g.landed, g.landed, g.landed]
    if has_prev:
        in_specs.append(ANY)
        args.append(shard.arr)
    shard.arr = pl.pallas_call(
        body,
        name="final_sum_" + g.key.replace("@", "_"),
        grid_spec=pltpu.PrefetchScalarGridSpec(
            num_scalar_prefetch=1,
            grid=(pr // tr, pc // tc),
            in_specs=in_specs,
            out_specs=pl.BlockSpec((tr, tc), lambda i, j, who_ref: (win.row0 // tr + who_ref[0] * (pr // tr) + i, j)),
        ),
        out_shape=_sds(win.shard_shape, F32),
        input_output_aliases={6: 0} if has_prev else {},
        compiler_params=_cparams(("parallel", "parallel")),
    )(*args)


def _gather_small(packed):
    r, c = packed.shape

    def body(in_ref, out_ref, send_sems, recv_sems):
        x, y, c_ = _coords()
        s_me = 2 * x + y
        out_ref[s_me] = in_ref[...]
        copies = []
        for j, (ox, oy) in enumerate(_other_chips(x, y)):
            cp = pltpu.make_async_remote_copy(src_ref=in_ref, dst_ref=out_ref.at[s_me], send_sem=send_sems.at[j],
                                              recv_sem=recv_sems.at[j], device_id=(ox, oy, c_), device_id_type=MESH)
            cp.start()
            copies.append(cp)
        for j, (ox, oy) in enumerate(_other_chips(x, y)):
            copies[j].wait_send()
            pltpu.make_async_remote_copy(src_ref=in_ref, dst_ref=out_ref.at[2 * ox + oy], send_sem=send_sems.at[j],
                                         recv_sem=recv_sems.at[j], device_id=(x, y, c_), device_id_type=MESH).wait_recv()

    return pl.pallas_call(
        body,
        name="gather_small",
        in_specs=[VMEM_FULL],
        out_specs=VMEM_FULL,
        out_shape=_sds((N_CHIPS, r, c), F32),
        scratch_shapes=[pltpu.SemaphoreType.DMA((3,)), pltpu.SemaphoreType.DMA((3,))],
    )(packed)


def _all_reduce_small(packed):
    r, c = packed.shape

    def body(in_ref, out_ref, slots, send_sems, recv_sems):
        x, y, c_ = _coords()
        me = 4 * x + 2 * y + c_
        slots[me] = in_ref[...]
        copies = []
        for k in range(1, N_DEV):
            peer = me ^ k
            cp = pltpu.make_async_remote_copy(src_ref=in_ref, dst_ref=slots.at[me], send_sem=send_sems.at[k - 1],
                                              recv_sem=recv_sems.at[k - 1],
                                              device_id=(peer // 4, (peer // 2) % 2, peer % 2), device_id_type=MESH)
            cp.start()
            copies.append(cp)
        for k in range(1, N_DEV):
            copies[k - 1].wait_send()
            pltpu.make_async_remote_copy(src_ref=in_ref, dst_ref=slots.at[me ^ k], send_sem=send_sems.at[k - 1],
                                         recv_sem=recv_sems.at[k - 1], device_id=(x, y, c_), device_id_type=MESH).wait_recv()
        acc = slots[0]
        for d in range(1, N_DEV):
            acc = acc + slots[d]
        out_ref[...] = acc

    return pl.pallas_call(
        body,
        name="all_reduce_small",
        in_specs=[VMEM_FULL],
        out_specs=VMEM_FULL,
        out_shape=_sds((r, c), F32),
        scratch_shapes=[pltpu.VMEM((N_DEV, r, c), F32), pltpu.SemaphoreType.DMA((N_DEV - 1,)), pltpu.SemaphoreType.DMA((N_DEV - 1,))],
    )(packed)


def _rows_of(a, width):
    flat = a.reshape(-1)
    n = -(-flat.shape[0] // width) * width
    return jnp.pad(flat, (0, n - flat.shape[0])).reshape(-1, width)


def _pad_rows(a, mult=8):
    n = -(-a.shape[0] // mult) * mult
    return jnp.pad(a, ((0, n - a.shape[0]), (0, 0)))


def _heads_major(a, nh):
    tp = a.shape[0]
    return a.reshape(tp, nh, HEAD_DIM).transpose(1, 0, 2)


def _heads_minor(a):
    nh, tp, hd = a.shape
    return a.transpose(1, 0, 2).reshape(tp, nh * hd)


def kernel(x, meta_tokens, ln_in_g, ln_in_b, w_in, b_gate, attn_sinks, w_attn_up, w_pool_grp, pool_scale, w_pool_up, w_out, ln1_g, ln1_b, w_ffn_in, w_ffn_down, ln2_g, ln2_b, loss_target, m_meta_tokens, m_ln_in_g, m_ln_in_b, m_w_in, m_b_gate, m_attn_sinks, m_w_attn_up, m_w_pool_grp, m_pool_scale, m_w_pool_up, m_w_out, m_ln1_g, m_ln1_b, m_w_ffn_in, m_w_ffn_down, m_ln2_g, m_ln2_b, v_meta_tokens, v_ln_in_g, v_ln_in_b, v_w_in, v_b_gate, v_attn_sinks, v_w_attn_up, v_w_pool_grp, v_pool_scale, v_w_pool_up, v_w_out, v_ln1_g, v_ln1_b, v_w_ffn_in, v_w_ffn_down, v_ln2_g, v_ln2_b):
    seq, d = x.shape[1], x.shape[2]
    tp = LEAD + N_META + seq
    nb = tp // BLK
    nq = attn_sinks.shape[1]
    grp = nq // N_KV
    attn_w = nq * HEAD_DIM
    kv_w = N_KV * HEAD_DIM
    qkv_w = attn_w + 2 * kv_w
    pool_w = pool_scale.shape[1]
    gw = pool_w // N_GRP
    g_off = qkv_w + pool_w
    dc = d // N_CHIPS
    cx, cy, cc = _coords()
    s_me = 2 * cx + cy
    who = jnp.stack([cc, s_me, (s_me + 1) % N_CHIPS, (s_me + 2) % N_CHIPS, (s_me + 3) % N_CHIPS]).astype(jnp.int32)

    names = ["w_in", "w_attn_up", "w_pool_grp", "w_pool_up", "w_out", "w_ffn_in", "w_ffn_down"]
    kinds = dict(w_in="col", w_attn_up="col", w_pool_grp="row", w_pool_up="col", w_out="row", w_ffn_in="col", w_ffn_down="row")
    grp_shard = (N_GRP * (gw // N_CHIPS), gw)
    big_w = dict(w_in=w_in[0], w_attn_up=w_attn_up[0], w_pool_grp=w_pool_grp[0].reshape(grp_shard), w_pool_up=w_pool_up[0],
                 w_out=w_out[0], w_ffn_in=w_ffn_in[0], w_ffn_down=w_ffn_down[0])
    big_m = dict(w_in=m_w_in[0], w_attn_up=m_w_attn_up[0], w_pool_grp=m_w_pool_grp[0].reshape(grp_shard), w_pool_up=m_w_pool_up[0],
                 w_out=m_w_out[0], w_ffn_in=m_w_ffn_in[0], w_ffn_down=m_w_ffn_down[0])
    big_v = dict(w_in=v_w_in[0], w_attn_up=v_w_attn_up[0], w_pool_grp=v_w_pool_grp[0].reshape(grp_shard), w_pool_up=v_w_pool_up[0],
                 w_out=v_w_out[0], w_ffn_in=v_w_ffn_in[0], w_ffn_down=v_w_ffn_down[0])
    small_rows = _pad_rows(jnp.concatenate([meta_tokens, b_gate[0]], axis=0))
    gathered = _gather_small(small_rows)
    gathered = gathered.transpose(1, 0, 2).reshape(small_rows.shape[0], d)
    meta_full, b_gate_full = gathered[:N_META], gathered[N_META:N_META + 2]

    W = {}

    def cast(n, side=None):
        win = _Win(kinds[n], big_w[n].shape)
        W[n] = _Weight(n, win, _cast_into_full(big_w[n], win, who, "cast_" + n, side=side))

    def whole(*ns):
        return [(W[n], W[n].win) for n in ns]

    def legs(first=(), second=(), third=()):
        ops = [_GatherD2d(third, DIAGONAL)] if third else []
        ops += [_GatherRing(second), _GatherD2d(second, NEIGHBOURS)] if second else []
        ops += [_GatherIci(first)] if first else []
        return _Side(ops)

    mid = ("w_attn_up", "w_pool_grp", "w_pool_up", "w_out")
    for n in ("w_in",) + mid:
        cast(n)
    cast("w_ffn_in", side=legs(first=whole("w_in")))
    cast("w_ffn_down", side=legs(second=whole("w_in")))
    wins = {n: W[n].win for n in names}
    ffn_in_parts = [(W["w_ffn_in"], win) for win in W["w_ffn_in"].win.split(4)]
    x2d, tgt2d = x[0], loss_target[0]
    g_in, b_in = ln_in_g.reshape(1, d), ln_in_b.reshape(1, d)
    h0, h0b = _ln_in_fwd(x2d, meta_full, g_in, b_in, nb, side=legs(first=whole(*mid), third=whole("w_in")))
    proj = _mm(h0b, W["w_in"].full, "nn", F32, 1408, 512, 4096, "mm_proj", side=legs(first=ffn_in_parts[0:3], second=whole(*mid)))
    cos, sin = _rope_tables(tp)
    n_rot = (attn_w + kv_w) // LANES
    qkv = _rope_fwd(proj, cos, sin, n_rot, qkv_w, side=legs(third=whole(*mid)))
    q_hm = _heads_major(qkv[:, :attn_w], nq)
    k_hm = _heads_major(qkv[:, attn_w:attn_w + kv_w], N_KV)
    v_hm = _heads_major(qkv[:, attn_w + kv_w:], N_KV)
    sink4 = jnp.broadcast_to(attn_sinks.reshape(N_KV, grp, 1, 1), (N_KV, grp, BLK, 1))
    o_hm = _attn_fwd(q_hm, k_hm, v_hm, sink4, side=legs(first=ffn_in_parts[3:4], second=ffn_in_parts[0:3]))
    o = _heads_minor(o_hm)
    wf_grp = W["w_pool_grp"].full.reshape(N_CHIPS, N_GRP, gw // N_CHIPS, gw).transpose(1, 0, 2, 3).reshape(N_GRP, gw, gw)
    pooled, mx, pm = _pool_fwd(proj, wf_grp, pool_scale, qkv_w, pool_w, side=legs(second=ffn_in_parts[3:4], third=ffn_in_parts[0:3]))
    a_out = _mm(o, W["w_attn_up"].full, "nn", F32, 1408, 1024, 2048, "mm_attn_up", side=legs(third=ffn_in_parts[3:4]))
    p_out = _mm(pm, W["w_pool_up"].full, "nn", F32, 1408, 1024, 2048, "mm_pool_up")
    mixed = _mix_fwd(proj, b_gate_full, a_out, p_out, g_off)
    z1 = _mm(mixed, W["w_out"].full, "nn", F32, 1408, 512, 4096, "mm_out", side=legs(first=whole("w_ffn_down")))
    r1, h1, h1b = _res_ln_fwd(h0, z1, ln1_g, ln1_b, side=legs(second=whole("w_ffn_down")))
    ff = _mm(h1b, W["w_ffn_in"].full, "nn", F32, 1408, 512, 4096, "mm_ffn_in", side=legs(third=whole("w_ffn_down")))
    act = _swiglu_fwd(ff)
    wf_down = W["w_ffn_down"].full
    z2 = _mm(act, wf_down, "nn", F32, 704, 1024, 5504, "mm_ffn_down")
    d_r2, d_r2b, loss_tile, dg2, db2 = _final_ln_loss(h1, z2, ln2_g, ln2_b, tgt2d)

    S = {n: _Shard(n) for n in names}

    def grads_of(name, grad, parts=1):
        return [_Grad(name, win, grad) for win in wins[name].split(parts)]

    def sibling(gs):
        return _ReduceSibling(gs)

    def chips(gs):
        for g in gs:
            _chip_sum(g, who)
        return _ReduceChips(gs)

    def share(gs):
        for g in gs:
            _final_sum(g, S[g.name], who)
        return _ShareReduced([(S[g.name], g.win) for g in gs])

    def row_blocks(name, act_b, d_out, parts, sides):
        out = []
        for k, win in enumerate(wins[name].split(parts)):
            blk = _mm(act_b[:, win.row0:win.row0 + win.nrows], d_out, "tn", F32, 1024, 512, tp, "mm_g%s_%d" % (name, k), side=sides(k, out))
            out.append(_Grad(name, win, blk, win.row0))
        return out

    g6 = grads_of("w_ffn_down", _mm(act, d_r2b, "tn", F32, 256, 2048, tp, "mm_gw_ffn_down", j_outer=True))
    d_act = _mm(d_r2b, wf_down, "nt", F32, 2112, 256, 4096, "mm_d_act", side=_Side([sibling(g6)]))
    d_ff = _swiglu_bwd(ff, d_act)
    g5 = row_blocks("w_ffn_in", h1b, d_ff, 2, lambda k, done: _Side([chips(g6)] if k == 0 else [sibling(done[0:1]), share(g6)]))
    d_h1_mm = _mm(d_ff, W["w_ffn_in"].full, "nt", F32, 704, 1024, 5504, "mm_d_h1", side=_Side([chips(g5[0:1]), sibling(g5[1:2])]))
    d_r1, d_r1b, dg1, db1 = _ln1_bwd(d_r2, d_h1_mm, r1, ln1_g)
    g4 = grads_of("w_out", _mm(mixed, d_r1b, "tn", F32, 1024, 512, tp, "mm_gw_out", side=_Side([share(g5[0:1])])))
    d_mixed = _mm(d_r1b, W["w_out"].full, "nt", F32, 1408, 512, 4096, "mm_d_mixed", side=_Side([sibling(g4)]))
    d_a, d_p, d_gl0, d_gl1, d_bgate = _mix_bwd(proj, b_gate_full, a_out, p_out, d_mixed, g_off, side=_Side([chips(g4)]))
    g1 = grads_of("w_attn_up", _mm(o, d_a, "tn", F32, 1024, 512, tp, "mm_gw_attn_up", side=_Side([share(g4)])))
    d_o = _mm(d_a, W["w_attn_up"].full, "nt", F32, 1408, 512, 4096, "mm_d_o", side=_Side([sibling(g1)]))
    g3 = grads_of("w_pool_up", _mm(pm, d_p, "tn", F32, 1024, 512, tp, "mm_gw_pool_up", side=_Side([chips(g1)])))
    d_pm = _mm(d_p, W["w_pool_up"].full, "nt", F32, 1408, 512, 4096, "mm_d_pm", side=_Side([sibling(g3)]))
    d_pooled, gw_grp, d_scale = _pool_bwd_mix(d_pm, mx, pooled, wf_grp, pool_scale, side=_Side([chips(g3), share(g1)]))
    gw_grp_sm = gw_grp.reshape(N_GRP, N_CHIPS, gw // N_CHIPS, gw).transpose(1, 0, 2, 3).reshape(N_CHIPS * grp_shard[0], gw)
    g2 = grads_of("w_pool_grp", gw_grp_sm)
    d_u = _pool_bwd_band(d_pooled, side=_Side([sibling(g2), share(g3)]))
    dq_hm, dk_cur, dk_prev, dk_meta, dv_cur, dv_prev, dv_meta, d_sink = _attn_bwd(
        q_hm, k_hm, v_hm, sink4, _heads_major(d_o, nq), side=_Side([chips(g5[1:2] + g2)]))
    d_qkv = _rope_bwd(_heads_minor(dq_hm), _heads_minor(dk_cur), _heads_minor(dk_prev), _heads_minor(dk_meta),
                      _heads_minor(dv_cur), _heads_minor(dv_prev), _heads_minor(dv_meta), cos, sin, side=_Side([share(g2)]))
    d_proj = jnp.concatenate([d_qkv, d_u, d_gl0, d_gl1], axis=1)
    def gw_in_sides(k, done):
        ops = [share(g5[1:2])] if k == 0 else [sibling(done[k - 1:k])]
        if k >= 2:
            ops.append(chips(done[k - 2:k - 1]))
        if k >= 3:
            ops.append(share(done[k - 3:k - 2]))
        return _Side(ops)

    g0 = row_blocks("w_in", h0b, d_proj, 4, gw_in_sides)
    d_h0_mm = _mm(d_proj, W["w_in"].full, "nt", F32, 1408, 1024, 2560, "mm_d_h0", side=_Side([chips(g0[2:3]), sibling(g0[3:4]), share(g0[1:2])]))
    grad_x2d, d_meta, dg_in, db_in = _ln_in_bwd(d_r1, d_h0_mm, x2d, meta_full, g_in, side=_Side([chips(g0[3:4]), share(g0[2:3])]))

    small_parts = [d_meta, d_bgate, dg_in, db_in, dg1, db1, dg2, db2, _rows_of(d_scale, d), _rows_of(d_sink[:, :, 0, 0], d)]
    offs = [0]
    for p in small_parts:
        offs.append(offs[-1] + p.shape[0])
    red = _all_reduce_small(_pad_rows(jnp.concatenate(small_parts, axis=0)))
    r_meta, r_bgate, r_g_in, r_b_in, r_g1, r_b1, r_g2, r_b2, r_scale, r_sink = [red[offs[k]:offs[k + 1]] for k in range(len(small_parts))]
    col0 = s_me * dc
    g_meta = lax.dynamic_slice(r_meta, (0, col0), (N_META, dc))
    g_bgate = lax.dynamic_slice(r_bgate, (0, col0), (2, dc))
    g_scale = r_scale.reshape(-1)[:pool_w]
    g_sink = r_sink.reshape(-1)[:nq]

    upd = {}

    def adamw(n, side=None):
        upd[n] = _adamw(big_w[n], S[n].arr, big_m[n], big_v[n], "adamw_" + n, side=side)

    adamw("w_out", side=_Side([share(g0[3:4])]))
    for n in ("w_ffn_in", "w_ffn_down", "w_attn_up", "w_pool_grp", "w_pool_up", "w_in"):
        adamw(n)

    small_w = [meta_tokens, b_gate[0], ln_in_g, ln_in_b, attn_sinks, pool_scale, ln1_g, ln1_b, ln2_g, ln2_b]
    small_m = [m_meta_tokens, m_b_gate[0], m_ln_in_g, m_ln_in_b, m_attn_sinks, m_pool_scale, m_ln1_g, m_ln1_b, m_ln2_g, m_ln2_b]
    small_v = [v_meta_tokens, v_b_gate[0], v_ln_in_g, v_ln_in_b, v_attn_sinks, v_pool_scale, v_ln1_g, v_ln1_b, v_ln2_g, v_ln2_b]
    small_g = [g_meta, g_bgate, r_g_in, r_b_in, g_sink, g_scale, r_g1, r_b1, r_g2, r_b2]
    small_g = [g.reshape(w.shape) for g, w in zip(small_g, small_w)]

    def pack(parts):
        return _pad_rows(jnp.concatenate([_rows_of(p, dc) for p in parts], axis=0))

    s_delta, s_m, s_v = _adamw(pack(small_w), pack(small_g), pack(small_m), pack(small_v), "adamw_small")

    def unpack(packed):
        out, row = [], 0
        for w in small_w:
            nrow = -(-w.size // dc)
            out.append(packed[row:row + nrow].reshape(-1)[:w.size].reshape(w.shape))
            row += nrow
        return out

    s_delta, s_m, s_v = unpack(s_delta), unpack(s_m), unpack(s_v)

    order = ["meta_tokens", "ln_in_g", "ln_in_b", "w_in", "b_gate", "attn_sinks", "w_attn_up", "w_pool_grp", "pool_scale",
             "w_pool_up", "w_out", "ln1_g", "ln1_b", "w_ffn_in", "w_ffn_down", "ln2_g", "ln2_b"]
    small_names = ["meta_tokens", "b_gate", "ln_in_g", "ln_in_b", "attn_sinks", "pool_scale", "ln1_g", "ln1_b", "ln2_g", "ln2_b"]
    out_shapes = dict(meta_tokens=meta_tokens.shape, ln_in_g=ln_in_g.shape, ln_in_b=ln_in_b.shape, w_in=w_in.shape, b_gate=b_gate.shape,
                      attn_sinks=attn_sinks.shape, w_attn_up=w_attn_up.shape, w_pool_grp=w_pool_grp.shape, pool_scale=pool_scale.shape,
                      w_pool_up=w_pool_up.shape, w_out=w_out.shape, ln1_g=ln1_g.shape, ln1_b=ln1_b.shape, w_ffn_in=w_ffn_in.shape,
                      w_ffn_down=w_ffn_down.shape, ln2_g=ln2_g.shape, ln2_b=ln2_b.shape)
    grads, deltas, new_m, new_v = {}, {}, {}, {}
    for n in names:
        grads[n], (deltas[n], new_m[n], new_v[n]) = S[n].arr, upd[n]
    for k, n in enumerate(small_names):
        grads[n], deltas[n], new_m[n], new_v[n] = small_g[k], s_delta[k], s_m[k], s_v[k]

    loss = lax.psum(loss_tile[0, 0], ("x", "y", "c"))
    outs = [loss, grad_x2d.reshape(x.shape)]
    for group in (grads, deltas, new_m, new_v):
        outs += [group[n].reshape(out_shapes[n]) for n in order]
    return tuple(outs)
```

```python
import functools
import math

import jax
import jax.numpy as jnp
from jax import lax
from jax.experimental import pallas as pl
from jax.experimental.pallas import tpu as pltpu

F32 = jnp.float32
BF16 = jnp.bfloat16
MESH = pl.DeviceIdType.MESH
ANY = pl.BlockSpec(memory_space=pl.ANY)
VMEM_FULL = pl.BlockSpec(memory_space=pltpu.VMEM)

N_META = 16
HEAD_DIM = 64
N_KV = 4
BLK = 128
LEAD = (-N_META) % BLK
ROPE_DIM = HEAD_DIM // 4
ROPE_THETA = 500000.0
NEG_INF = -1e30
POOL_WINDOWS = (2, 4, 8, 16)
N_GRP = len(POOL_WINDOWS)
LN_EPS = 1e-5
DN_ALPHA = 2.0 ** 0.25
ADAM_LR = 0.001
ADAM_B1 = 0.9
ADAM_B2 = 0.999
ADAM_EPS = 1e-08
ADAM_WD = 0.01
ADAM_STEP = 10
N_CHIPS = 4
N_DEV = 8
LANES = 128
VMEM_LIMIT_MB = 56
HI = lax.Precision.HIGHEST


def _cparams(sem=None, vmem_mb=VMEM_LIMIT_MB):
    kw = dict(vmem_limit_bytes=vmem_mb << 20)
    if sem is not None:
        kw["dimension_semantics"] = sem
    return pltpu.CompilerParams(**kw)


def _pick(dim, *cands):
    for c in cands:
        if c <= dim and dim % c == 0:
            return c
    return dim


def _sds(shape, dtype):
    return jax.ShapeDtypeStruct(tuple(shape), dtype)


def _coords():
    return lax.axis_index("x"), lax.axis_index("y"), lax.axis_index("c")


def _other_chips(x, y):
    return [(1 - x, y), (x, 1 - y), (1 - x, 1 - y)]


class _Win:
    def __init__(self, kind, shard_shape, row0=0, nrows=None):
        self.kind, self.shard_shape, self.row0 = kind, tuple(shard_shape), row0
        self.nrows = shard_shape[0] if nrows is None else nrows
        self.half = self.nrows // 2

    @property
    def piece_shape(self):
        return (self.half, self.shard_shape[1])

    @property
    def full_shape(self):
        r, c = self.shard_shape
        return (r, N_CHIPS * c) if self.kind == "col" else (N_CHIPS * r, c)

    def in_full(self, ref, s, h, q=None):
        r, c = self.shard_shape
        start, size = self.row0 + h * self.half, self.half
        if q is not None:
            start, size = start + q * (self.half // 2), self.half // 2
        if self.kind == "col":
            return ref.at[pl.ds(start, size), pl.ds(s * c, c)]
        return ref.at[pl.ds(s * r + start, size), :]

    def in_shard(self, ref, h):
        return ref.at[pl.ds(self.row0 + h * self.half, self.half), :]

    def split(self, n):
        return [_Win(self.kind, self.shard_shape, self.row0 + q * (self.nrows // n), self.nrows // n) for q in range(n)]


class _Weight:
    def __init__(self, name, win, full):
        self.name, self.win, self.full = name, win, full


class _Grad:
    def __init__(self, name, win, grad, grad_row0=0):
        self.name, self.win, self.grad = name, win, grad
        self.key = "%s@%d" % (name, win.row0)
        self.grad_key = "%s@%d" % (name, grad_row0)
        self.src = _Win(win.kind, win.shard_shape, win.row0 - grad_row0, win.nrows)
        self.other = self.chip = self.landed = None


class _Shard:
    def __init__(self, name):
        self.name, self.arr = name, None


COPY_STREAMS = 8
BF16_ROWS = 16


def _stream_views(ref):
    rows, cols = ref.shape
    n = COPY_STREAMS
    if rows % (n * BF16_ROWS) == 0:
        return [ref.at[pl.ds(i * (rows // n), rows // n), :] for i in range(n)]
    if cols % (n * LANES) == 0:
        return [ref.at[:, pl.ds(i * (cols // n), cols // n)] for i in range(n)]
    return [ref]


class _StreamedCopy:
    def __init__(self, make, src, dst):
        self.whole = make(src, dst)
        self.parts = [make(s, d) for s, d in zip(_stream_views(src), _stream_views(dst))]

    def start(self):
        for cp in self.parts:
            cp.start()

    def wait(self):
        self.whole.wait()

    def wait_send(self):
        self.whole.wait_send()

    def wait_recv(self):
        self.whole.wait_recv()


class _Ctx:
    def __init__(self, side, in_refs, out_refs, send_sems, recv_sems, base):
        self.side, self.in_refs, self.out_refs = side, in_refs, out_refs
        self.send_sems, self.recv_sems, self.base = send_sems, recv_sems, base

    def ref(self, key):
        info = self.side.info[key]
        return self.in_refs[info["in"]] if info["in"] is not None else self.out_refs[info["out"]]

    def remote(self, k, src, dst, to):
        def make(s, d):
            return pltpu.make_async_remote_copy(src_ref=s, dst_ref=d, send_sem=self.send_sems.at[self.base + k],
                                                recv_sem=self.recv_sems.at[self.base + k], device_id=to, device_id_type=MESH)

        return _StreamedCopy(make, src, dst)


class _Side:
    def __init__(self, ops):
        self.ops, self.info, self.keys = ops, {}, []
        self.nsem = 0
        self.bases = []
        for op in ops:
            op.register(self)
            self.bases.append(self.nsem)
            self.nsem += op.nsem
        self.inputs, self.out_shape, self.aliases = [], [], {}
        for key in self.keys:
            info = self.info[key]
            info["in"] = info["out"] = None
            if info["arr"] is not None:
                info["in"] = len(self.inputs)
                self.inputs.append(info["arr"])
            if info["write"]:
                info["out"] = len(self.out_shape)
                self.out_shape.append(info["sds"])
                if info["in"] is not None:
                    self.aliases[info["in"]] = info["out"]

    def need(self, key, arr=None, sds=None, write=False):
        if key not in self.info:
            self.keys.append(key)
            self.info[key] = dict(arr=arr, sds=sds if arr is None else _sds(arr.shape, arr.dtype), write=write)
        else:
            self.info[key]["write"] = self.info[key]["write"] or write

    def _ctx(self, k, in_refs, out_refs, sems):
        return _Ctx(self, in_refs, out_refs, sems[0], sems[1], self.bases[k])

    def start(self, in_refs, out_refs, sems):
        for k, op in enumerate(self.ops):
            op.start(self._ctx(k, in_refs, out_refs, sems))

    def finish(self, in_refs, out_refs, sems):
        for k, op in enumerate(self.ops):
            op.finish(self._ctx(k, in_refs, out_refs, sems))

    def scratch(self):
        return [pltpu.SemaphoreType.DMA((self.nsem,)), pltpu.SemaphoreType.DMA((self.nsem,))]

    def commit(self, outs):
        res = {key: outs[self.info[key]["out"]] for key in self.keys if self.info[key]["write"]}
        for op in self.ops:
            op.commit(res)


NEIGHBOURS, DIAGONAL = (0, 1), (2,)


class _GatherIci:
    def __init__(self, pairs):
        self.pairs = pairs
        self.nsem = 2 * len(pairs)

    def register(self, side):
        for w, win in self.pairs:
            side.need(("full", w.name), arr=w.full, write=True)

    def _copies(self, ctx):
        x, y, c = _coords()
        s_me = 2 * x + y
        sends, recvs = [], []
        for t, (w, win) in enumerate(self.pairs):
            full = ctx.ref(("full", w.name))
            for j in NEIGHBOURS:
                ox, oy = _other_chips(x, y)[j]
                mine, landing = win.in_full(full, s_me, c), win.in_full(full, 2 * ox + oy, c)
                sends.append(ctx.remote(2 * t + j, mine, mine, (ox, oy, c)))
                recvs.append(ctx.remote(2 * t + j, landing, landing, (x, y, c)))
        return sends, recvs

    def start(self, ctx):
        for cp in self._copies(ctx)[0]:
            cp.start()

    def finish(self, ctx):
        sends, recvs = self._copies(ctx)
        for cp in recvs:
            cp.wait_recv()
        for cp in sends:
            cp.wait_send()

    def commit(self, res):
        for w, _ in self.pairs:
            w.full = res[("full", w.name)]


class _GatherRing:
    def __init__(self, pairs):
        self.pairs = pairs
        self.nsem = 2 * len(pairs)

    def register(self, side):
        for w, win in self.pairs:
            side.need(("full", w.name), arr=w.full, write=True)

    def _copies(self, ctx):
        x, y, c = _coords()
        s_x, s_y, s_d = 2 * (1 - x) + y, 2 * x + (1 - y), 2 * (1 - x) + (1 - y)
        sends, recvs = [], []
        for t, (w, win) in enumerate(self.pairs):
            full = ctx.ref(("full", w.name))
            for q, (s_from, to) in enumerate([(s_x, (x, 1 - y, c)), (s_y, (1 - x, y, c))]):
                passed, landing = win.in_full(full, s_from, c, q), win.in_full(full, s_d, c, q)
                sends.append(ctx.remote(2 * t + q, passed, passed, to))
                recvs.append(ctx.remote(2 * t + q, landing, landing, (x, y, c)))
        return sends, recvs

    def start(self, ctx):
        for cp in self._copies(ctx)[0]:
            cp.start()

    def finish(self, ctx):
        sends, recvs = self._copies(ctx)
        for cp in recvs:
            cp.wait_recv()
        for cp in sends:
            cp.wait_send()

    def commit(self, res):
        for w, _ in self.pairs:
            w.full = res[("full", w.name)]


class _GatherD2d:
    def __init__(self, pairs, which):
        self.pairs, self.which = pairs, which
        self.nsem = 3 * len(pairs)

    def register(self, side):
        for w, win in self.pairs:
            side.need(("full", w.name), arr=w.full, write=True)

    def _copies(self, ctx):
        x, y, c = _coords()
        sends, recvs = [], []
        for t, (w, win) in enumerate(self.pairs):
            full = ctx.ref(("full", w.name))
            for j in self.which:
                ox, oy = _other_chips(x, y)[j]
                mine, theirs = win.in_full(full, 2 * ox + oy, c), win.in_full(full, 2 * ox + oy, 1 - c)
                sends.append(ctx.remote(3 * t + j, mine, mine, (x, y, 1 - c)))
                recvs.append(ctx.remote(3 * t + j, theirs, theirs, (x, y, c)))
        return sends, recvs

    def start(self, ctx):
        for cp in self._copies(ctx)[0]:
            cp.start()

    def finish(self, ctx):
        sends, recvs = self._copies(ctx)
        for cp in recvs:
            cp.wait_recv()
        for cp in sends:
            cp.wait_send()

    def commit(self, res):
        for w, _ in self.pairs:
            w.full = res[("full", w.name)]


class _ReduceSibling:
    def __init__(self, grads):
        self.grads = grads
        self.nsem = N_CHIPS * len(grads)

    def register(self, side):
        for g in self.grads:
            side.need(("grad", g.grad_key), arr=g.grad)
            side.need(("other", g.key), sds=_sds((N_CHIPS,) + g.win.piece_shape, F32), write=True)

    def _copies(self, ctx):
        x, y, c = _coords()
        out = []
        for t, g in enumerate(self.grads):
            grad, other = ctx.ref(("grad", g.grad_key)), ctx.ref(("other", g.key))
            for s in range(N_CHIPS):
                out.append(ctx.remote(N_CHIPS * t + s, g.src.in_full(grad, s, 1 - c), other.at[s], (x, y, 1 - c)))
        return out

    def start(self, ctx):
        for cp in self._copies(ctx):
            cp.start()

    def finish(self, ctx):
        for cp in self._copies(ctx):
            cp.wait()

    def commit(self, res):
        for g in self.grads:
            g.other = res[("other", g.key)]


class _ReduceChips:
    def __init__(self, grads):
        self.grads = grads
        self.nsem = 3 * len(grads)

    def register(self, side):
        for g in self.grads:
            side.need(("chip", g.key), arr=g.chip)
            side.need(("landed", g.key), sds=_sds(g.chip.shape, g.chip.dtype), write=True)

    def _copies(self, ctx):
        x, y, c = _coords()
        s_me = 2 * x + y
        out = []
        for t, g in enumerate(self.grads):
            chip, landed = ctx.ref(("chip", g.key)), ctx.ref(("landed", g.key))
            for j, (ox, oy) in enumerate(_other_chips(x, y)):
                out.append(ctx.remote(3 * t + j, chip.at[2 * ox + oy], landed.at[s_me], (ox, oy, c)))
        return out

    def start(self, ctx):
        for cp in self._copies(ctx):
            cp.start()

    def finish(self, ctx):
        for cp in self._copies(ctx):
            cp.wait()

    def commit(self, res):
        for g in self.grads:
            g.landed = res[("landed", g.key)]


class _ShareReduced:
    def __init__(self, items):
        self.items = items
        self.nsem = len(items)

    def register(self, side):
        for sh, _ in self.items:
            side.need(("reduced", sh.name), arr=sh.arr, write=True)

    def _copies(self, ctx):
        x, y, c = _coords()
        sends, recvs = [], []
        for t, (sh, win) in enumerate(self.items):
            ref = ctx.ref(("reduced", sh.name))
            sends.append(ctx.remote(t, win.in_shard(ref, c), win.in_shard(ref, c), (x, y, 1 - c)))
            recvs.append(ctx.remote(t, win.in_shard(ref, 1 - c), win.in_shard(ref, 1 - c), (x, y, c)))
        return sends, recvs

    def start(self, ctx):
        for cp in self._copies(ctx)[0]:
            cp.start()

    def finish(self, ctx):
        sends, recvs = self._copies(ctx)
        for cp in recvs:
            cp.wait_recv()
        for cp in sends:
            cp.wait_send()

    def commit(self, res):
        for sh, _ in self.items:
            sh.arr = res[("reduced", sh.name)]


def _gcall(body, name, grid, in_specs, out_specs, out_shape, args, scratch=(), sem=None, side=None, prefetch=()):
    single = not isinstance(out_shape, (list, tuple))
    out_shapes = [out_shape] if single else list(out_shape)
    out_specs = [out_specs] if single else list(out_specs)
    hosted = side is not None and bool(side.ops)
    n_pf, n_in, n_out, n_scr = len(prefetch), len(args), len(out_shapes), len(scratch)
    ns_in, ns_out = (len(side.inputs), len(side.out_shape)) if hosted else (0, 0)

    def wrapped(*refs):
        pf, refs = refs[:n_pf], refs[n_pf:]
        a, si = refs[:n_in], refs[n_in:n_in + ns_in]
        o = refs[n_in + ns_in:n_in + ns_in + n_out]
        so = refs[n_in + ns_in + n_out:n_in + ns_in + n_out + ns_out]
        rest = refs[n_in + ns_in + n_out + ns_out:]
        scr, sems = rest[:n_scr], rest[n_scr:]
        ids = [pl.program_id(k) for k in range(len(grid))]
        first = functools.reduce(jnp.logical_and, [i == 0 for i in ids])
        last = functools.reduce(jnp.logical_and, [i == g - 1 for i, g in zip(ids, grid)])

        @pl.when(first)
        def _():
            side.start(si, so, sems)

        body(*pf, *a, *o, *scr)

        @pl.when(last)
        def _():
            side.finish(si, so, sems)

    res = pl.pallas_call(
        wrapped if hosted else body, name=name,
        grid_spec=pltpu.PrefetchScalarGridSpec(
            num_scalar_prefetch=n_pf, grid=grid,
            in_specs=list(in_specs) + [ANY] * ns_in,
            out_specs=out_specs + [ANY] * ns_out,
            scratch_shapes=list(scratch) + (side.scratch() if hosted else []),
        ),
        out_shape=out_shapes + (side.out_shape if hosted else []),
        input_output_aliases={n_pf + n_in + i: n_out + j for i, j in side.aliases.items()} if hosted else {},
        compiler_params=_cparams(("arbitrary",) * len(grid) if hosted else sem),
    )(*prefetch, *args, *(side.inputs if hosted else []))
    if hosted:
        side.commit(res[n_out:])
    return res[0] if single else res[:n_out]


_DOT_DIMS = {
    "nn": (((1,), (0,)), ((), ())),
    "nt": (((1,), (1,)), ((), ())),
    "tn": (((0,), (0,)), ((), ())),
}


def _mm(a, b, mode, out_dtype, tm, tn, tk, name, j_outer=False, side=None, a_cols=None):
    if mode == "nn":
        (m, k), n = a.shape, b.shape[1]
    elif mode == "nt":
        (m, k), n = a.shape, b.shape[0]
    else:
        (k, m), n = a.shape, b.shape[1]
    col0 = 0
    if a_cols is not None:
        col0, m = a_cols
    tm, tn, tk = _pick(m, tm), _pick(n, tn), _pick(k, tk)
    gi, gj, gk = m // tm, n // tn, k // tk
    dims = _DOT_DIMS[mode]

    def ij(g0, g1):
        return (g1, g0) if j_outer else (g0, g1)

    if mode == "tn":
        a_spec = pl.BlockSpec((tk, tm), lambda g0, g1, kk: (kk, col0 // tm + ij(g0, g1)[0]))
    else:
        a_spec = pl.BlockSpec((tm, tk), lambda g0, g1, kk: (ij(g0, g1)[0], kk))
    if mode == "nt":
        b_spec = pl.BlockSpec((tn, tk), lambda g0, g1, kk: (ij(g0, g1)[1], kk))
    else:
        b_spec = pl.BlockSpec((tk, tn), lambda g0, g1, kk: (kk, ij(g0, g1)[1]))
    o_spec = pl.BlockSpec((tm, tn), lambda g0, g1, kk: ij(g0, g1))

    def body(a_ref, b_ref, o_ref, *scr):
        p = lax.dot_general(a_ref[...], b_ref[...], dims, preferred_element_type=F32)
        if gk == 1:
            o_ref[...] = p.astype(out_dtype)
        else:
            acc = scr[0]
            kk = pl.program_id(2)

            @pl.when(kk == 0)
            def _():
                acc[...] = p

            @pl.when(kk > 0)
            def _():
                acc[...] += p

            @pl.when(kk == gk - 1)
            def _():
                o_ref[...] = acc[...].astype(out_dtype)

    return _gcall(body, name, (gj, gi, gk) if j_outer else (gi, gj, gk), [a_spec, b_spec], o_spec, _sds((m, n), out_dtype), (a, b),
                  scratch=[pltpu.VMEM((tm, tn), F32)] if gk > 1 else [], sem=("parallel", "parallel", "arbitrary"), side=side)


def _stream_block(i, x_ref, meta_ref):
    d = x_ref.shape[-1]
    first = jnp.concatenate([jnp.zeros((LEAD, d), F32), meta_ref[...]], axis=0)
    return jnp.where(i == 0, first, x_ref[...])


def _norm(xb):
    mu = jnp.mean(xb, axis=-1, keepdims=True)
    xc = xb - mu
    var = jnp.mean(xc * xc, axis=-1, keepdims=True)
    rstd = lax.rsqrt(var + LN_EPS)
    return xc * rstd, rstd


def _ln_bwd_rows(dy, xhat, rstd, g):
    dyg = dy * g
    m1 = jnp.mean(dyg, axis=-1, keepdims=True)
    m2 = jnp.mean(dyg * xhat, axis=-1, keepdims=True)
    return rstd * (dyg - m1 - xhat * m2)


def _ln_in_fwd(x2d, meta, g, b, nb, side=None):
    seq, d = x2d.shape

    def body(x_ref, meta_ref, g_ref, b_ref, h_ref, hb_ref):
        xb = _stream_block(pl.program_id(0), x_ref, meta_ref)
        xhat, _ = _norm(xb)
        y = xhat * g_ref[...] + b_ref[...]
        h_ref[...] = y
        hb_ref[...] = y.astype(BF16)

    row = pl.BlockSpec((BLK, d), lambda i: (i, 0))
    vec = pl.BlockSpec((1, d), lambda i: (0, 0))
    return _gcall(body, "ln_in_fwd", (nb,),
                  [pl.BlockSpec((BLK, d), lambda i: (jnp.maximum(i - 1, 0), 0)), pl.BlockSpec((N_META, d), lambda i: (0, 0)), vec, vec],
                  [row, row], [_sds((nb * BLK, d), F32), _sds((nb * BLK, d), BF16)], (x2d, meta, g, b), sem=("parallel",), side=side)


def _res_ln_fwd(h, z, g, b, side=None):
    tp, d = h.shape

    def body(h_ref, z_ref, g_ref, b_ref, r_ref, y_ref, yb_ref):
        r = DN_ALPHA * h_ref[...] + z_ref[...]
        xhat, _ = _norm(r)
        y = xhat * g_ref[...] + b_ref[...]
        r_ref[...] = r
        y_ref[...] = y
        yb_ref[...] = y.astype(BF16)

    row = pl.BlockSpec((BLK, d), lambda i: (i, 0))
    vec = pl.BlockSpec((1, d), lambda i: (0, 0))
    return _gcall(body, "res_ln1_fwd", (tp // BLK,), [row, row, vec, vec], [row, row, row],
                  [_sds((tp, d), F32), _sds((tp, d), F32), _sds((tp, d), BF16)], (h, z, g, b), sem=("parallel",), side=side)


def _final_ln_loss(h1, z2, g, b, tgt):
    tp, d = h1.shape

    def body(h_ref, z_ref, g_ref, b_ref, t_ref, dr_ref, drb_ref, loss_ref, dg_ref, db_ref):
        i = pl.program_id(0)
        r = DN_ALPHA * h_ref[...] + z_ref[...]
        xhat, rstd = _norm(r)
        y = xhat * g_ref[...] + b_ref[...]
        err = jnp.where(i >= 1, y - t_ref[...], 0.0)
        dy = err * (1.0 / d)
        dr = _ln_bwd_rows(dy, xhat, rstd, g_ref[...])
        dr_ref[...] = dr
        drb_ref[...] = dr.astype(BF16)

        @pl.when(i == 0)
        def _():
            loss_ref[...] = jnp.zeros_like(loss_ref)
            dg_ref[...] = jnp.zeros_like(dg_ref)
            db_ref[...] = jnp.zeros_like(db_ref)

        loss_ref[...] += 0.5 * jnp.sum(jnp.sum(err * err, axis=-1, keepdims=True) * (1.0 / d), axis=0, keepdims=True)
        dg_ref[...] += jnp.sum(dy * xhat, axis=0, keepdims=True)
        db_ref[...] += jnp.sum(dy, axis=0, keepdims=True)

    row = pl.BlockSpec((BLK, d), lambda i: (i, 0))
    vec = pl.BlockSpec((1, d), lambda i: (0, 0))
    return _gcall(body, "final_ln_loss", (tp // BLK,),
                  [row, row, vec, vec, pl.BlockSpec((BLK, d), lambda i: (jnp.maximum(i - 1, 0), 0))],
                  [row, row, pl.BlockSpec((8, LANES), lambda i: (0, 0)), vec, vec],
                  [_sds((tp, d), F32), _sds((tp, d), BF16), _sds((8, LANES), F32), _sds((1, d), F32), _sds((1, d), F32)],
                  (h1, z2, g, b, tgt), sem=("arbitrary",))


def _ln1_bwd(d_res, d_mm, r, g, side=None):
    tp, d = r.shape

    def body(a_ref, m_ref, r_ref, g_ref, dr_ref, drb_ref, dg_ref, db_ref):
        dy = DN_ALPHA * a_ref[...] + m_ref[...]
        xhat, rstd = _norm(r_ref[...])
        dr = _ln_bwd_rows(dy, xhat, rstd, g_ref[...])
        dr_ref[...] = dr
        drb_ref[...] = dr.astype(BF16)

        @pl.when(pl.program_id(0) == 0)
        def _():
            dg_ref[...] = jnp.zeros_like(dg_ref)
            db_ref[...] = jnp.zeros_like(db_ref)

        dg_ref[...] += jnp.sum(dy * xhat, axis=0, keepdims=True)
        db_ref[...] += jnp.sum(dy, axis=0, keepdims=True)

    row = pl.BlockSpec((BLK, d), lambda i: (i, 0))
    vec = pl.BlockSpec((1, d), lambda i: (0, 0))
    return _gcall(body, "ln1_bwd", (tp // BLK,), [row, row, row, vec], [row, row, vec, vec],
                  [_sds((tp, d), F32), _sds((tp, d), BF16), _sds((1, d), F32), _sds((1, d), F32)], (d_res, d_mm, r, g),
                  sem=("arbitrary",), side=side)


def _ln_in_bwd(d_res, d_mm, x2d, meta, g, side=None):
    seq, d = x2d.shape
    nb = d_res.shape[0] // BLK

    def body(a_ref, m_ref, x_ref, meta_ref, g_ref, gx_ref, gm_ref, dg_ref, db_ref):
        i = pl.program_id(0)
        dy = DN_ALPHA * a_ref[...] + m_ref[...]
        xhat, rstd = _norm(_stream_block(i, x_ref, meta_ref))
        dx = _ln_bwd_rows(dy, xhat, rstd, g_ref[...])
        gx_ref[...] = dx

        @pl.when(i == 0)
        def _():
            gm_ref[...] = dx[LEAD:, :]
            dg_ref[...] = jnp.zeros_like(dg_ref)
            db_ref[...] = jnp.zeros_like(db_ref)

        dg_ref[...] += jnp.sum(dy * xhat, axis=0, keepdims=True)
        db_ref[...] += jnp.sum(dy, axis=0, keepdims=True)

    row = pl.BlockSpec((BLK, d), lambda i: (i, 0))
    xrow = pl.BlockSpec((BLK, d), lambda i: (jnp.maximum(i - 1, 0), 0))
    vec = pl.BlockSpec((1, d), lambda i: (0, 0))
    met = pl.BlockSpec((N_META, d), lambda i: (0, 0))
    return _gcall(body, "ln_in_bwd", (nb,), [row, row, xrow, met, vec], [xrow, met, vec, vec],
                  [_sds((seq, d), F32), _sds((N_META, d), F32), _sds((1, d), F32), _sds((1, d), F32)], (d_res, d_mm, x2d, meta, g),
                  sem=("arbitrary",), side=side)


def _rope_tables(tp):
    half = ROPE_DIM // 2
    inv_freq = ROPE_THETA ** (-jnp.arange(half, dtype=F32) * 2.0 / ROPE_DIM)
    pos = (jnp.arange(tp) - LEAD).astype(F32)
    ang = pos[:, None] * inv_freq[None, :]
    cos, sin = jnp.cos(ang), jnp.sin(ang)
    ones = jnp.ones((tp, HEAD_DIM - ROPE_DIM), F32)
    cos_h = jnp.concatenate([cos, cos, ones], axis=1)
    sin_h = jnp.concatenate([-sin, sin, 0.0 * ones], axis=1)
    reps = LANES // HEAD_DIM
    return jnp.tile(cos_h, (1, reps)), jnp.tile(sin_h, (1, reps))


def _rope_partner(x):
    half = ROPE_DIM // 2
    lane = lax.broadcasted_iota(jnp.int32, x.shape, 1) % HEAD_DIM
    upper = jnp.where(lane < ROPE_DIM, pltpu.roll(x, half, 1), 0.0)
    return jnp.where(lane < half, pltpu.roll(x, LANES - half, 1), upper)


def _rope_fwd(proj, cos, sin, n_rot, width, side=None):
    tp = proj.shape[0]

    def body(p_ref, c_ref, s_ref, o_ref):
        c, s = c_ref[...], s_ref[...]
        for j in range(width // LANES):
            sl = slice(j * LANES, (j + 1) * LANES)
            xj = p_ref[:, sl]
            if j < n_rot:
                xj = xj * c + _rope_partner(xj) * s
            o_ref[:, sl] = xj.astype(BF16)

    tab = pl.BlockSpec((BLK, LANES), lambda i: (i, 0))
    blk = pl.BlockSpec((BLK, width), lambda i: (i, 0))
    return _gcall(body, "rope_fwd", (tp // BLK,), [blk, tab, tab], blk, _sds((tp, width), BF16), (proj, cos, sin), sem=("parallel",), side=side)


def _rope_bwd(dq, dk_cur, dk_prev, dk_meta, dv_cur, dv_prev, dv_meta, cos, sin, side=None):
    tp, aw = dq.shape
    kw = dk_cur.shape[1]
    nb = tp // BLK

    def body(dq_ref, kc_ref, kp_ref, km_ref, vc_ref, vp_ref, vm_ref, c_ref, s_ref, o_ref):
        i = pl.program_id(0)
        c, s = c_ref[...], s_ref[...]
        has_next = i + 1 < nb

        def unrot(g):
            return g * c + _rope_partner(g * s)

        def kv_sum(cur, prv, met):
            return cur[...] + jnp.where(has_next, prv[...], 0.0) + jnp.where(i == 0, met[...], 0.0)

        for j in range(aw // LANES):
            sl = slice(j * LANES, (j + 1) * LANES)
            o_ref[:, sl] = unrot(dq_ref[:, sl]).astype(BF16)
        dk = kv_sum(kc_ref, kp_ref, km_ref)
        dv = kv_sum(vc_ref, vp_ref, vm_ref)
        for j in range(kw // LANES):
            sl = slice(j * LANES, (j + 1) * LANES)
            o_ref[:, aw + j * LANES:aw + (j + 1) * LANES] = unrot(dk[:, sl]).astype(BF16)
            o_ref[:, aw + kw + j * LANES:aw + kw + (j + 1) * LANES] = dv[:, sl].astype(BF16)

    cur = pl.BlockSpec((BLK, kw), lambda i: (i, 0))
    nxt = pl.BlockSpec((BLK, kw), lambda i: (jnp.minimum(i + 1, nb - 1), 0))
    met = pl.BlockSpec((BLK, kw), lambda i: (0, 0))
    tab = pl.BlockSpec((BLK, LANES), lambda i: (i, 0))
    return _gcall(body, "rope_bwd", (nb,), [pl.BlockSpec((BLK, aw), lambda i: (i, 0)), cur, nxt, met, cur, nxt, met, tab, tab],
                  pl.BlockSpec((BLK, aw + 2 * kw), lambda i: (i, 0)), _sds((tp, aw + 2 * kw), BF16),
                  (dq, dk_cur, dk_prev, dk_meta, dv_cur, dv_prev, dv_meta, cos, sin), sem=("parallel",), side=side)


KV_PER_STEP = 1


def _attn_probs(n, q, kcat, sink, grp):
    scale = HEAD_DIM ** -0.5
    qs = q.reshape(grp * BLK, HEAD_DIM)
    s = lax.dot_general(qs, kcat, _DOT_DIMS["nt"], preferred_element_type=F32) * scale
    s = s.reshape(grp, BLK, 3 * BLK)
    r = lax.broadcasted_iota(jnp.int32, (1, BLK, 3 * BLK), 1)
    j = lax.broadcasted_iota(jnp.int32, (1, BLK, 3 * BLK), 2)
    q_idx = n * BLK + r
    meta_ok = (j >= LEAD) & (j < BLK) & (q_idx >= j)
    k_idx = (n - 1) * BLK + (j - BLK)
    diff = q_idx - k_idx
    band_ok = (j >= BLK) & (diff >= 0) & (diff < BLK) & (k_idx >= LEAD + N_META)
    s = jnp.where(meta_ok | band_ok, s, NEG_INF)
    m = jnp.maximum(jnp.max(s, axis=-1, keepdims=True), sink)
    p = jnp.exp(s - m)
    e_sink = jnp.exp(sink - m)
    inv = 1.0 / (jnp.sum(p, axis=-1, keepdims=True) + e_sink)
    return qs, p * inv, e_sink * inv


def _attn_specs(grp, hp):
    qspec = pl.BlockSpec((hp * grp, BLK, HEAD_DIM), lambda kk, n: (kk, n, 0))
    kmeta = pl.BlockSpec((hp, BLK, HEAD_DIM), lambda kk, n: (kk, 0, 0))
    kprev = pl.BlockSpec((hp, BLK, HEAD_DIM), lambda kk, n: (kk, jnp.maximum(n - 1, 0), 0))
    kcur = pl.BlockSpec((hp, BLK, HEAD_DIM), lambda kk, n: (kk, n, 0))
    sink = pl.BlockSpec((hp, grp, BLK, 1), lambda kk, n: (kk, 0, 0, 0))
    return qspec, kmeta, kprev, kcur, sink


def _cat3(a_ref, b_ref, c_ref, h):
    return jnp.concatenate([a_ref[h], b_ref[h], c_ref[h]], axis=0)


def _attn_fwd(q_hm, k_hm, v_hm, sink4, side=None):
    nq, tp, _ = q_hm.shape
    nkv = k_hm.shape[0]
    grp = nq // nkv
    hp = _pick(nkv, KV_PER_STEP)

    def body(q_ref, km_ref, kp_ref, kc_ref, vm_ref, vp_ref, vc_ref, sink_ref, o_ref):
        n = pl.program_id(1)
        for h in range(hp):
            heads = slice(h * grp, (h + 1) * grp)
            _, pn, _ = _attn_probs(n, q_ref[heads], _cat3(km_ref, kp_ref, kc_ref, h), sink_ref[h], grp)
            o = jnp.dot(pn.reshape(grp * BLK, 3 * BLK).astype(BF16), _cat3(vm_ref, vp_ref, vc_ref, h), preferred_element_type=F32)
            o_ref[heads] = o.reshape(grp, BLK, HEAD_DIM).astype(BF16)

    qspec, kmeta, kprev, kcur, sink = _attn_specs(grp, hp)
    return _gcall(body, "attn_fwd", (nkv // hp, tp // BLK), [qspec, kmeta, kprev, kcur, kmeta, kprev, kcur, sink], qspec,
                  _sds((nq, tp, HEAD_DIM), BF16), (q_hm, k_hm, k_hm, k_hm, v_hm, v_hm, v_hm, sink4), sem=("parallel", "parallel"), side=side)


def _attn_bwd(q_hm, k_hm, v_hm, sink4, do_hm, side=None):
    nq, tp, _ = q_hm.shape
    nkv = k_hm.shape[0]
    grp = nq // nkv
    scale = HEAD_DIM ** -0.5
    hp = _pick(nkv, KV_PER_STEP)

    def body(q_ref, km_ref, kp_ref, kc_ref, vm_ref, vp_ref, vc_ref, sink_ref, do_ref,
             dq_ref, dkc_ref, dkp_ref, dkm_ref, dvc_ref, dvp_ref, dvm_ref, dsk_ref):
        n = pl.program_id(1)

        @pl.when(n == 0)
        def _():
            dkm_ref[...] = jnp.zeros_like(dkm_ref)
            dvm_ref[...] = jnp.zeros_like(dvm_ref)
            dsk_ref[...] = jnp.zeros_like(dsk_ref)

        for h in range(hp):
            heads = slice(h * grp, (h + 1) * grp)
            kcat, vcat = _cat3(km_ref, kp_ref, kc_ref, h), _cat3(vm_ref, vp_ref, vc_ref, h)
            qs, pn, p_sink = _attn_probs(n, q_ref[heads], kcat, sink_ref[h], grp)
            pn2 = pn.reshape(grp * BLK, 3 * BLK)
            pnb = pn2.astype(BF16)
            dob = do_ref[heads].reshape(grp * BLK, HEAD_DIM).astype(BF16)
            dp = lax.dot_general(dob, vcat, _DOT_DIMS["nt"], preferred_element_type=F32)
            delta = jnp.sum(pn2 * dp, axis=-1, keepdims=True)
            ds = (pn2 * (dp - delta) * scale).astype(BF16)
            dq_ref[heads] = jnp.dot(ds, kcat, preferred_element_type=F32).reshape(grp, BLK, HEAD_DIM)
            dk = lax.dot_general(ds, qs, _DOT_DIMS["tn"], preferred_element_type=F32)
            dv = lax.dot_general(pnb, dob, _DOT_DIMS["tn"], preferred_element_type=F32)
            dkp_ref[h] = dk[BLK:2 * BLK]
            dkc_ref[h] = dk[2 * BLK:]
            dvp_ref[h] = dv[BLK:2 * BLK]
            dvc_ref[h] = dv[2 * BLK:]
            dsk = -jnp.sum(p_sink * delta.reshape(grp, BLK, 1), axis=1, keepdims=True)
            dkm_ref[h] += dk[:BLK]
            dvm_ref[h] += dv[:BLK]
            dsk_ref[h] += jnp.broadcast_to(dsk, (grp, BLK, 1))

    qspec, kmeta, kprev, kcur, sink = _attn_specs(grp, hp)
    kv_shape = _sds((nkv, tp, HEAD_DIM), F32)
    meta_shape = _sds((nkv, BLK, HEAD_DIM), F32)
    return _gcall(body, "attn_bwd", (nkv // hp, tp // BLK), [qspec, kmeta, kprev, kcur, kmeta, kprev, kcur, sink, qspec],
                  [qspec, kcur, kcur, kmeta, kcur, kcur, kmeta, sink],
                  [_sds((nq, tp, HEAD_DIM), F32), kv_shape, kv_shape, meta_shape, kv_shape, kv_shape, meta_shape,
                   _sds((nkv, grp, BLK, 1), F32)],
                  (q_hm, k_hm, k_hm, k_hm, v_hm, v_hm, v_hm, sink4, do_hm), sem=("parallel", "arbitrary"), side=side)


def _pool_coef(row_blk, col_blk, w):
    r = lax.broadcasted_iota(jnp.int32, (BLK, BLK), 0)
    j = lax.broadcasted_iota(jnp.int32, (BLK, BLK), 1)
    t = row_blk * BLK + r - LEAD
    tj = col_blk * BLK + j - LEAD
    dist = t - tj
    inwin = (dist >= 0) & (dist < w) & (tj >= 0)
    count = jnp.maximum(jnp.minimum(t + 1, w), 1).astype(F32)
    return jnp.where(inwin, 1.0 / count, 0.0) - jnp.where((dist == 0) & (tj >= 0), 1.0, 0.0)


def _pool_fwd(proj, wg, scale, u_off, pool_w, side=None):
    tp = proj.shape[0]
    gw = pool_w // N_GRP
    nb = tp // BLK
    cb = u_off // gw

    def body(up_ref, uc_ref, wg_ref, sc_ref, pooled_ref, mx_ref, pm_ref):
        n, g = pl.program_id(0), pl.program_id(1)
        w = jnp.left_shift(2, g)
        pooled = (jnp.dot(_pool_coef(n, n - 1, w), up_ref[...], precision=HI, preferred_element_type=F32)
                  + jnp.dot(_pool_coef(n, n, w), uc_ref[...], precision=HI, preferred_element_type=F32))
        pb = pooled.astype(BF16)
        mx = jnp.dot(pb, wg_ref[...], preferred_element_type=F32)
        pooled_ref[...] = pb
        mx_ref[...] = mx
        pm_ref[...] = (mx * sc_ref[...]).astype(BF16)

    blk = pl.BlockSpec((BLK, gw), lambda n, g: (n, g))
    return _gcall(body, "pool_fwd", (nb, N_GRP),
                  [pl.BlockSpec((BLK, gw), lambda n, g: (jnp.maximum(n - 1, 0), cb + g)),
                   pl.BlockSpec((BLK, gw), lambda n, g: (n, cb + g)),
                   pl.BlockSpec((None, gw, gw), lambda n, g: (g, 0, 0)),
                   pl.BlockSpec((1, gw), lambda n, g: (0, g))],
                  [blk, blk, blk], [_sds((tp, pool_w), BF16), _sds((tp, pool_w), F32), _sds((tp, pool_w), BF16)],
                  (proj, proj, wg, scale), sem=("parallel", "parallel"), side=side)


def _pool_bwd_mix(d_pm, mx, pooled, wg, scale, side=None):
    tp, pool_w = d_pm.shape
    gw = pool_w // N_GRP

    def body(d_ref, mx_ref, pl_ref, wg_ref, sc_ref, dp_ref, dwg_ref, dsc_ref):
        n = pl.program_id(1)
        d = d_ref[...]
        dmx = (d * sc_ref[...]).astype(BF16)
        dp_ref[...] = lax.dot_general(dmx, wg_ref[...], _DOT_DIMS["nt"], preferred_element_type=F32)

        @pl.when(n == 0)
        def _():
            dwg_ref[...] = jnp.zeros_like(dwg_ref)
            dsc_ref[...] = jnp.zeros_like(dsc_ref)

        dwg_ref[...] += lax.dot_general(pl_ref[...], dmx, _DOT_DIMS["tn"], preferred_element_type=F32)
        dsc_ref[...] += jnp.sum(d * mx_ref[...], axis=0, keepdims=True)

    blk = pl.BlockSpec((BLK, gw), lambda g, n: (n, g))
    wspec = pl.BlockSpec((None, gw, gw), lambda g, n: (g, 0, 0))
    sspec = pl.BlockSpec((1, gw), lambda g, n: (0, g))
    return _gcall(body, "pool_bwd_mix", (N_GRP, tp // BLK), [blk, blk, blk, wspec, sspec], [blk, wspec, sspec],
                  [_sds((tp, pool_w), F32), _sds((N_GRP, gw, gw), F32), _sds((1, pool_w), F32)], (d_pm, mx, pooled, wg, scale),
                  sem=("parallel", "arbitrary"), side=side)


def _pool_bwd_band(dp, side=None):
    tp, pool_w = dp.shape
    gw = pool_w // N_GRP
    nb = tp // BLK

    def body(dc_ref, dn_ref, du_ref):
        n, g = pl.program_id(0), pl.program_id(1)
        w = jnp.left_shift(2, g)
        dnext = jnp.where(n + 1 < nb, dn_ref[...], 0.0)
        du = (lax.dot_general(_pool_coef(n, n, w), dc_ref[...], _DOT_DIMS["tn"], precision=HI, preferred_element_type=F32)
              + lax.dot_general(_pool_coef(n + 1, n, w), dnext, _DOT_DIMS["tn"], precision=HI, preferred_element_type=F32))
        du_ref[...] = du.astype(BF16)

    blk = pl.BlockSpec((BLK, gw), lambda n, g: (n, g))
    return _gcall(body, "pool_bwd_band", (nb, N_GRP), [blk, pl.BlockSpec((BLK, gw), lambda n, g: (jnp.minimum(n + 1, nb - 1), g))],
                  blk, _sds((tp, pool_w), BF16), (dp, dp), sem=("parallel", "parallel"), side=side)


def _gate_tiles(tp, d, g_off):
    tc = _pick(math.gcd(g_off, d), 512, 256, 128)
    tr = _pick(tp, 384, 128)
    return tr, tc


def _mix_fwd(proj, b_gate, a_out, p_out, g_off, side=None):
    tp, d = a_out.shape
    tr, tc = _gate_tiles(tp, d, g_off)
    c0, c1 = g_off // tc, (g_off + d) // tc

    def body(g0_ref, g1_ref, b_ref, a_ref, p_ref, o_ref):
        g0 = jax.nn.sigmoid(g0_ref[...] + b_ref[0:1, :])
        g1 = jax.nn.sigmoid(g1_ref[...] + b_ref[1:2, :])
        o_ref[...] = (g0 * a_ref[...] + g1 * p_ref[...]).astype(BF16)

    blk = pl.BlockSpec((tr, tc), lambda i, j: (i, j))
    return _gcall(body, "mix_fwd", (tp // tr, d // tc),
                  [pl.BlockSpec((tr, tc), lambda i, j: (i, c0 + j)), pl.BlockSpec((tr, tc), lambda i, j: (i, c1 + j)),
                   pl.BlockSpec((2, tc), lambda i, j: (0, j)), blk, blk],
                  blk, _sds((tp, d), BF16), (proj, proj, b_gate, a_out, p_out), sem=("parallel", "parallel"), side=side)


def _mix_bwd(proj, b_gate, a_out, p_out, d_mixed, g_off, side=None):
    tp, d = a_out.shape
    tr, tc = _gate_tiles(tp, d, g_off)
    c0, c1 = g_off // tc, (g_off + d) // tc

    def body(g0_ref, g1_ref, b_ref, a_ref, p_ref, d_ref, da_ref, dp_ref, dl0_ref, dl1_ref, db_ref):
        g0 = jax.nn.sigmoid(g0_ref[...] + b_ref[0:1, :])
        g1 = jax.nn.sigmoid(g1_ref[...] + b_ref[1:2, :])
        dm = d_ref[...]
        da_ref[...] = (dm * g0).astype(BF16)
        dp_ref[...] = (dm * g1).astype(BF16)
        dl0 = dm * a_ref[...] * g0 * (1.0 - g0)
        dl1 = dm * p_ref[...] * g1 * (1.0 - g1)
        dl0_ref[...] = dl0.astype(BF16)
        dl1_ref[...] = dl1.astype(BF16)

        @pl.when(pl.program_id(1) == 0)
        def _():
            db_ref[...] = jnp.zeros_like(db_ref)

        db_ref[...] += jnp.concatenate([jnp.sum(dl0, axis=0, keepdims=True), jnp.sum(dl1, axis=0, keepdims=True)], axis=0)

    blk = pl.BlockSpec((tr, tc), lambda j, i: (i, j))
    big = _sds((tp, d), BF16)
    return _gcall(body, "mix_bwd", (d // tc, tp // tr),
                  [pl.BlockSpec((tr, tc), lambda j, i: (i, c0 + j)), pl.BlockSpec((tr, tc), lambda j, i: (i, c1 + j)),
                   pl.BlockSpec((2, tc), lambda j, i: (0, j)), blk, blk, blk],
                  [blk, blk, blk, blk, pl.BlockSpec((2, tc), lambda j, i: (0, j))], [big, big, big, big, _sds((2, d), F32)],
                  (proj, proj, b_gate, a_out, p_out, d_mixed), sem=("parallel", "arbitrary"), side=side)


SWIGLU_ROWS = 64


def _swiglu_fwd(ff, side=None):
    tp, f2 = ff.shape
    f = f2 // 2
    tr = _pick(tp, SWIGLU_ROWS)

    def body(x_ref, o_ref):
        gate, up = x_ref[:, :f], x_ref[:, f:]
        o_ref[...] = (gate * jax.nn.sigmoid(gate) * up).astype(BF16)

    return _gcall(body, "swiglu_fwd", (tp // tr,), [pl.BlockSpec((tr, f2), lambda i: (i, 0))], pl.BlockSpec((tr, f), lambda i: (i, 0)),
                  _sds((tp, f), BF16), (ff,), sem=("parallel",), side=side)


def _swiglu_bwd(ff, d_act, side=None):
    tp, f2 = ff.shape
    f = f2 // 2
    tr = _pick(tp, SWIGLU_ROWS)

    def body(x_ref, d_ref, o_ref):
        gate, up = x_ref[:, :f], x_ref[:, f:]
        d = d_ref[...]
        sg = jax.nn.sigmoid(gate)
        silu = gate * sg
        o_ref[:, :f] = (d * up * (sg + silu * (1.0 - sg))).astype(BF16)
        o_ref[:, f:] = (d * silu).astype(BF16)

    return _gcall(body, "swiglu_bwd", (tp // tr,), [pl.BlockSpec((tr, f2), lambda i: (i, 0)), pl.BlockSpec((tr, f), lambda i: (i, 0))],
                  pl.BlockSpec((tr, f2), lambda i: (i, 0)), _sds((tp, f2), BF16), (ff, d_act), sem=("parallel",), side=side)


def _tile2(rows, cols, max_bytes=3 << 20):
    tc = _pick(cols, 1024, 640, 512)
    for tr in (512, 344, 256, 128, 64, 32, 16, 8):
        if rows % tr == 0 and tr * tc * 4 <= max_bytes:
            return tr, tc
    return rows, tc


def _cast_into_full(w, win, who, name, side=None):
    r, c = w.shape
    tr, tc = _tile2(r, c)

    def body(who_ref, x_ref, o_ref):
        o_ref[...] = x_ref[...].astype(BF16)

    if win.kind == "col":
        own = pl.BlockSpec((tr, tc), lambda i, j, who_ref: (i, who_ref[1] * (c // tc) + j))
    else:
        own = pl.BlockSpec((tr, tc), lambda i, j, who_ref: (who_ref[1] * (r // tr) + i, j))
    return _gcall(body, name, (r // tr, c // tc), [pl.BlockSpec((tr, tc), lambda i, j, who_ref: (i, j))], own,
                  _sds(win.full_shape, BF16), (w,), sem=("parallel", "parallel"), side=side, prefetch=(who,))


def _adamw(w, g, m, v, name, side=None):
    r, c = w.shape
    tr, tc = _tile2(r, c, 1 << 20)

    def body(w_ref, g_ref, m_ref, v_ref, d_ref, nm_ref, nv_ref, go_ref):
        gg = g_ref[...]
        nm = ADAM_B1 * m_ref[...] + (1.0 - ADAM_B1) * gg
        nv = ADAM_B2 * v_ref[...] + (1.0 - ADAM_B2) * jnp.square(gg)
        m_hat = nm / (1.0 - ADAM_B1 ** ADAM_STEP)
        v_hat = nv / (1.0 - ADAM_B2 ** ADAM_STEP)
        d_ref[...] = -ADAM_LR * (m_hat / (jnp.sqrt(v_hat) + ADAM_EPS) + ADAM_WD * w_ref[...])
        nm_ref[...] = nm
        nv_ref[...] = nv
        go_ref[...] = gg

    blk = pl.BlockSpec((tr, tc), lambda i, j: (i, j))
    shp = _sds((r, c), F32)
    return _gcall(body, name, (r // tr, c // tc), [blk] * 4, [blk] * 4, [shp] * 4, (w, g, m, v), sem=("parallel", "parallel"), side=side)


def _piece_block_index(win, tr, tc):
    r, c = win.shard_shape
    if win.kind == "col":
        return lambda s, h, i, j: (win.row0 // tr + h * (win.half // tr) + i, s * (c // tc) + j)
    return lambda s, h, i, j: ((s * r + win.row0) // tr + h * (win.half // tr) + i, j)


def _chip_sum(g, who):
    pr, pc = g.win.piece_shape
    tr, tc = _tile2(pr, pc)
    full_idx = _piece_block_index(g.src, tr, tc)

    def body(who_ref, g_ref, o_ref, out_ref):
        out_ref[...] = (g_ref[...] + o_ref[...]).astype(BF16)

    slot = pl.BlockSpec((None, tr, tc), lambda s, i, j, who_ref: (s, i, j))
    g.chip = pl.pallas_call(
        body,
        name="chip_sum_" + g.key.replace("@", "_"),
        grid_spec=pltpu.PrefetchScalarGridSpec(
            num_scalar_prefetch=1,
            grid=(N_CHIPS, pr // tr, pc // tc),
            in_specs=[pl.BlockSpec((tr, tc), lambda s, i, j, who_ref: full_idx(s, who_ref[0], i, j)), slot],
            out_specs=slot,
        ),
        out_shape=_sds((N_CHIPS, pr, pc), BF16),
        compiler_params=_cparams(("parallel", "parallel", "parallel")),
    )(who, g.grad, g.other)


def _final_sum(g, shard, who):
    win = g.win
    pr, pc = win.piece_shape
    tr, tc = _tile2(pr, pc)
    full_idx = _piece_block_index(g.src, tr, tc)
    has_prev = shard.arr is not None

    def body(who_ref, g_ref, o_ref, l1_ref, l2_ref, l3_ref, *rest):
        out_ref = rest[-1]
        acc = g_ref[...] + o_ref[...]
        for l_ref in (l1_ref, l2_ref, l3_ref):
            acc = acc + l_ref[...].astype(F32)
        out_ref[...] = acc

    def landed_spec(k):
        return pl.BlockSpec((None, tr, tc), lambda i, j, who_ref: (who_ref[1 + k], i, j))

    in_specs = [pl.BlockSpec((tr, tc), lambda i, j, who_ref: full_idx(who_ref[1], who_ref[0], i, j)),
                pl.BlockSpec((None, tr, tc), lambda i, j, who_ref: (who_ref[1], i, j)),
                landed_spec(1), landed_spec(2), landed_spec(3)]
    args = [who, g.grad, g.other, g.landed, g.landed, g.landed]
    if has_prev:
        in_specs.append(ANY)
        args.append(shard.arr)
    shard.arr = pl.pallas_call(
        body,
        name="final_sum_" + g.key.replace("@", "_"),
        grid_spec=pltpu.PrefetchScalarGridSpec(
            num_scalar_prefetch=1,
            grid=(pr // tr, pc // tc),
            in_specs=in_specs,
            out_specs=pl.BlockSpec((tr, tc), lambda i, j, who_ref: (win.row0 // tr + who_ref[0] * (pr // tr) + i, j)),
        ),
        out_shape=_sds(win.shard_shape, F32),
        input_output_aliases={6: 0} if has_prev else {},
        compiler_params=_cparams(("parallel", "parallel")),
    )(*args)


def _gather_small(packed):
    r, c = packed.shape

    def body(in_ref, out_ref, send_sems, recv_sems):
        x, y, c_ = _coords()
        s_me = 2 * x + y
        out_ref[s_me] = in_ref[...]
        copies = []
        for j, (ox, oy) in enumerate(_other_chips(x, y)):
            cp = pltpu.make_async_remote_copy(src_ref=in_ref, dst_ref=out_ref.at[s_me], send_sem=send_sems.at[j],
                                              recv_sem=recv_sems.at[j], device_id=(ox, oy, c_), device_id_type=MESH)
            cp.start()
            copies.append(cp)
        for j, (ox, oy) in enumerate(_other_chips(x, y)):
            copies[j].wait_send()
            pltpu.make_async_remote_copy(src_ref=in_ref, dst_ref=out_ref.at[2 * ox + oy], send_sem=send_sems.at[j],
                                         recv_sem=recv_sems.at[j], device_id=(x, y, c_), device_id_type=MESH).wait_recv()

    return pl.pallas_call(
        body,
        name="gather_small",
        in_specs=[VMEM_FULL],
        out_specs=VMEM_FULL,
        out_shape=_sds((N_CHIPS, r, c), F32),
        scratch_shapes=[pltpu.SemaphoreType.DMA((3,)), pltpu.SemaphoreType.DMA((3,))],
    )(packed)


def _all_reduce_small(packed):
    r, c = packed.shape

    def body(in_ref, out_ref, slots, send_sems, recv_sems):
        x, y, c_ = _coords()
        me = 4 * x + 2 * y + c_
        slots[me] = in_ref[...]
        copies = []
        for k in range(1, N_DEV):
            peer = me ^ k
            cp = pltpu.make_async_remote_copy(src_ref=in_ref, dst_ref=slots.at[me], send_sem=send_sems.at[k - 1],
                                              recv_sem=recv_sems.at[k - 1],
                                              device_id=(peer // 4, (peer // 2) % 2, peer % 2), device_id_type=MESH)
            cp.start()
            copies.append(cp)
        for k in range(1, N_DEV):
            copies[k - 1].wait_send()
            pltpu.make_async_remote_copy(src_ref=in_ref, dst_ref=slots.at[me ^ k], send_sem=send_sems.at[k - 1],
                                         recv_sem=recv_sems.at[k - 1], device_id=(x, y, c_), device_id_type=MESH).wait_recv()
        acc = slots[0]
        for d in range(1, N_DEV):
            acc = acc + slots[d]
        out_ref[...] = acc

    return pl.pallas_call(
        body,
        name="all_reduce_small",
        in_specs=[VMEM_FULL],
        out_specs=VMEM_FULL,
        out_shape=_sds((r, c), F32),
        scratch_shapes=[pltpu.VMEM((N_DEV, r, c), F32), pltpu.SemaphoreType.DMA((N_DEV - 1,)), pltpu.SemaphoreType.DMA((N_DEV - 1,))],
    )(packed)


def _rows_of(a, width):
    flat = a.reshape(-1)
    n = -(-flat.shape[0] // width) * width
    return jnp.pad(flat, (0, n - flat.shape[0])).reshape(-1, width)


def _pad_rows(a, mult=8):
    n = -(-a.shape[0] // mult) * mult
    return jnp.pad(a, ((0, n - a.shape[0]), (0, 0)))


def _heads_major(a, nh):
    tp = a.shape[0]
    return a.reshape(tp, nh, HEAD_DIM).transpose(1, 0, 2)


def _heads_minor(a):
    nh, tp, hd = a.shape
    return a.transpose(1, 0, 2).reshape(tp, nh * hd)


def kernel(x, meta_tokens, ln_in_g, ln_in_b, w_in, b_gate, attn_sinks, w_attn_up, w_pool_grp, pool_scale, w_pool_up, w_out, ln1_g, ln1_b, w_ffn_in, w_ffn_down, ln2_g, ln2_b, loss_target, m_meta_tokens, m_ln_in_g, m_ln_in_b, m_w_in, m_b_gate, m_attn_sinks, m_w_attn_up, m_w_pool_grp, m_pool_scale, m_w_pool_up, m_w_out, m_ln1_g, m_ln1_b, m_w_ffn_in, m_w_ffn_down, m_ln2_g, m_ln2_b, v_meta_tokens, v_ln_in_g, v_ln_in_b, v_w_in, v_b_gate, v_attn_sinks, v_w_attn_up, v_w_pool_grp, v_pool_scale, v_w_pool_up, v_w_out, v_ln1_g, v_ln1_b, v_w_ffn_in, v_w_ffn_down, v_ln2_g, v_ln2_b):
    seq, d = x.shape[1], x.shape[2]
    tp = LEAD + N_META + seq
    nb = tp // BLK
    nq = attn_sinks.shape[1]
    grp = nq // N_KV
    attn_w = nq * HEAD_DIM
    kv_w = N_KV * HEAD_DIM
    qkv_w = attn_w + 2 * kv_w
    pool_w = pool_scale.shape[1]
    gw = pool_w // N_GRP
    g_off = qkv_w + pool_w
    dc = d // N_CHIPS
    cx, cy, cc = _coords()
    s_me = 2 * cx + cy
    who = jnp.stack([cc, s_me, (s_me + 1) % N_CHIPS, (s_me + 2) % N_CHIPS, (s_me + 3) % N_CHIPS]).astype(jnp.int32)

    names = ["w_in", "w_attn_up", "w_pool_grp", "w_pool_up", "w_out", "w_ffn_in", "w_ffn_down"]
    kinds = dict(w_in="col", w_attn_up="col", w_pool_grp="row", w_pool_up="col", w_out="row", w_ffn_in="col", w_ffn_down="row")
    grp_shard = (N_GRP * (gw // N_CHIPS), gw)
    big_w = dict(w_in=w_in[0], w_attn_up=w_attn_up[0], w_pool_grp=w_pool_grp[0].reshape(grp_shard), w_pool_up=w_pool_up[0],
                 w_out=w_out[0], w_ffn_in=w_ffn_in[0], w_ffn_down=w_ffn_down[0])
    big_m = dict(w_in=m_w_in[0], w_attn_up=m_w_attn_up[0], w_pool_grp=m_w_pool_grp[0].reshape(grp_shard), w_pool_up=m_w_pool_up[0],
                 w_out=m_w_out[0], w_ffn_in=m_w_ffn_in[0], w_ffn_down=m_w_ffn_down[0])
    big_v = dict(w_in=v_w_in[0], w_attn_up=v_w_attn_up[0], w_pool_grp=v_w_pool_grp[0].reshape(grp_shard), w_pool_up=v_w_pool_up[0],
                 w_out=v_w_out[0], w_ffn_in=v_w_ffn_in[0], w_ffn_down=v_w_ffn_down[0])
    small_rows = _pad_rows(jnp.concatenate([meta_tokens, b_gate[0]], axis=0))
    gathered = _gather_small(small_rows)
    gathered = gathered.transpose(1, 0, 2).reshape(small_rows.shape[0], d)
    meta_full, b_gate_full = gathered[:N_META], gathered[N_META:N_META + 2]

    W = {}

    def cast(n, side=None):
        win = _Win(kinds[n], big_w[n].shape)
        W[n] = _Weight(n, win, _cast_into_full(big_w[n], win, who, "cast_" + n, side=side))

    def whole(*ns):
        return [(W[n], W[n].win) for n in ns]

    def legs(first=(), second=(), third=()):
        ops = [_GatherD2d(third, DIAGONAL)] if third else []
        ops += [_GatherRing(second), _GatherD2d(second, NEIGHBOURS)] if second else []
        ops += [_GatherIci(first)] if first else []
        return _Side(ops)

    mid = ("w_attn_up", "w_pool_grp", "w_pool_up", "w_out")
    for n in ("w_in",) + mid:
        cast(n)
    cast("w_ffn_in", side=legs(first=whole("w_in")))
    cast("w_ffn_down", side=legs(second=whole("w_in")))
    wins = {n: W[n].win for n in names}
    ffn_in_parts = [(W["w_ffn_in"], win) for win in W["w_ffn_in"].win.split(4)]
    x2d, tgt2d = x[0], loss_target[0]
    g_in, b_in = ln_in_g.reshape(1, d), ln_in_b.reshape(1, d)
    h0, h0b = _ln_in_fwd(x2d, meta_full, g_in, b_in, nb, side=legs(first=whole(*mid), third=whole("w_in")))
    proj = _mm(h0b, W["w_in"].full, "nn", F32, 1408, 512, 4096, "mm_proj", side=legs(first=ffn_in_parts[0:3], second=whole(*mid)))
    cos, sin = _rope_tables(tp)
    n_rot = (attn_w + kv_w) // LANES
    qkv = _rope_fwd(proj, cos, sin, n_rot, qkv_w, side=legs(third=whole(*mid)))
    q_hm = _heads_major(qkv[:, :attn_w], nq)
    k_hm = _heads_major(qkv[:, attn_w:attn_w + kv_w], N_KV)
    v_hm = _heads_major(qkv[:, attn_w + kv_w:], N_KV)
    sink4 = jnp.broadcast_to(attn_sinks.reshape(N_KV, grp, 1, 1), (N_KV, grp, BLK, 1))
    o_hm = _attn_fwd(q_hm, k_hm, v_hm, sink4, side=legs(first=ffn_in_parts[3:4], second=ffn_in_parts[0:3]))
    o = _heads_minor(o_hm)
    wf_grp = W["w_pool_grp"].full.reshape(N_CHIPS, N_GRP, gw // N_CHIPS, gw).transpose(1, 0, 2, 3).reshape(N_GRP, gw, gw)
    pooled, mx, pm = _pool_fwd(proj, wf_grp, pool_scale, qkv_w, pool_w, side=legs(second=ffn_in_parts[3:4], third=ffn_in_parts[0:3]))
    a_out = _mm(o, W["w_attn_up"].full, "nn", F32, 1408, 1024, 2048, "mm_attn_up", side=legs(third=ffn_in_parts[3:4]))
    p_out = _mm(pm, W["w_pool_up"].full, "nn", F32, 1408, 1024, 2048, "mm_pool_up")
    mixed = _mix_fwd(proj, b_gate_full, a_out, p_out, g_off)
    z1 = _mm(mixed, W["w_out"].full, "nn", F32, 1408, 512, 4096, "mm_out", side=legs(first=whole("w_ffn_down")))
    r1, h1, h1b = _res_ln_fwd(h0, z1, ln1_g, ln1_b, side=legs(second=whole("w_ffn_down")))
    ff = _mm(h1b, W["w_ffn_in"].full, "nn", F32, 1408, 512, 4096, "mm_ffn_in", side=legs(third=whole("w_ffn_down")))
    act = _swiglu_fwd(ff)
    wf_down = W["w_ffn_down"].full
    z2 = _mm(act, wf_down, "nn", F32, 704, 1024, 5504, "mm_ffn_down")
    d_r2, d_r2b, loss_tile, dg2, db2 = _final_ln_loss(h1, z2, ln2_g, ln2_b, tgt2d)

    S = {n: _Shard(n) for n in names}

    def grads_of(name, grad, parts=1):
        return [_Grad(name, win, grad) for win in wins[name].split(parts)]

    def sibling(gs):
        return _ReduceSibling(gs)

    def chips(gs):
        for g in gs:
            _chip_sum(g, who)
        return _ReduceChips(gs)

    def share(gs):
        for g in gs:
            _final_sum(g, S[g.name], who)
        return _ShareReduced([(S[g.name], g.win) for g in gs])

    def row_blocks(name, act_b, d_out, parts, sides):
        out = []
        for k, win in enumerate(wins[name].split(parts)):
            blk = _mm(act_b, d_out, "tn", F32, 1024, 512, tp, "mm_g%s_%d" % (name, k), side=sides(k, out), a_cols=(win.row0, win.nrows))
            out.append(_Grad(name, win, blk, win.row0))
        return out

    g6 = grads_of("w_ffn_down", _mm(act, d_r2b, "tn", F32, 256, 2048, tp, "mm_gw_ffn_down", j_outer=True))
    d_act = _mm(d_r2b, wf_down, "nt", F32, 2112, 256, 4096, "mm_d_act", side=_Side([sibling(g6)]))
    d_ff = _swiglu_bwd(ff, d_act)
    g5 = row_blocks("w_ffn_in", h1b, d_ff, 2, lambda k, done: _Side([chips(g6)] if k == 0 else [sibling(done[0:1]), share(g6)]))
    d_h1_mm = _mm(d_ff, W["w_ffn_in"].full, "nt", F32, 704, 1024, 5504, "mm_d_h1", side=_Side([chips(g5[0:1]), sibling(g5[1:2])]))
    d_r1, d_r1b, dg1, db1 = _ln1_bwd(d_r2, d_h1_mm, r1, ln1_g)
    g4 = grads_of("w_out", _mm(mixed, d_r1b, "tn", F32, 1024, 512, tp, "mm_gw_out", side=_Side([share(g5[0:1])])))
    d_mixed = _mm(d_r1b, W["w_out"].full, "nt", F32, 1408, 512, 4096, "mm_d_mixed", side=_Side([sibling(g4)]))
    d_a, d_p, d_gl0, d_gl1, d_bgate = _mix_bwd(proj, b_gate_full, a_out, p_out, d_mixed, g_off, side=_Side([chips(g4)]))
    g1 = grads_of("w_attn_up", _mm(o, d_a, "tn", F32, 1024, 512, tp, "mm_gw_attn_up", side=_Side([share(g4)])))
    d_o = _mm(d_a, W["w_attn_up"].full, "nt", F32, 1408, 512, 4096, "mm_d_o", side=_Side([sibling(g1)]))
    g3 = grads_of("w_pool_up", _mm(pm, d_p, "tn", F32, 1024, 512, tp, "mm_gw_pool_up", side=_Side([chips(g1)])))
    d_pm = _mm(d_p, W["w_pool_up"].full, "nt", F32, 1408, 512, 4096, "mm_d_pm", side=_Side([sibling(g3)]))
    d_pooled, gw_grp, d_scale = _pool_bwd_mix(d_pm, mx, pooled, wf_grp, pool_scale, side=_Side([chips(g3), share(g1)]))
    gw_grp_sm = gw_grp.reshape(N_GRP, N_CHIPS, gw // N_CHIPS, gw).transpose(1, 0, 2, 3).reshape(N_CHIPS * grp_shard[0], gw)
    g2 = grads_of("w_pool_grp", gw_grp_sm)
    d_u = _pool_bwd_band(d_pooled, side=_Side([sibling(g2), share(g3)]))
    dq_hm, dk_cur, dk_prev, dk_meta, dv_cur, dv_prev, dv_meta, d_sink = _attn_bwd(
        q_hm, k_hm, v_hm, sink4, _heads_major(d_o, nq), side=_Side([chips(g5[1:2] + g2)]))
    d_qkv = _rope_bwd(_heads_minor(dq_hm), _heads_minor(dk_cur), _heads_minor(dk_prev), _heads_minor(dk_meta),
                      _heads_minor(dv_cur), _heads_minor(dv_prev), _heads_minor(dv_meta), cos, sin, side=_Side([share(g2)]))
    d_proj = jnp.concatenate([d_qkv, d_u, d_gl0, d_gl1], axis=1)
    def gw_in_sides(k, done):
        ops = [share(g5[1:2])] if k == 0 else [sibling(done[k - 1:k])]
        if k >= 2:
            ops.append(chips(done[k - 2:k - 1]))
        if k >= 3:
            ops.append(share(done[k - 3:k - 2]))
        return _Side(ops)

    g0 = row_blocks("w_in", h0b, d_proj, 4, gw_in_sides)
    d_h0_mm = _mm(d_proj, W["w_in"].full, "nt", F32, 1408, 1024, 2560, "mm_d_h0", side=_Side([chips(g0[2:3]), sibling(g0[3:4]), share(g0[1:2])]))
    grad_x2d, d_meta, dg_in, db_in = _ln_in_bwd(d_r1, d_h0_mm, x2d, meta_full, g_in, side=_Side([chips(g0[3:4]), share(g0[2:3])]))

    small_parts = [d_meta, d_bgate, dg_in, db_in, dg1, db1, dg2, db2, _rows_of(d_scale, d), _rows_of(d_sink[:, :, 0, 0], d)]
    offs = [0]
    for p in small_parts:
        offs.append(offs[-1] + p.shape[0])
    red = _all_reduce_small(_pad_rows(jnp.concatenate(small_parts, axis=0)))
    r_meta, r_bgate, r_g_in, r_b_in, r_g1, r_b1, r_g2, r_b2, r_scale, r_sink = [red[offs[k]:offs[k + 1]] for k in range(len(small_parts))]
    col0 = s_me * dc
    g_meta = lax.dynamic_slice(r_meta, (0, col0), (N_META, dc))
    g_bgate = lax.dynamic_slice(r_bgate, (0, col0), (2, dc))
    g_scale = r_scale.reshape(-1)[:pool_w]
    g_sink = r_sink.reshape(-1)[:nq]

    upd = {}

    def adamw(n, side=None):
        upd[n] = _adamw(big_w[n], S[n].arr, big_m[n], big_v[n], "adamw_" + n, side=side)

    adamw("w_out", side=_Side([share(g0[3:4])]))
    for n in ("w_ffn_in", "w_ffn_down", "w_attn_up", "w_pool_grp", "w_pool_up", "w_in"):
        adamw(n)

    small_w = [meta_tokens, b_gate[0], ln_in_g, ln_in_b, attn_sinks, pool_scale, ln1_g, ln1_b, ln2_g, ln2_b]
    small_m = [m_meta_tokens, m_b_gate[0], m_ln_in_g, m_ln_in_b, m_attn_sinks, m_pool_scale, m_ln1_g, m_ln1_b, m_ln2_g, m_ln2_b]
    small_v = [v_meta_tokens, v_b_gate[0], v_ln_in_g, v_ln_in_b, v_attn_sinks, v_pool_scale, v_ln1_g, v_ln1_b, v_ln2_g, v_ln2_b]
    small_g = [g_meta, g_bgate, r_g_in, r_b_in, g_sink, g_scale, r_g1, r_b1, r_g2, r_b2]
    small_g = [g.reshape(w.shape) for g, w in zip(small_g, small_w)]

    def pack(parts):
        return _pad_rows(jnp.concatenate([_rows_of(p, dc) for p in parts], axis=0))

    s_delta, s_m, s_v, _ = _adamw(pack(small_w), pack(small_g), pack(small_m), pack(small_v), "adamw_small")

    def unpack(packed):
        out, row = [], 0
        for w in small_w:
            nrow = -(-w.size // dc)
            out.append(packed[row:row + nrow].reshape(-1)[:w.size].reshape(w.shape))
            row += nrow
        return out

    s_delta, s_m, s_v = unpack(s_delta), unpack(s_m), unpack(s_v)

    order = ["meta_tokens", "ln_in_g", "ln_in_b", "w_in", "b_gate", "attn_sinks", "w_attn_up", "w_pool_grp", "pool_scale",
             "w_pool_up", "w_out", "ln1_g", "ln1_b", "w_ffn_in", "w_ffn_down", "ln2_g", "ln2_b"]
    small_names = ["meta_tokens", "b_gate", "ln_in_g", "ln_in_b", "attn_sinks", "pool_scale", "ln1_g", "ln1_b", "ln2_g", "ln2_b"]
    out_shapes = dict(meta_tokens=meta_tokens.shape, ln_in_g=ln_in_g.shape, ln_in_b=ln_in_b.shape, w_in=w_in.shape, b_gate=b_gate.shape,
                      attn_sinks=attn_sinks.shape, w_attn_up=w_attn_up.shape, w_pool_grp=w_pool_grp.shape, pool_scale=pool_scale.shape,
                      w_pool_up=w_pool_up.shape, w_out=w_out.shape, ln1_g=ln1_g.shape, ln1_b=ln1_b.shape, w_ffn_in=w_ffn_in.shape,
                      w_ffn_down=w_ffn_down.shape, ln2_g=ln2_g.shape, ln2_b=ln2_b.shape)
    grads, deltas, new_m, new_v = {}, {}, {}, {}
    for n in names:
        deltas[n], new_m[n], new_v[n], grads[n] = upd[n]
    for k, n in enumerate(small_names):
        grads[n], deltas[n], new_m[n], new_v[n] = small_g[k], s_delta[k], s_m[k], s_v[k]

    loss = lax.psum(loss_tile[0, 0], ("x", "y", "c"))
    outs = [loss, grad_x2d.reshape(x.shape)]
    for group in (grads, deltas, new_m, new_v):
        outs += [group[n].reshape(out_shapes[n]) for n in order]
    return tuple(outs)
```
